```python
import math
import jax, jax.numpy as jnp
from jax import lax
import numpy as np

D_MODEL = 1024
BATCH = 2
SEQ = 16384
DEPTH = 4

GRID_W = 64
CTX_LEN = 256
D_MIX = D_MODEL
F_GROUPS = 4
F_GROUP_DIM = 64
F_WIDTH = F_GROUPS * F_GROUP_DIM
MLA_HEADS = 8
QK_NOPE_DIM = 64
QK_ROPE_DIM = 32
V_HEAD_DIM = 64
MLA_WIDTH = MLA_HEADS * V_HEAD_DIM
Q_LORA_RANK = 256
KV_LORA_RANK = 128
CONV_WIDTH = D_MIX - F_WIDTH - MLA_WIDTH
CONV_KERNEL = 31
D_IN = F_WIDTH + Q_LORA_RANK + KV_LORA_RANK + QK_ROPE_DIM + 2 * CONV_WIDTH
SPLITS = (F_WIDTH,
          F_WIDTH + Q_LORA_RANK,
          F_WIDTH + Q_LORA_RANK + KV_LORA_RANK,
          F_WIDTH + Q_LORA_RANK + KV_LORA_RANK + QK_ROPE_DIM)
D_FF = 4 * D_MODEL
N_MOD = 6
ROPE_BASE = 10000.0
ROPE_AXIS_DIM = QK_ROPE_DIM // 2
ATTN_SCALE = (QK_NOPE_DIM + QK_ROPE_DIM) ** -0.5
Q_BLOCK = 128
EPS = 1e-6

kernel_name = "hybrid_fourier_mla_conformer_dit"


def rms_norm(x, g):
    xf = x.astype(jnp.float32)
    y = xf * lax.rsqrt(jnp.mean(jnp.square(xf), axis=-1, keepdims=True) + EPS)
    return (y * g.astype(jnp.float32)).astype(x.dtype)


def layer_norm(x, g, b):
    xf = x.astype(jnp.float32)
    mu = jnp.mean(xf, axis=-1, keepdims=True)
    xc = xf - mu
    var = jnp.mean(jnp.square(xc), axis=-1, keepdims=True)
    y = xc * lax.rsqrt(var + EPS) * g.astype(jnp.float32) + b.astype(jnp.float32)
    return y.astype(x.dtype)


def modulate(h, shift, scale):
    return h * (1 + scale) + shift


def axial_rope_angles(n_tokens):
    rows = n_tokens // GRID_W
    t = jnp.arange(rows * GRID_W)
    row = (t // GRID_W).astype(jnp.float32)
    col = (t % GRID_W).astype(jnp.float32)
    inv = ROPE_BASE ** (-jnp.arange(0, ROPE_AXIS_DIM, 2, dtype=jnp.float32) / ROPE_AXIS_DIM)
    ang = jnp.concatenate([row[:, None] * inv, col[:, None] * inv], axis=-1)
    return jnp.cos(ang), jnp.sin(ang)


def apply_rope(x, cos, sin):
    half = QK_ROPE_DIM // 2
    x1, x2 = x[..., :half], x[..., half:]
    cos = cos.astype(x.dtype)
    sin = sin.astype(x.dtype)
    return jnp.concatenate([x1 * cos - x2 * sin, x1 * sin + x2 * cos], axis=-1)


def fourier_mix(zf, w_f):
    B, N, _ = zf.shape
    u = zf.reshape(B, N, F_GROUPS, F_GROUP_DIM).astype(jnp.float32)
    y = jnp.fft.fftn(u, axes=(1, 3), norm="ortho").real
    return y.reshape(B, N, F_WIDTH).astype(zf.dtype) @ w_f


def conformer_conv(zg, w_dw, b_dw, ln_g, ln_b, w_pw2):
    a, g = jnp.split(zg, 2, axis=-1)
    u = a * jax.nn.sigmoid(g)
    pad = CONV_KERNEL // 2
    u = lax.conv_general_dilated(u, w_dw[:, None, :], window_strides=(1,),
                                 padding=((pad, pad),),
                                 dimension_numbers=("NWC", "WIO", "NWC"),
                                 feature_group_count=CONV_WIDTH) + b_dw
    u = jax.nn.silu(layer_norm(u, ln_g, ln_b))
    return u @ w_pw2


def mla_queries(cq, q_g, w_uq):
    B, N, _ = cq.shape
    q = (rms_norm(cq, q_g) @ w_uq).reshape(B, N, MLA_HEADS, QK_NOPE_DIM + QK_ROPE_DIM)
    return q[..., :QK_NOPE_DIM], q[..., QK_NOPE_DIM:]


def mla_keys_values(ckv, kv_g, w_ukv):
    B, N, _ = ckv.shape
    kv = (rms_norm(ckv, kv_g) @ w_ukv).reshape(B, N, MLA_HEADS, QK_NOPE_DIM + V_HEAD_DIM)
    return kv[..., :QK_NOPE_DIM], kv[..., QK_NOPE_DIM:]


def mla_attend(qn, qr, kn, kr, v):
    s = jnp.einsum("bqhd,bkhd->bhqk", qn, kn) + jnp.einsum("bqhr,bkr->bhqk", qr, kr)
    p = jax.nn.softmax(s.astype(jnp.float32) * ATTN_SCALE, axis=-1).astype(v.dtype)
    return jnp.einsum("bhqk,bkhd->bqhd", p, v)


def blocked_mla_attend(qn, qr, kn, kr, v):
    B, N, H, _ = qn.shape
    nblk = N // Q_BLOCK

    def to_blocks(q):
        return q.reshape(B, nblk, Q_BLOCK, H, q.shape[-1]).transpose(1, 0, 2, 3, 4)

    out = lax.map(lambda qs: mla_attend(qs[0], qs[1], kn, kr, v), (to_blocks(qn), to_blocks(qr)))
    return out.transpose(1, 0, 2, 3, 4).reshape(B, N, H * V_HEAD_DIM)


def setup_inputs(seed: int = 0) -> dict:
    key = jax.random.key(seed)
    ks = jax.random.split(key, 24)
    f32 = jnp.float32
    L = DEPTH

    def nrm(k, shape, scale):
        return jax.random.normal(k, shape, f32) * scale

    return {
        "x": nrm(ks[0], (BATCH, SEQ, D_MODEL), 1.0),
        "c": nrm(ks[1], (BATCH, D_MODEL), 1.0),
        "ctx": nrm(ks[2], (BATCH, CTX_LEN, D_MODEL), 1.0),
        "c_ctx": nrm(ks[3], (D_MODEL,), 1.0),
        "w_mod": nrm(ks[4], (L, D_MODEL, N_MOD * D_MODEL), 0.5 * D_MODEL ** -0.5),
        "b_mod": nrm(ks[5], (L, N_MOD * D_MODEL), 0.01),
        "norm1_g": 1.0 + nrm(ks[6], (L, D_MODEL), 0.05),
        "w_in": nrm(ks[7], (L, D_MODEL, D_IN), D_MODEL ** -0.5),
        "q_norm_g": 1.0 + nrm(ks[8], (L, Q_LORA_RANK), 0.05),
        "w_uq": nrm(ks[9], (L, Q_LORA_RANK, MLA_HEADS * (QK_NOPE_DIM + QK_ROPE_DIM)), Q_LORA_RANK ** -0.5),
        "kv_norm_g": 1.0 + nrm(ks[10], (L, KV_LORA_RANK), 0.05),
        "w_ukv": nrm(ks[11], (L, KV_LORA_RANK, MLA_HEADS * (QK_NOPE_DIM + V_HEAD_DIM)), KV_LORA_RANK ** -0.5),
        "w_fourier": nrm(ks[12], (L, F_WIDTH, F_WIDTH), F_WIDTH ** -0.5),
        "w_dw": nrm(ks[13], (L, CONV_KERNEL, CONV_WIDTH), CONV_KERNEL ** -0.5),
        "b_dw": nrm(ks[14], (L, CONV_WIDTH), 0.01),
        "conv_ln_g": 1.0 + nrm(ks[15], (L, CONV_WIDTH), 0.05),
        "conv_ln_b": nrm(ks[16], (L, CONV_WIDTH), 0.01),
        "w_pw2": nrm(ks[17], (L, CONV_WIDTH, CONV_WIDTH), CONV_WIDTH ** -0.5),
        "w_o": nrm(ks[18], (L, D_MIX, D_MODEL), D_MIX ** -0.5),
        "norm2_g": 1.0 + nrm(ks[19], (L, D_MODEL), 0.05),
        "w_mlp1": nrm(ks[20], (L, D_MODEL, D_FF), D_MODEL ** -0.5),
        "w_mlp2": nrm(ks[21], (L, D_FF, D_MODEL), D_FF ** -0.5),
        "final_norm_g": 1.0 + nrm(ks[22], (D_MODEL,), 0.05),
    }


def reference(x, c, ctx, c_ctx, w_mod, b_mod, norm1_g, w_in, q_norm_g, w_uq, kv_norm_g, w_ukv,
              w_fourier, w_dw, b_dw, conv_ln_g, conv_ln_b, w_pw2, w_o, norm2_g, w_mlp1, w_mlp2,
              final_norm_g):
    B, S, D = x.shape
    cos, sin = axial_rope_angles(S)
    silu_c = jax.nn.silu(c)
    silu_cc = jax.nn.silu(c_ctx)

    def mixer_rest(zf, zg, attn_out, l):
        y_f = fourier_mix(zf, w_fourier[l])
        y_c = conformer_conv(zg, w_dw[l], b_dw[l], conv_ln_g[l], conv_ln_b[l], w_pw2[l])
        return jnp.concatenate([y_f, attn_out, y_c], axis=-1) @ w_o[l]

    def channel_mlp(h, l):
        return jnp.square(jax.nn.relu(h @ w_mlp1[l])) @ w_mlp2[l]

    for l in range(DEPTH):
        last = l == DEPTH - 1
        mx = jnp.split((silu_c @ w_mod[l] + b_mod[l])[:, None, :], N_MOD, axis=-1)
        mc = jnp.split(silu_cc @ w_mod[l] + b_mod[l], N_MOD, axis=-1)

        zx = modulate(rms_norm(x, norm1_g[l]), mx[0], mx[1]) @ w_in[l]
        zc = modulate(rms_norm(ctx, norm1_g[l]), mc[0], mc[1]) @ w_in[l]
        fx, cqx, ckvx, krx, gx = jnp.split(zx, SPLITS, axis=-1)
        fc, cqc, ckvc, krc, gc = jnp.split(zc, SPLITS, axis=-1)

        kn_c, v_c = mla_keys_values(ckvc, kv_norm_g[l], w_ukv[l])
        kn_x, v_x = mla_keys_values(ckvx, kv_norm_g[l], w_ukv[l])
        kr_x = apply_rope(krx, cos, sin)
        qn_x, qr_x = mla_queries(cqx, q_norm_g[l], w_uq[l])
        qr_x = apply_rope(qr_x, cos[:, None, :], sin[:, None, :])

        kn_all = jnp.concatenate([kn_c, kn_x], axis=1)
        kr_all = jnp.concatenate([krc, kr_x], axis=1)
        v_all = jnp.concatenate([v_c, v_x], axis=1)
        attn_x = blocked_mla_attend(qn_x, qr_x, kn_all, kr_all, v_all)

        if not last:
            qn_c, qr_c = mla_queries(cqc, q_norm_g[l], w_uq[l])
            attn_c = mla_attend(qn_c, qr_c, kn_c, krc, v_c).reshape(B, ctx.shape[1], MLA_WIDTH)
            ctx = ctx + mc[2] * mixer_rest(fc, gc, attn_c, l)
            hc = modulate(rms_norm(ctx, norm2_g[l]), mc[3], mc[4])
            ctx = ctx + mc[5] * channel_mlp(hc, l)

        x = x + mx[2] * mixer_rest(fx, gx, attn_x, l)
        hx = modulate(rms_norm(x, norm2_g[l]), mx[3], mx[4])
        x = x + mx[5] * channel_mlp(hx, l)

    return rms_norm(x, final_norm_g)
```

```python
import functools
import math

import numpy as np
import jax
import jax.numpy as jnp
from jax import lax
from jax.experimental import pallas as pl
from jax.experimental.pallas import tpu as pltpu

F32 = jnp.float32
BF16 = jnp.bfloat16

GRID_W = 64
F_GROUPS = 4
F_GROUP_DIM = 64
F_WIDTH = F_GROUPS * F_GROUP_DIM
MLA_HEADS = 8
QK_NOPE_DIM = 64
QK_ROPE_DIM = 32
V_HEAD_DIM = 64
Q_LORA_RANK = 256
KV_LORA_RANK = 128
CONV_WIDTH = 256
CONV_KERNEL = 31
N_MOD = 6
ROPE_BASE = 10000.0
ROPE_AXIS_DIM = QK_ROPE_DIM // 2
ATTN_SCALE = (QK_NOPE_DIM + QK_ROPE_DIM) ** -0.5
EPS = 1e-6

LANES = 128
HEAD_PAD = LANES
HP = MLA_HEADS * HEAD_PAD
ONES_LANE = V_HEAD_DIM
CONV_HALO = 16
FFT_N2 = 128
VMEM_LIMIT = 56 * 1024 * 1024

Q_SCALE = ATTN_SCALE * math.log2(math.e)


def _cparams(n_axes):
    return pltpu.CompilerParams(dimension_semantics=("parallel",) * n_axes,
                                vmem_limit_bytes=VMEM_LIMIT)


def _const_spec(shape):
    nd = len(shape)
    return pl.BlockSpec(shape, lambda *_: (0,) * nd, pipeline_mode=pl.Buffered(1))


def _dot(a, b):
    return jnp.dot(a, b, preferred_element_type=F32)


def _split_bf16(v):
    hi = v.astype(BF16)
    lo = (v - hi.astype(F32)).astype(BF16)
    return hi, lo


def _mod_kernel(c_ref, w_ref, b_ref, o_ref):
    c = c_ref[...]
    s = c * jax.nn.sigmoid(c)
    s_hi, s_lo = _split_bf16(s)
    w_hi, w_lo = _split_bf16(w_ref[...])
    o_ref[...] = _dot(s_hi, w_hi) + _dot(s_hi, w_lo) + _dot(s_lo, w_hi) + b_ref[...]


def _modulation(c_rows, w_mod, b_mod):
    n_layers, d, n6 = w_mod.shape
    tn = 1536
    return pl.pallas_call(
        _mod_kernel,
        out_shape=jax.ShapeDtypeStruct((n_layers, 8, n6), F32),
        grid=(n_layers, n6 // tn),
        in_specs=[
            pl.BlockSpec((8, d), lambda l, j: (0, 0)),
            pl.BlockSpec((None, d, tn), lambda l, j: (l, 0, j)),
            pl.BlockSpec((None, 1, tn), lambda l, j: (l, 0, j)),
        ],
        out_specs=pl.BlockSpec((None, 8, tn), lambda l, j: (l, 0, j)),
        compiler_params=_cparams(2),
        name="modulation",
    )(c_rows, w_mod, b_mod.reshape(n_layers, 1, n6))


def _rms(v, g):
    return v * lax.rsqrt(jnp.mean(v * v, axis=-1, keepdims=True) + EPS) * g


def _prep_kernel(x_ref, shift_ref, scale_ref, g1_ref, w1_ref, qg_ref, wqa_ref, wqb_ref,
                 kvg_ref, wk_ref, wv_ref, dft_ref, ta_ref, tb_ref, vone_ref,
                 q_ref, k_ref, v_ref, z_ref, u_ref):
    x = x_ref[...]
    h = (_rms(x, g1_ref[...]) * (1.0 + scale_ref[...]) + shift_ref[...]).astype(BF16)
    z = _dot(h, w1_ref[...])

    zz = _dot(z[:, 0:F_WIDTH].astype(BF16), dft_ref[...])
    z_ref[0] = zz[:, :F_WIDTH].astype(BF16)
    z_ref[1] = zz[:, F_WIDTH:].astype(BF16)

    ta = ta_ref[...]
    tb = tb_ref[...]
    lane = lax.broadcasted_iota(jnp.int32, (1, HEAD_PAD), 1)
    nope = (lane < QK_NOPE_DIM).astype(F32)

    c0 = F_WIDTH
    cqn = _rms(z[:, c0:c0 + Q_LORA_RANK], qg_ref[...]).astype(BF16)
    qa = _dot(cqn, wqa_ref[...])
    qb = _dot(cqn, wqb_ref[...])
    qta = (ta + nope) * Q_SCALE
    qtb = tb * Q_SCALE
    for hd in range(MLA_HEADS):
        sl = slice(hd * HEAD_PAD, (hd + 1) * HEAD_PAD)
        q_ref[:, sl] = (qa[:, sl] * qta + qb[:, sl] * qtb).astype(BF16)

    c1 = c0 + Q_LORA_RANK
    ckvn = _rms(z[:, c1:c1 + KV_LORA_RANK], kvg_ref[...]).astype(BF16)
    kn = _dot(ckvn, wk_ref[...])
    vv = _dot(ckvn, wv_ref[...]) + vone_ref[...]
    c2 = c1 + KV_LORA_RANK
    kr = z[:, c2:c2 + HEAD_PAD] * ta + z[:, c2 + HEAD_PAD:c2 + 2 * HEAD_PAD] * tb
    for hd in range(MLA_HEADS):
        sl = slice(hd * HEAD_PAD, (hd + 1) * HEAD_PAD)
        k_ref[:, sl] = (kn[:, sl] + kr).astype(BF16)
    v_ref[...] = vv.astype(BF16)

    c3 = c2 + 2 * HEAD_PAD
    a = z[:, c3:c3 + CONV_WIDTH]
    g = z[:, c3 + CONV_WIDTH:c3 + 2 * CONV_WIDTH]
    u_ref[...] = (a * jax.nn.sigmoid(g)).astype(BF16)


def _prep(x, mods, mod_row, g1, w, ta, tb, tm):
    bsz, s, d = x.shape
    n1 = w["w1"].shape[1]
    row = lambda j: pl.BlockSpec((None, None, 1, d), lambda b, i: (mod_row(b), j, 0, 0))
    tok = lambda width: pl.BlockSpec((None, tm, width), lambda b, i: (b, i, 0))
    return pl.pallas_call(
        _prep_kernel,
        out_shape=(
            jax.ShapeDtypeStruct((bsz, s, HP), BF16),
            jax.ShapeDtypeStruct((bsz, s, HP), BF16),
            jax.ShapeDtypeStruct((bsz, s, HP), BF16),
            jax.ShapeDtypeStruct((bsz, 2, s, F_WIDTH), BF16),
            jax.ShapeDtypeStruct((bsz, s, CONV_WIDTH), BF16),
        ),
        grid=(bsz, s // tm),
        in_specs=[
            tok(d), row(0), row(1), _const_spec((1, d)),
            _const_spec((d, n1)),
            _const_spec((1, Q_LORA_RANK)), _const_spec((Q_LORA_RANK, HP)), _const_spec((Q_LORA_RANK, HP)),
            _const_spec((1, KV_LORA_RANK)), _const_spec((KV_LORA_RANK, HP)), _const_spec((KV_LORA_RANK, HP)),
            _const_spec((F_WIDTH, 2 * F_WIDTH)),
            pl.BlockSpec((tm, HEAD_PAD), lambda b, i: (i, 0)),
            pl.BlockSpec((tm, HEAD_PAD), lambda b, i: (i, 0)),
            _const_spec((1, HP)),
        ],
        out_specs=(
            tok(HP), tok(HP), tok(HP),
            pl.BlockSpec((None, 2, tm, F_WIDTH), lambda b, i: (b, 0, i, 0)),
            tok(CONV_WIDTH),
        ),
        compiler_params=_cparams(2),
        name="prep",
    )(x, mods, mods, g1, w["w1"], w["qg"], w["wqa"], w["wqb"], w["kvg"], w["wk"], w["wv"],
      w["dft_c"], ta, tb, w["vone"])


def _attn_kernel(*refs, tk, n_x_blocks):
    if n_x_blocks:
        q_ref, kc_ref, vc_ref, kx_ref, vx_ref, o_ref, m_sc, acc_sc = refs
    else:
        q_ref, kc_ref, vc_ref, o_ref, m_sc, acc_sc = refs
    q = q_ref[...]
    m_sc[...] = jnp.full(m_sc.shape, -jnp.inf, F32)
    acc_sc[...] = jnp.zeros(acc_sc.shape, F32)

    def step(k, v):
        s = lax.dot_general(q, k, (((1,), (1,)), ((), ())), preferred_element_type=F32)
        m_prev = m_sc[...]
        m_new = jnp.maximum(m_prev, jnp.max(s, axis=1, keepdims=True))
        alpha = jnp.exp2(m_prev - m_new)
        p = jnp.exp2(s - jnp.tile(m_new, (1, s.shape[1] // LANES)))
        acc_sc[...] = alpha * acc_sc[...] + _dot(p.astype(BF16), v)
        m_sc[...] = m_new

    step(kc_ref[...], vc_ref[...])
    if n_x_blocks:
        def body(i, carry):
            off = pl.multiple_of(i * tk, tk)
            step(kx_ref[pl.ds(off, tk), :], vx_ref[pl.ds(off, tk), :])
            return carry
        lax.fori_loop(0, n_x_blocks, body, 0)

    acc = acc_sc[...]
    o_ref[...] = (acc / acc[:, ONES_LANE:ONES_LANE + 1]).astype(BF16)


def _attention(q, kc, vc, kx, vx, tq, tk):
    bsz, sq, _ = q.shape
    n_ctx = kc.shape[1]
    qspec = pl.BlockSpec((None, tq, HEAD_PAD), lambda b, h, i: (b, i, h))
    kvspec = lambda n: pl.BlockSpec((None, n, HEAD_PAD), lambda b, h, i: (b, 0, h))
    args = [q, kc, vc]
    in_specs = [qspec, kvspec(n_ctx), kvspec(n_ctx)]
    n_x_blocks = 0
    if kx is not None:
        s = kx.shape[1]
        n_x_blocks = s // tk
        args += [kx, vx]
        in_specs += [kvspec(s), kvspec(s)]
    return pl.pallas_call(
        functools.partial(_attn_kernel, tk=tk, n_x_blocks=n_x_blocks),
        out_shape=jax.ShapeDtypeStruct((bsz, sq, HP), BF16),
        grid=(bsz, MLA_HEADS, sq // tq),
        in_specs=in_specs,
        out_specs=qspec,
        scratch_shapes=[pltpu.VMEM((tq, LANES), F32), pltpu.VMEM((tq, HEAD_PAD), F32)],
        compiler_params=_cparams(3),
        name="attn_x" if kx is not None else "attn_ctx",
    )(*args)


def _fft1_kernel(z_ref, m_ref, o_ref, *, g, n1):
    c = F_WIDTH
    for j in range(g):
        zz = jnp.concatenate([z_ref[0, :, j * c:(j + 1) * c], z_ref[1, :, j * c:(j + 1) * c]], axis=0)
        r = _dot(m_ref[j], zz)
        o_ref[0, j] = r[:n1].astype(BF16)
        o_ref[1, j] = r[n1:].astype(BF16)


def _fft2_kernel(w_ref, x_ref, o_ref, *, scale):
    o_ref[...] = (_dot(w_ref[...], x_ref[...]) * scale).astype(BF16)


def _dft_cos_sin(rows, cols, period):
    r = (rows[:, None] * cols[None, :]) % period
    ang = r.astype(F32) * (2.0 * math.pi / period)
    return jnp.cos(ang), jnp.sin(ang)


def _fourier_seq(z, consts):
    bsz, _, s, c = z.shape
    scale = 1.0 / math.sqrt(s * F_GROUP_DIM)
    if s <= 2 * FFT_N2:
        w2 = consts["w_direct"]
        x2 = z.reshape(bsz, 2 * s, c)
        n_out, tn = s, c
    else:
        n1, n2 = s // FFT_N2, FFT_N2
        g = 8
        x2 = pl.pallas_call(
            functools.partial(_fft1_kernel, g=g, n1=n1),
            out_shape=jax.ShapeDtypeStruct((bsz, 2, n2, n1, c), BF16),
            grid=(bsz, n2 // g),
            in_specs=[
                pl.BlockSpec((None, 2, n1, g * c), lambda b, j: (b, 0, 0, j)),
                pl.BlockSpec((g, 2 * n1, 2 * n1), lambda b, j: (j, 0, 0)),
            ],
            out_specs=pl.BlockSpec((None, 2, g, n1, c), lambda b, j: (b, 0, j, 0, 0)),
            compiler_params=_cparams(2),
            name="fft_stage1",
        )(z.reshape(bsz, 2, n1, n2 * c), consts["m1"])
        x2 = x2.reshape(bsz, 2 * n2, n1 * c)
        w2 = consts["w2"]
        n_out, tn = n2, min(2048, n1 * c)
    width = x2.shape[2]
    y = pl.pallas_call(
        functools.partial(_fft2_kernel, scale=scale),
        out_shape=jax.ShapeDtypeStruct((bsz, n_out, width), BF16),
        grid=(bsz, width // tn),
        in_specs=[
            _const_spec(w2.shape),
            pl.BlockSpec((None, x2.shape[1], tn), lambda b, j: (b, 0, j)),
        ],
        out_specs=pl.BlockSpec((None, n_out, tn), lambda b, j: (b, 0, j)),
        compiler_params=_cparams(2),
        name="fft_stage2",
    )(w2, x2)
    return y.reshape(bsz, s, c)


def _conv_kernel(uc_ref, up_ref, un_ref, wdw_ref, bdw_ref, lng_ref, lnb_ref, wpw_ref, o_ref, ext_sc, *, tm, rc):
    i = pl.program_id(1)
    last = pl.num_programs(1) - 1
    ext_sc[0:CONV_HALO] = jnp.where(i > 0, up_ref[...].astype(F32), 0.0)
    ext_sc[CONV_HALO:CONV_HALO + tm] = uc_ref[...].astype(F32)
    ext_sc[CONV_HALO + tm:2 * CONV_HALO + tm] = jnp.where(i < last, un_ref[...].astype(F32), 0.0)
    pad = CONV_KERNEL // 2
    for r0 in range(0, tm, rc):
        acc = jnp.zeros((rc, CONV_WIDTH), F32)
        for k in range(CONV_KERNEL):
            start = CONV_HALO + r0 + k - pad
            acc = acc + ext_sc[start:start + rc, :] * wdw_ref[k:k + 1, :]
        acc = acc + bdw_ref[...]
        mu = jnp.mean(acc, axis=-1, keepdims=True)
        xc = acc - mu
        var = jnp.mean(xc * xc, axis=-1, keepdims=True)
        y = xc * lax.rsqrt(var + EPS) * lng_ref[...] + lnb_ref[...]
        y = y * jax.nn.sigmoid(y)
        o_ref[r0:r0 + rc, :] = _dot(y.astype(BF16), wpw_ref[...]).astype(BF16)


def _conv_module(u, w, tm):
    bsz, s, c = u.shape
    hb = tm // CONV_HALO
    n_halo = s // CONV_HALO
    return pl.pallas_call(
        functools.partial(_conv_kernel, tm=tm, rc=64),
        out_shape=jax.ShapeDtypeStruct((bsz, s, c), BF16),
        grid=(bsz, s // tm),
        in_specs=[
            pl.BlockSpec((None, tm, c), lambda b, i: (b, i, 0)),
            pl.BlockSpec((None, CONV_HALO, c), lambda b, i: (b, jnp.maximum(i * hb - 1, 0), 0)),
            pl.BlockSpec((None, CONV_HALO, c), lambda b, i: (b, jnp.minimum((i + 1) * hb, n_halo - 1), 0)),
            _const_spec((32, c)), _const_spec((1, c)), _const_spec((1, c)), _const_spec((1, c)),
            _const_spec((c, c)),
        ],
        out_specs=pl.BlockSpec((None, tm, c), lambda b, i: (b, i, 0)),
        scratch_shapes=[pltpu.VMEM((tm + 2 * CONV_HALO, c), F32)],
        compiler_params=_cparams(2),
        name="conv",
    )(u, u, u, w["w_dw"], w["b_dw"], w["ln_g"], w["ln_b"], w["w_pw2"])


def _mix_kernel(x_ref, gate_ref, yf_ref, at_ref, yc_ref, wf_ref, wof_ref, woa_ref, woc_ref, o_ref):
    yf = _dot(yf_ref[...], wf_ref[...]).astype(BF16)
    mix = _dot(yf, wof_ref[...]) + _dot(at_ref[...], woa_ref[...]) + _dot(yc_ref[...], woc_ref[...])
    o_ref[...] = x_ref[...] + gate_ref[...] * mix


def _mix(x, mods, mod_row, yf, attn, yc, w, tm):
    bsz, s, d = x.shape
    tok = lambda width: pl.BlockSpec((None, tm, width), lambda b, i: (b, i, 0))
    return pl.pallas_call(
        _mix_kernel,
        out_shape=jax.ShapeDtypeStruct((bsz, s, d), F32),
        grid=(bsz, s // tm),
        in_specs=[
            tok(d),
            pl.BlockSpec((None, None, 1, d), lambda b, i: (mod_row(b), 2, 0, 0)),
            tok(F_WIDTH), tok(HP), tok(CONV_WIDTH),
            _const_spec((F_WIDTH, F_WIDTH)), _const_spec((F_WIDTH, d)), _const_spec((HP, d)),
            _const_spec((CONV_WIDTH, d)),
        ],
        out_specs=tok(d),
        compiler_params=_cparams(2),
        name="mix",
    )(x, mods, yf, attn, yc, w["w_f"], w["wo_f"], w["wo_a"], w["wo_c"])


def _mlp_kernel(x_ref, shift_ref, scale_ref, gate_ref, g2_ref, w1_ref, w2_ref, gf_ref, o_ref, *, ff_chunk, final_norm):
    x = x_ref[...]
    h = (_rms(x, g2_ref[...]) * (1.0 + scale_ref[...]) + shift_ref[...]).astype(BF16)
    d_ff = w1_ref.shape[1]
    acc = jnp.zeros(x.shape, F32)
    for c0 in range(0, d_ff, ff_chunk):
        a = jnp.maximum(_dot(h, w1_ref[:, c0:c0 + ff_chunk]), 0.0)
        acc = acc + _dot((a * a).astype(BF16), w2_ref[c0:c0 + ff_chunk, :])
    y = x + gate_ref[...] * acc
    if final_norm:
        y = _rms(y, gf_ref[...])
    o_ref[...] = y


def _mlp(x, mods, mod_row, g2, w1, w2, gf, tm, final_norm):
    bsz, s, d = x.shape
    d_ff = w1.shape[1]
    row = lambda j: pl.BlockSpec((None, None, 1, d), lambda b, i: (mod_row(b), j, 0, 0))
    tok = pl.BlockSpec((None, tm, d), lambda b, i: (b, i, 0))
    return pl.pallas_call(
        functools.partial(_mlp_kernel, ff_chunk=1024, final_norm=final_norm),
        out_shape=jax.ShapeDtypeStruct((bsz, s, d), F32),
        grid=(bsz, s // tm),
        in_specs=[tok, row(3), row(4), row(5), _const_spec((1, d)),
                  _const_spec((d, d_ff)), _const_spec((d_ff, d)), _const_spec((1, d))],
        out_specs=tok,
        compiler_params=_cparams(2),
        name="mlp",
    )(x, mods, mods, mods, g2, w1, w2, gf)


def _rope_tables(s):
    t = jnp.arange(s)
    row = (t // GRID_W).astype(F32)
    col = (t % GRID_W).astype(F32)
    inv = ROPE_BASE ** (-jnp.arange(0, ROPE_AXIS_DIM, 2, dtype=F32) / ROPE_AXIS_DIM)
    ang = jnp.concatenate([row[:, None] * inv, col[:, None] * inv], axis=-1)
    cos, sin = jnp.cos(ang), jnp.sin(ang)
    z_lo = jnp.zeros((s, QK_NOPE_DIM), F32)
    z_hi = jnp.zeros((s, HEAD_PAD - QK_NOPE_DIM - QK_ROPE_DIM), F32)
    ta = jnp.concatenate([z_lo, cos, cos, z_hi], axis=-1)
    tb = jnp.concatenate([z_lo, -sin, sin, z_hi], axis=-1)
    return ta, tb


def _identity_rope_tables(s):
    lane = np.arange(HEAD_PAD)
    ta = ((lane >= QK_NOPE_DIM) & (lane < QK_NOPE_DIM + QK_ROPE_DIM)).astype(np.float32)
    return jnp.asarray(np.tile(ta, (s, 1))), jnp.zeros((s, HEAD_PAD), F32)


def _fourier_consts(s, n_ctx):
    ar = lambda n: jnp.arange(n, dtype=jnp.int32)
    cc, sc = _dft_cos_sin(ar(F_GROUP_DIM), ar(F_GROUP_DIM), F_GROUP_DIM)
    eye = jnp.eye(F_GROUPS, dtype=F32)
    dft_c = jnp.concatenate([jnp.kron(eye, cc), -jnp.kron(eye, sc)], axis=1).astype(BF16)
    consts = {"dft_c": dft_c}

    def direct(n):
        c, sn = _dft_cos_sin(ar(n), ar(n), n)
        return jnp.concatenate([c, sn], axis=1).astype(BF16)

    def two_stage(n):
        n1, n2 = n // FFT_N2, FFT_N2
        pos = ar(n1)[None, :] * n2 + ar(n2)[:, None]
        r = (ar(n1)[None, :, None] * pos[:, None, :]) % n
        ang = r.astype(F32) * (2.0 * math.pi / n)
        c, sn = jnp.cos(ang), jnp.sin(ang)
        m1 = jnp.concatenate([jnp.concatenate([c, sn], axis=2),
                              jnp.concatenate([-sn, c], axis=2)], axis=1).astype(BF16)
        return m1, direct(n2)

    for n, tag in ((s, "x"), (n_ctx, "ctx")):
        if n <= 2 * FFT_N2:
            consts[tag] = {"w_direct": direct(n)}
        else:
            m1, w2 = two_stage(n)
            consts[tag] = {"m1": m1, "w2": w2}
    return consts


def _pad_heads(w, width):
    k = w.shape[0]
    w = w.reshape(k, MLA_HEADS, width)
    return jnp.pad(w, ((0, 0), (0, 0), (0, HEAD_PAD - width))).reshape(k, HP)


def _layer_weights(l, p, dft_c):
    d = p["w_in"].shape[1]
    w_in = p["w_in"][l]
    s0, s1, s2 = F_WIDTH, F_WIDTH + Q_LORA_RANK, F_WIDTH + Q_LORA_RANK + KV_LORA_RANK
    s3 = s2 + QK_ROPE_DIM
    half = QK_ROPE_DIM // 2
    w_kr = w_in[:, s2:s3]
    w_kr_sw = jnp.concatenate([w_kr[:, half:], w_kr[:, :half]], axis=1)
    lo = jnp.zeros((d, QK_NOPE_DIM), F32)
    hi = jnp.zeros((d, HEAD_PAD - QK_NOPE_DIM - QK_ROPE_DIM), F32)
    w1 = jnp.concatenate([w_in[:, :s2], lo, w_kr, hi, lo, w_kr_sw, hi, w_in[:, s3:]], axis=1).astype(BF16)

    qd = QK_NOPE_DIM + QK_ROPE_DIM
    w_uq = p["w_uq"][l].reshape(Q_LORA_RANK, MLA_HEADS, qd)
    rope = w_uq[:, :, QK_NOPE_DIM:]
    rope_sw = jnp.concatenate([rope[:, :, half:], rope[:, :, :half]], axis=2)
    wqa = _pad_heads(w_uq.reshape(Q_LORA_RANK, MLA_HEADS * qd), qd).astype(BF16)
    wqb = jnp.pad(rope_sw, ((0, 0), (0, 0), (QK_NOPE_DIM, HEAD_PAD - qd))).reshape(Q_LORA_RANK, HP).astype(BF16)

    w_ukv = p["w_ukv"][l].reshape(KV_LORA_RANK, MLA_HEADS, QK_NOPE_DIM + V_HEAD_DIM)
    wk = _pad_heads(w_ukv[:, :, :QK_NOPE_DIM].reshape(KV_LORA_RANK, -1), QK_NOPE_DIM).astype(BF16)
    wv = _pad_heads(w_ukv[:, :, QK_NOPE_DIM:].reshape(KV_LORA_RANK, -1), V_HEAD_DIM).astype(BF16)
    vone = np.zeros((1, HP), np.float32)
    vone[0, ONES_LANE::HEAD_PAD] = 1.0

    w_o = p["w_o"][l]
    a0, a1 = F_WIDTH, F_WIDTH + MLA_HEADS * V_HEAD_DIM
    wo_a = jnp.pad(w_o[a0:a1].reshape(MLA_HEADS, V_HEAD_DIM, d),
                   ((0, 0), (0, HEAD_PAD - V_HEAD_DIM), (0, 0))).reshape(HP, d).astype(BF16)
    w_dw = jnp.pad(p["w_dw"][l], ((0, 32 - CONV_KERNEL), (0, 0)))
    r2 = lambda v: v.reshape(1, -1)
    return {
        "w1": w1, "qg": r2(p["q_norm_g"][l]), "wqa": wqa, "wqb": wqb,
        "kvg": r2(p["kv_norm_g"][l]), "wk": wk, "wv": wv, "dft_c": dft_c, "vone": jnp.asarray(vone),
        "w_dw": w_dw, "b_dw": r2(p["b_dw"][l]), "ln_g": r2(p["conv_ln_g"][l]), "ln_b": r2(p["conv_ln_b"][l]),
        "w_pw2": p["w_pw2"][l].astype(BF16),
        "w_f": p["w_fourier"][l].astype(BF16), "wo_f": w_o[:a0].astype(BF16), "wo_a": wo_a,
        "wo_c": w_o[a1:].astype(BF16),
        "g1": r2(p["norm1_g"][l]), "g2": r2(p["norm2_g"][l]),
        "w_mlp1": p["w_mlp1"][l].astype(BF16), "w_mlp2": p["w_mlp2"][l].astype(BF16),
    }


def _token_tile(s):
    for tm in (512, 256, 128):
        if s % tm == 0:
            return tm
    raise ValueError(f"sequence length {s} must be a multiple of 128")


def kernel(x, c, ctx, c_ctx, w_mod, b_mod, norm1_g, w_in, q_norm_g, w_uq, kv_norm_g, w_ukv, w_fourier,
           w_dw, b_dw, conv_ln_g, conv_ln_b, w_pw2, w_o, norm2_g, w_mlp1, w_mlp2, final_norm_g):
    bsz, s, d = x.shape
    n_ctx = ctx.shape[1]
    depth = w_mod.shape[0]
    assert bsz < 8 and s % GRID_W == 0 and s % LANES == 0 and n_ctx % LANES == 0
    p = dict(w_in=w_in, w_uq=w_uq, w_ukv=w_ukv, q_norm_g=q_norm_g, kv_norm_g=kv_norm_g,
             w_fourier=w_fourier, w_dw=w_dw, b_dw=b_dw, conv_ln_g=conv_ln_g, conv_ln_b=conv_ln_b,
             w_pw2=w_pw2, w_o=w_o, norm1_g=norm1_g, norm2_g=norm2_g, w_mlp1=w_mlp1, w_mlp2=w_mlp2)

    c_rows = jnp.zeros((8, d), F32).at[:bsz].set(c).at[bsz].set(c_ctx)
    mods_all = _modulation(c_rows, w_mod, b_mod).reshape(depth, 8, N_MOD, 1, d)
    x_row = lambda b: b
    ctx_row = lambda b: bsz

    tm_x, tm_c = _token_tile(s), _token_tile(n_ctx)
    tq = tm_x
    tk = next(t for t in (512, 256, 128) if s % t == 0)
    ta_x, tb_x = _rope_tables(s)
    ta_c, tb_c = _identity_rope_tables(n_ctx)
    fc = _fourier_consts(s, n_ctx)
    gf = final_norm_g.reshape(1, d)

    for l in range(depth):
        last = l == depth - 1
        w = _layer_weights(l, p, fc["dft_c"])
        mods = mods_all[l]

        qx, kx, vx, zx, ux = _prep(x, mods, x_row, w["g1"], w, ta_x, tb_x, tm_x)
        qc, kc, vc, zc, uc = _prep(ctx, mods, ctx_row, w["g1"], w, ta_c, tb_c, tm_c)

        attn_x = _attention(qx, kc, vc, kx, vx, tq, tk)
        yf_x = _fourier_seq(zx, fc["x"])
        yc_x = _conv_module(ux, w, tm_x)
        x = _mix(x, mods, x_row, yf_x, attn_x, yc_x, w, tm_x)
        x = _mlp(x, mods, x_row, w["g2"], w["w_mlp1"], w["w_mlp2"], gf, tm_x, final_norm=last)

        if not last:
            attn_c = _attention(qc, kc, vc, None, None, tm_c, tk)
            yf_c = _fourier_seq(zc, fc["ctx"])
            yc_c = _conv_module(uc, w, tm_c)
            ctx = _mix(ctx, mods, ctx_row, yf_c, attn_c, yc_c, w, tm_c)
            ctx = _mlp(ctx, mods, ctx_row, w["g2"], w["w_mlp1"], w["w_mlp2"], gf, tm_c, final_norm=False)
    return x
```

```python
import functools
import math

import numpy as np
import jax
import jax.numpy as jnp
from jax import lax
from jax.experimental import pallas as pl
from jax.experimental.pallas import tpu as pltpu

F32 = jnp.float32
BF16 = jnp.bfloat16

GRID_W = 64
F_GROUPS = 4
F_GROUP_DIM = 64
F_WIDTH = F_GROUPS * F_GROUP_DIM
MLA_HEADS = 8
QK_NOPE_DIM = 64
QK_ROPE_DIM = 32
V_HEAD_DIM = 64
Q_LORA_RANK = 256
KV_LORA_RANK = 128
CONV_WIDTH = 256
CONV_KERNEL = 31
N_MOD = 6
ROPE_BASE = 10000.0
ROPE_AXIS_DIM = QK_ROPE_DIM // 2
ATTN_SCALE = (QK_NOPE_DIM + QK_ROPE_DIM) ** -0.5
EPS = 1e-6

LANES = 128
HEAD_PAD = LANES
HP = MLA_HEADS * HEAD_PAD
ONES_LANE = V_HEAD_DIM
CONV_HALO = 16
FFT_N2 = 128
VMEM_LIMIT = 56 * 1024 * 1024

Q_SCALE = ATTN_SCALE * math.log2(math.e)


def _cparams(n_axes):
    return pltpu.CompilerParams(dimension_semantics=("parallel",) * n_axes,
                                vmem_limit_bytes=VMEM_LIMIT)


def _const_spec(shape):
    nd = len(shape)
    return pl.BlockSpec(shape, lambda *_: (0,) * nd, pipeline_mode=pl.Buffered(1))


def _dot(a, b):
    return jnp.dot(a, b, preferred_element_type=F32)


def _split_bf16(v):
    hi = v.astype(BF16)
    lo = (v - hi.astype(F32)).astype(BF16)
    return hi, lo


def _mod_kernel(c_ref, w_ref, b_ref, o_ref):
    c = c_ref[...]
    s = c * jax.nn.sigmoid(c)
    s_hi, s_lo = _split_bf16(s)
    w_hi, w_lo = _split_bf16(w_ref[...])
    o_ref[...] = _dot(s_hi, w_hi) + _dot(s_hi, w_lo) + _dot(s_lo, w_hi) + b_ref[...]


def _modulation(c_rows, w_mod, b_mod):
    n_layers, d, n6 = w_mod.shape
    tn = 1536
    return pl.pallas_call(
        _mod_kernel,
        out_shape=jax.ShapeDtypeStruct((n_layers, 8, n6), F32),
        grid=(n_layers, n6 // tn),
        in_specs=[
            pl.BlockSpec((8, d), lambda l, j: (0, 0)),
            pl.BlockSpec((None, d, tn), lambda l, j: (l, 0, j)),
            pl.BlockSpec((None, 1, tn), lambda l, j: (l, 0, j)),
        ],
        out_specs=pl.BlockSpec((None, 8, tn), lambda l, j: (l, 0, j)),
        compiler_params=_cparams(2),
        name="modulation",
    )(c_rows, w_mod, b_mod.reshape(n_layers, 1, n6))


def _rms(v, g):
    return v * lax.rsqrt(jnp.mean(v * v, axis=-1, keepdims=True) + EPS) * g


def _prep_kernel(x_ref, shift_ref, scale_ref, g1_ref, w1_ref, qg_ref, wqa_ref, wqb_ref,
                 kvg_ref, wk_ref, wv_ref, dft_ref, ta_ref, tb_ref, vone_ref,
                 q_ref, k_ref, v_ref, z_ref, u_ref):
    x = x_ref[...]
    h = (_rms(x, g1_ref[...]) * (1.0 + scale_ref[...]) + shift_ref[...]).astype(BF16)
    z = _dot(h, w1_ref[...])

    zz = _dot(z[:, 0:F_WIDTH].astype(BF16), dft_ref[...])
    z_ref[0] = zz[:, :F_WIDTH].astype(BF16)
    z_ref[1] = zz[:, F_WIDTH:].astype(BF16)

    ta = ta_ref[...]
    tb = tb_ref[...]
    lane = lax.broadcasted_iota(jnp.int32, (1, HEAD_PAD), 1)
    nope = (lane < QK_NOPE_DIM).astype(F32)

    c0 = F_WIDTH
    cqn = _rms(z[:, c0:c0 + Q_LORA_RANK], qg_ref[...]).astype(BF16)
    qa = _dot(cqn, wqa_ref[...])
    qb = _dot(cqn, wqb_ref[...])
    qta = (ta + nope) * Q_SCALE
    qtb = tb * Q_SCALE
    for hd in range(MLA_HEADS):
        sl = slice(hd * HEAD_PAD, (hd + 1) * HEAD_PAD)
        q_ref[:, sl] = (qa[:, sl] * qta + qb[:, sl] * qtb).astype(BF16)

    c1 = c0 + Q_LORA_RANK
    ckvn = _rms(z[:, c1:c1 + KV_LORA_RANK], kvg_ref[...]).astype(BF16)
    kn = _dot(ckvn, wk_ref[...])
    vv = _dot(ckvn, wv_ref[...]) + vone_ref[...]
    c2 = c1 + KV_LORA_RANK
    kr = z[:, c2:c2 + HEAD_PAD] * ta + z[:, c2 + HEAD_PAD:c2 + 2 * HEAD_PAD] * tb
    for hd in range(MLA_HEADS):
        sl = slice(hd * HEAD_PAD, (hd + 1) * HEAD_PAD)
        k_ref[:, sl] = (kn[:, sl] + kr).astype(BF16)
    v_ref[...] = vv.astype(BF16)

    c3 = c2 + 2 * HEAD_PAD
    a = z[:, c3:c3 + CONV_WIDTH]
    g = z[:, c3 + CONV_WIDTH:c3 + 2 * CONV_WIDTH]
    u_ref[...] = (a * jax.nn.sigmoid(g)).astype(BF16)


def _prep(x, mods, mod_row, g1, w, ta, tb, tm):
    bsz, s, d = x.shape
    n1 = w["w1"].shape[1]
    row = lambda j: pl.BlockSpec((None, None, 1, d), lambda b, i: (mod_row(b), j, 0, 0))
    tok = lambda width: pl.BlockSpec((None, tm, width), lambda b, i: (b, i, 0))
    return pl.pallas_call(
        _prep_kernel,
        out_shape=(
            jax.ShapeDtypeStruct((bsz, s, HP), BF16),
            jax.ShapeDtypeStruct((bsz, s, HP), BF16),
            jax.ShapeDtypeStruct((bsz, s, HP), BF16),
            jax.ShapeDtypeStruct((bsz, 2, s, F_WIDTH), BF16),
            jax.ShapeDtypeStruct((bsz, s, CONV_WIDTH), BF16),
        ),
        grid=(bsz, s // tm),
        in_specs=[
            tok(d), row(0), row(1), _const_spec((1, d)),
            _const_spec((d, n1)),
            _const_spec((1, Q_LORA_RANK)), _const_spec((Q_LORA_RANK, HP)), _const_spec((Q_LORA_RANK, HP)),
            _const_spec((1, KV_LORA_RANK)), _const_spec((KV_LORA_RANK, HP)), _const_spec((KV_LORA_RANK, HP)),
            _const_spec((F_WIDTH, 2 * F_WIDTH)),
            pl.BlockSpec((tm, HEAD_PAD), lambda b, i: (i, 0)),
            pl.BlockSpec((tm, HEAD_PAD), lambda b, i: (i, 0)),
            _const_spec((1, HP)),
        ],
        out_specs=(
            tok(HP), tok(HP), tok(HP),
            pl.BlockSpec((None, 2, tm, F_WIDTH), lambda b, i: (b, 0, i, 0)),
            tok(CONV_WIDTH),
        ),
        compiler_params=_cparams(2),
        name="prep",
    )(x, mods, mods, g1, w["w1"], w["qg"], w["wqa"], w["wqb"], w["kvg"], w["wk"], w["wv"],
      w["dft_c"], ta, tb, w["vone"])


def _attn_kernel(*refs, tk, n_x_blocks):
    if n_x_blocks:
        q_ref, kc_ref, vc_ref, kx_ref, vx_ref, o_ref, m_sc, acc_sc, s0_sc, s1_sc = refs
    else:
        q_ref, kc_ref, vc_ref, o_ref, m_sc, acc_sc = refs
    q = q_ref[...]
    m_sc[...] = jnp.full(m_sc.shape, -jnp.inf, F32)
    acc_sc[...] = jnp.zeros(acc_sc.shape, F32)

    def scores(k):
        return lax.dot_general(q, k, (((1,), (1,)), ((), ())), preferred_element_type=F32)

    def softmax_pv(s, v):
        m_prev = m_sc[...]
        m_new = jnp.maximum(m_prev, jnp.max(s, axis=1, keepdims=True))
        alpha = jnp.exp2(m_prev - m_new)
        p = jnp.exp2(s - jnp.tile(m_new, (1, s.shape[1] // LANES)))
        acc_sc[...] = alpha * acc_sc[...] + _dot(p.astype(BF16), v)
        m_sc[...] = m_new

    def x_tile(ref, i):
        return ref[pl.ds(pl.multiple_of(i * tk, tk), tk), :]

    if n_x_blocks:
        s0_sc[...] = scores(x_tile(kx_ref, 0))
    softmax_pv(scores(kc_ref[...]), vc_ref[...])
    if n_x_blocks:
        def body(j, carry):
            i0 = 2 * j
            s1_sc[...] = scores(x_tile(kx_ref, i0 + 1))
            softmax_pv(s0_sc[...], x_tile(vx_ref, i0))
            s0_sc[...] = scores(x_tile(kx_ref, jnp.minimum(i0 + 2, n_x_blocks - 1)))
            softmax_pv(s1_sc[...], x_tile(vx_ref, i0 + 1))
            return carry
        lax.fori_loop(0, n_x_blocks // 2, body, 0)

    acc = acc_sc[...]
    o_ref[...] = (acc / acc[:, ONES_LANE:ONES_LANE + 1]).astype(BF16)


def _attention(q, kc, vc, kx, vx, tq, tk):
    bsz, sq, _ = q.shape
    n_ctx = kc.shape[1]
    qspec = pl.BlockSpec((None, tq, HEAD_PAD), lambda b, h, i: (b, i, h))
    kvspec = lambda n: pl.BlockSpec((None, n, HEAD_PAD), lambda b, h, i: (b, 0, h))
    args = [q, kc, vc]
    in_specs = [qspec, kvspec(n_ctx), kvspec(n_ctx)]
    n_x_blocks = 0
    scratch = [pltpu.VMEM((tq, LANES), F32), pltpu.VMEM((tq, HEAD_PAD), F32)]
    if kx is not None:
        s = kx.shape[1]
        n_x_blocks = s // tk
        assert n_x_blocks % 2 == 0, "the key loop consumes two tiles per pass"
        args += [kx, vx]
        in_specs += [kvspec(s), kvspec(s)]
        scratch += [pltpu.VMEM((tq, tk), F32), pltpu.VMEM((tq, tk), F32)]
    return pl.pallas_call(
        functools.partial(_attn_kernel, tk=tk, n_x_blocks=n_x_blocks),
        out_shape=jax.ShapeDtypeStruct((bsz, sq, HP), BF16),
        grid=(bsz, MLA_HEADS, sq // tq),
        in_specs=in_specs,
        out_specs=qspec,
        scratch_shapes=scratch,
        compiler_params=_cparams(3),
        name="attn_x" if kx is not None else "attn_ctx",
    )(*args)


def _fft1_kernel(z_ref, m_ref, o_ref, *, g, n1):
    c = F_WIDTH
    for j in range(g):
        zz = jnp.concatenate([z_ref[0, :, j * c:(j + 1) * c], z_ref[1, :, j * c:(j + 1) * c]], axis=0)
        r = _dot(m_ref[j], zz)
        o_ref[0, j] = r[:n1].astype(BF16)
        o_ref[1, j] = r[n1:].astype(BF16)


def _fft2_kernel(w_ref, x_ref, o_ref, *, scale):
    o_ref[...] = (_dot(w_ref[...], x_ref[...]) * scale).astype(BF16)


def _dft_cos_sin(rows, cols, period):
    r = (rows[:, None] * cols[None, :]) % period
    ang = r.astype(F32) * (2.0 * math.pi / period)
    return jnp.cos(ang), jnp.sin(ang)


def _fourier_seq(z, consts):
    bsz, _, s, c = z.shape
    scale = 1.0 / math.sqrt(s * F_GROUP_DIM)
    if s <= 2 * FFT_N2:
        w2 = consts["w_direct"]
        x2 = z.reshape(bsz, 2 * s, c)
        n_out, tn = s, c
    else:
        n1, n2 = s // FFT_N2, FFT_N2
        g = 8
        x2 = pl.pallas_call(
            functools.partial(_fft1_kernel, g=g, n1=n1),
            out_shape=jax.ShapeDtypeStruct((bsz, 2, n2, n1, c), BF16),
            grid=(bsz, n2 // g),
            in_specs=[
                pl.BlockSpec((None, 2, n1, g * c), lambda b, j: (b, 0, 0, j)),
                pl.BlockSpec((g, 2 * n1, 2 * n1), lambda b, j: (j, 0, 0)),
            ],
            out_specs=pl.BlockSpec((None, 2, g, n1, c), lambda b, j: (b, 0, j, 0, 0)),
            compiler_params=_cparams(2),
            name="fft_stage1",
        )(z.reshape(bsz, 2, n1, n2 * c), consts["m1"])
        x2 = x2.reshape(bsz, 2 * n2, n1 * c)
        w2 = consts["w2"]
        n_out, tn = n2, min(2048, n1 * c)
    width = x2.shape[2]
    y = pl.pallas_call(
        functools.partial(_fft2_kernel, scale=scale),
        out_shape=jax.ShapeDtypeStruct((bsz, n_out, width), BF16),
        grid=(bsz, width // tn),
        in_specs=[
            _const_spec(w2.shape),
            pl.BlockSpec((None, x2.shape[1], tn), lambda b, j: (b, 0, j)),
        ],
        out_specs=pl.BlockSpec((None, n_out, tn), lambda b, j: (b, 0, j)),
        compiler_params=_cparams(2),
        name="fft_stage2",
    )(w2, x2)
    return y.reshape(bsz, s, c)


def _conv_kernel(uc_ref, up_ref, un_ref, wdw_ref, bdw_ref, lng_ref, lnb_ref, wpw_ref, o_ref, ext_sc, *, tm, rc):
    i = pl.program_id(1)
    last = pl.num_programs(1) - 1
    ext_sc[0:CONV_HALO] = jnp.where(i > 0, up_ref[...].astype(F32), 0.0)
    ext_sc[CONV_HALO:CONV_HALO + tm] = uc_ref[...].astype(F32)
    ext_sc[CONV_HALO + tm:2 * CONV_HALO + tm] = jnp.where(i < last, un_ref[...].astype(F32), 0.0)
    pad = CONV_KERNEL // 2
    for r0 in range(0, tm, rc):
        acc = jnp.zeros((rc, CONV_WIDTH), F32)
        for k in range(CONV_KERNEL):
            start = CONV_HALO + r0 + k - pad
            acc = acc + ext_sc[start:start + rc, :] * wdw_ref[k:k + 1, :]
        acc = acc + bdw_ref[...]
        mu = jnp.mean(acc, axis=-1, keepdims=True)
        xc = acc - mu
        var = jnp.mean(xc * xc, axis=-1, keepdims=True)
        y = xc * lax.rsqrt(var + EPS) * lng_ref[...] + lnb_ref[...]
        y = y * jax.nn.sigmoid(y)
        o_ref[r0:r0 + rc, :] = _dot(y.astype(BF16), wpw_ref[...]).astype(BF16)


def _conv_module(u, w, tm):
    bsz, s, c = u.shape
    hb = tm // CONV_HALO
    n_halo = s // CONV_HALO
    return pl.pallas_call(
        functools.partial(_conv_kernel, tm=tm, rc=64),
        out_shape=jax.ShapeDtypeStruct((bsz, s, c), BF16),
        grid=(bsz, s // tm),
        in_specs=[
            pl.BlockSpec((None, tm, c), lambda b, i: (b, i, 0)),
            pl.BlockSpec((None, CONV_HALO, c), lambda b, i: (b, jnp.maximum(i * hb - 1, 0), 0)),
            pl.BlockSpec((None, CONV_HALO, c), lambda b, i: (b, jnp.minimum((i + 1) * hb, n_halo - 1), 0)),
            _const_spec((32, c)), _const_spec((1, c)), _const_spec((1, c)), _const_spec((1, c)),
            _const_spec((c, c)),
        ],
        out_specs=pl.BlockSpec((None, tm, c), lambda b, i: (b, i, 0)),
        scratch_shapes=[pltpu.VMEM((tm + 2 * CONV_HALO, c), F32)],
        compiler_params=_cparams(2),
        name="conv",
    )(u, u, u, w["w_dw"], w["b_dw"], w["ln_g"], w["ln_b"], w["w_pw2"])


def _mix_kernel(x_ref, gate_ref, yf_ref, at_ref, yc_ref, wf_ref, wof_ref, woa_ref, woc_ref, o_ref):
    yf = _dot(yf_ref[...], wf_ref[...]).astype(BF16)
    mix = _dot(yf, wof_ref[...]) + _dot(at_ref[...], woa_ref[...]) + _dot(yc_ref[...], woc_ref[...])
    o_ref[...] = x_ref[...] + gate_ref[...] * mix


def _mix(x, mods, mod_row, yf, attn, yc, w, tm):
    bsz, s, d = x.shape
    tok = lambda width: pl.BlockSpec((None, tm, width), lambda b, i: (b, i, 0))
    return pl.pallas_call(
        _mix_kernel,
        out_shape=jax.ShapeDtypeStruct((bsz, s, d), F32),
        grid=(bsz, s // tm),
        in_specs=[
            tok(d),
            pl.BlockSpec((None, None, 1, d), lambda b, i: (mod_row(b), 2, 0, 0)),
            tok(F_WIDTH), tok(HP), tok(CONV_WIDTH),
            _const_spec((F_WIDTH, F_WIDTH)), _const_spec((F_WIDTH, d)), _const_spec((HP, d)),
            _const_spec((CONV_WIDTH, d)),
        ],
        out_specs=tok(d),
        compiler_params=_cparams(2),
        name="mix",
    )(x, mods, yf, attn, yc, w["w_f"], w["wo_f"], w["wo_a"], w["wo_c"])


def _mlp_kernel(x_ref, shift_ref, scale_ref, gate_ref, g2_ref, w1_ref, w2_ref, gf_ref, o_ref, *, ff_chunk, final_norm):
    x = x_ref[...]
    h = (_rms(x, g2_ref[...]) * (1.0 + scale_ref[...]) + shift_ref[...]).astype(BF16)
    d_ff = w1_ref.shape[1]
    acc = jnp.zeros(x.shape, F32)
    for c0 in range(0, d_ff, ff_chunk):
        a = jnp.maximum(_dot(h, w1_ref[:, c0:c0 + ff_chunk]), 0.0)
        acc = acc + _dot((a * a).astype(BF16), w2_ref[c0:c0 + ff_chunk, :])
    y = x + gate_ref[...] * acc
    if final_norm:
        y = _rms(y, gf_ref[...])
    o_ref[...] = y


def _mlp(x, mods, mod_row, g2, w1, w2, gf, tm, final_norm):
    bsz, s, d = x.shape
    d_ff = w1.shape[1]
    row = lambda j: pl.BlockSpec((None, None, 1, d), lambda b, i: (mod_row(b), j, 0, 0))
    tok = pl.BlockSpec((None, tm, d), lambda b, i: (b, i, 0))
    return pl.pallas_call(
        functools.partial(_mlp_kernel, ff_chunk=1024, final_norm=final_norm),
        out_shape=jax.ShapeDtypeStruct((bsz, s, d), F32),
        grid=(bsz, s // tm),
        in_specs=[tok, row(3), row(4), row(5), _const_spec((1, d)),
                  _const_spec((d, d_ff)), _const_spec((d_ff, d)), _const_spec((1, d))],
        out_specs=tok,
        compiler_params=_cparams(2),
        name="mlp",
    )(x, mods, mods, mods, g2, w1, w2, gf)


def _rope_tables(s):
    t = jnp.arange(s)
    row = (t // GRID_W).astype(F32)
    col = (t % GRID_W).astype(F32)
    inv = ROPE_BASE ** (-jnp.arange(0, ROPE_AXIS_DIM, 2, dtype=F32) / ROPE_AXIS_DIM)
    ang = jnp.concatenate([row[:, None] * inv, col[:, None] * inv], axis=-1)
    cos, sin = jnp.cos(ang), jnp.sin(ang)
    z_lo = jnp.zeros((s, QK_NOPE_DIM), F32)
    z_hi = jnp.zeros((s, HEAD_PAD - QK_NOPE_DIM - QK_ROPE_DIM), F32)
    ta = jnp.concatenate([z_lo, cos, cos, z_hi], axis=-1)
    tb = jnp.concatenate([z_lo, -sin, sin, z_hi], axis=-1)
    return ta, tb


def _identity_rope_tables(s):
    lane = np.arange(HEAD_PAD)
    ta = ((lane >= QK_NOPE_DIM) & (lane < QK_NOPE_DIM + QK_ROPE_DIM)).astype(np.float32)
    return jnp.asarray(np.tile(ta, (s, 1))), jnp.zeros((s, HEAD_PAD), F32)


def _fourier_consts(s, n_ctx):
    ar = lambda n: jnp.arange(n, dtype=jnp.int32)
    cc, sc = _dft_cos_sin(ar(F_GROUP_DIM), ar(F_GROUP_DIM), F_GROUP_DIM)
    eye = jnp.eye(F_GROUPS, dtype=F32)
    dft_c = jnp.concatenate([jnp.kron(eye, cc), -jnp.kron(eye, sc)], axis=1).astype(BF16)
    consts = {"dft_c": dft_c}

    def direct(n):
        c, sn = _dft_cos_sin(ar(n), ar(n), n)
        return jnp.concatenate([c, sn], axis=1).astype(BF16)

    def two_stage(n):
        n1, n2 = n // FFT_N2, FFT_N2
        pos = ar(n1)[None, :] * n2 + ar(n2)[:, None]
        r = (ar(n1)[None, :, None] * pos[:, None, :]) % n
        ang = r.astype(F32) * (2.0 * math.pi / n)
        c, sn = jnp.cos(ang), jnp.sin(ang)
        m1 = jnp.concatenate([jnp.concatenate([c, sn], axis=2),
                              jnp.concatenate([-sn, c], axis=2)], axis=1).astype(BF16)
        return m1, direct(n2)

    for n, tag in ((s, "x"), (n_ctx, "ctx")):
        if n <= 2 * FFT_N2:
            consts[tag] = {"w_direct": direct(n)}
        else:
            m1, w2 = two_stage(n)
            consts[tag] = {"m1": m1, "w2": w2}
    return consts


def _pad_heads(w, width):
    k = w.shape[0]
    w = w.reshape(k, MLA_HEADS, width)
    return jnp.pad(w, ((0, 0), (0, 0), (0, HEAD_PAD - width))).reshape(k, HP)


def _layer_weights(l, p, dft_c):
    d = p["w_in"].shape[1]
    w_in = p["w_in"][l]
    s0, s1, s2 = F_WIDTH, F_WIDTH + Q_LORA_RANK, F_WIDTH + Q_LORA_RANK + KV_LORA_RANK
    s3 = s2 + QK_ROPE_DIM
    half = QK_ROPE_DIM // 2
    w_kr = w_in[:, s2:s3]
    w_kr_sw = jnp.concatenate([w_kr[:, half:], w_kr[:, :half]], axis=1)
    lo = jnp.zeros((d, QK_NOPE_DIM), F32)
    hi = jnp.zeros((d, HEAD_PAD - QK_NOPE_DIM - QK_ROPE_DIM), F32)
    w1 = jnp.concatenate([w_in[:, :s2], lo, w_kr, hi, lo, w_kr_sw, hi, w_in[:, s3:]], axis=1).astype(BF16)

    qd = QK_NOPE_DIM + QK_ROPE_DIM
    w_uq = p["w_uq"][l].reshape(Q_LORA_RANK, MLA_HEADS, qd)
    rope = w_uq[:, :, QK_NOPE_DIM:]
    rope_sw = jnp.concatenate([rope[:, :, half:], rope[:, :, :half]], axis=2)
    wqa = _pad_heads(w_uq.reshape(Q_LORA_RANK, MLA_HEADS * qd), qd).astype(BF16)
    wqb = jnp.pad(rope_sw, ((0, 0), (0, 0), (QK_NOPE_DIM, HEAD_PAD - qd))).reshape(Q_LORA_RANK, HP).astype(BF16)

    w_ukv = p["w_ukv"][l].reshape(KV_LORA_RANK, MLA_HEADS, QK_NOPE_DIM + V_HEAD_DIM)
    wk = _pad_heads(w_ukv[:, :, :QK_NOPE_DIM].reshape(KV_LORA_RANK, -1), QK_NOPE_DIM).astype(BF16)
    wv = _pad_heads(w_ukv[:, :, QK_NOPE_DIM:].reshape(KV_LORA_RANK, -1), V_HEAD_DIM).astype(BF16)
    vone = np.zeros((1, HP), np.float32)
    vone[0, ONES_LANE::HEAD_PAD] = 1.0

    w_o = p["w_o"][l]
    a0, a1 = F_WIDTH, F_WIDTH + MLA_HEADS * V_HEAD_DIM
    wo_a = jnp.pad(w_o[a0:a1].reshape(MLA_HEADS, V_HEAD_DIM, d),
                   ((0, 0), (0, HEAD_PAD - V_HEAD_DIM), (0, 0))).reshape(HP, d).astype(BF16)
    w_dw = jnp.pad(p["w_dw"][l], ((0, 32 - CONV_KERNEL), (0, 0)))
    r2 = lambda v: v.reshape(1, -1)
    return {
        "w1": w1, "qg": r2(p["q_norm_g"][l]), "wqa": wqa, "wqb": wqb,
        "kvg": r2(p["kv_norm_g"][l]), "wk": wk, "wv": wv, "dft_c": dft_c, "vone": jnp.asarray(vone),
        "w_dw": w_dw, "b_dw": r2(p["b_dw"][l]), "ln_g": r2(p["conv_ln_g"][l]), "ln_b": r2(p["conv_ln_b"][l]),
        "w_pw2": p["w_pw2"][l].astype(BF16),
        "w_f": p["w_fourier"][l].astype(BF16), "wo_f": w_o[:a0].astype(BF16), "wo_a": wo_a,
        "wo_c": w_o[a1:].astype(BF16),
        "g1": r2(p["norm1_g"][l]), "g2": r2(p["norm2_g"][l]),
        "w_mlp1": p["w_mlp1"][l].astype(BF16), "w_mlp2": p["w_mlp2"][l].astype(BF16),
    }


def _token_tile(s):
    for tm in (512, 256, 128):
        if s % tm == 0:
            return tm
    raise ValueError(f"sequence length {s} must be a multiple of 128")


def kernel(x, c, ctx, c_ctx, w_mod, b_mod, norm1_g, w_in, q_norm_g, w_uq, kv_norm_g, w_ukv, w_fourier,
           w_dw, b_dw, conv_ln_g, conv_ln_b, w_pw2, w_o, norm2_g, w_mlp1, w_mlp2, final_norm_g):
    bsz, s, d = x.shape
    n_ctx = ctx.shape[1]
    depth = w_mod.shape[0]
    assert bsz < 8 and s % GRID_W == 0 and s % LANES == 0 and n_ctx % LANES == 0
    p = dict(w_in=w_in, w_uq=w_uq, w_ukv=w_ukv, q_norm_g=q_norm_g, kv_norm_g=kv_norm_g,
             w_fourier=w_fourier, w_dw=w_dw, b_dw=b_dw, conv_ln_g=conv_ln_g, conv_ln_b=conv_ln_b,
             w_pw2=w_pw2, w_o=w_o, norm1_g=norm1_g, norm2_g=norm2_g, w_mlp1=w_mlp1, w_mlp2=w_mlp2)

    c_rows = jnp.zeros((8, d), F32).at[:bsz].set(c).at[bsz].set(c_ctx)
    mods_all = _modulation(c_rows, w_mod, b_mod).reshape(depth, 8, N_MOD, 1, d)
    x_row = lambda b: b
    ctx_row = lambda b: bsz

    tm_x, tm_c = _token_tile(s), _token_tile(n_ctx)
    tq = tm_x
    tk = next(t for t in (512, 256, 128) if s % t == 0)
    ta_x, tb_x = _rope_tables(s)
    ta_c, tb_c = _identity_rope_tables(n_ctx)
    fc = _fourier_consts(s, n_ctx)
    gf = final_norm_g.reshape(1, d)

    for l in range(depth):
        last = l == depth - 1
        w = _layer_weights(l, p, fc["dft_c"])
        mods = mods_all[l]

        qx, kx, vx, zx, ux = _prep(x, mods, x_row, w["g1"], w, ta_x, tb_x, tm_x)
        qc, kc, vc, zc, uc = _prep(ctx, mods, ctx_row, w["g1"], w, ta_c, tb_c, tm_c)

        attn_x = _attention(qx, kc, vc, kx, vx, tq, tk)
        yf_x = _fourier_seq(zx, fc["x"])
        yc_x = _conv_module(ux, w, tm_x)
        x = _mix(x, mods, x_row, yf_x, attn_x, yc_x, w, tm_x)
        x = _mlp(x, mods, x_row, w["g2"], w["w_mlp1"], w["w_mlp2"], gf, tm_x, final_norm=last)

        if not last:
            attn_c = _attention(qc, kc, vc, None, None, tm_c, tk)
            yf_c = _fourier_seq(zc, fc["ctx"])
            yc_c = _conv_module(uc, w, tm_c)
            ctx = _mix(ctx, mods, ctx_row, yf_c, attn_c, yc_c, w, tm_c)
            ctx = _mlp(ctx, mods, ctx_row, w["g2"], w["w_mlp1"], w["w_mlp2"], gf, tm_c, final_norm=False)
    return x
```

```python
import functools
import math

import numpy as np
import jax
import jax.numpy as jnp
from jax import lax
from jax.experimental import pallas as pl
from jax.experimental.pallas import tpu as pltpu

F32 = jnp.float32
BF16 = jnp.bfloat16

GRID_W = 64
F_GROUPS = 4
F_GROUP_DIM = 64
F_WIDTH = F_GROUPS * F_GROUP_DIM
MLA_HEADS = 8
QK_NOPE_DIM = 64
QK_ROPE_DIM = 32
V_HEAD_DIM = 64
Q_LORA_RANK = 256
KV_LORA_RANK = 128
CONV_WIDTH = 256
CONV_KERNEL = 31
N_MOD = 6
ROPE_BASE = 10000.0
ROPE_AXIS_DIM = QK_ROPE_DIM // 2
ATTN_SCALE = (QK_NOPE_DIM + QK_ROPE_DIM) ** -0.5
EPS = 1e-6

LANES = 128
HEAD_PAD = LANES
HP = MLA_HEADS * HEAD_PAD
ONES_LANE = V_HEAD_DIM
V_ROWS = 80
CONV_HALO = 16
FFT_N2 = 128
VMEM_LIMIT = 56 * 1024 * 1024
TILES_PER_PASS = 10
LAZY_HEADROOM = 60.0
ATTN_KEY_TILE = 512
ATTN_QUERY_TILE = 1024
Q_SCALE = ATTN_SCALE * math.log2(math.e)


def _cparams(n_axes, flags=None):
    return pltpu.CompilerParams(dimension_semantics=("parallel",) * n_axes,
                                vmem_limit_bytes=VMEM_LIMIT, flags=flags)


def _const_spec(shape):
    nd = len(shape)
    return pl.BlockSpec(shape, lambda *_: (0,) * nd, pipeline_mode=pl.Buffered(1))


def _dot(a, b):
    return jnp.dot(a, b, preferred_element_type=F32)


def _split_bf16(v):
    hi = v.astype(BF16)
    lo = (v - hi.astype(F32)).astype(BF16)
    return hi, lo


def _mod_kernel(c_ref, w_ref, b_ref, o_ref):
    c = c_ref[...]
    s = c * jax.nn.sigmoid(c)
    s_hi, s_lo = _split_bf16(s)
    w_hi, w_lo = _split_bf16(w_ref[...])
    o_ref[...] = _dot(s_hi, w_hi) + _dot(s_hi, w_lo) + _dot(s_lo, w_hi) + b_ref[...]


def _modulation(c_rows, w_mod, b_mod):
    n_layers, d, n6 = w_mod.shape
    tn = 1536
    return pl.pallas_call(
        _mod_kernel,
        out_shape=jax.ShapeDtypeStruct((n_layers, 8, n6), F32),
        grid=(n_layers, n6 // tn),
        in_specs=[
            pl.BlockSpec((8, d), lambda l, j: (0, 0)),
            pl.BlockSpec((None, d, tn), lambda l, j: (l, 0, j)),
            pl.BlockSpec((None, 1, tn), lambda l, j: (l, 0, j)),
        ],
        out_specs=pl.BlockSpec((None, 8, tn), lambda l, j: (l, 0, j)),
        compiler_params=_cparams(2),
        name="modulation",
    )(c_rows, w_mod, b_mod.reshape(n_layers, 1, n6))


def _rms(v, g):
    return v * lax.rsqrt(jnp.mean(v * v, axis=-1, keepdims=True) + EPS) * g


def _prep_kernel(x_ref, shift_ref, scale_ref, g1_ref, w1_ref, qg_ref, wqa_ref, wqb_ref,
                 kvg_ref, wk_ref, wv_ref, dft_ref, ta_ref, tb_ref, vone_ref,
                 q_ref, k_ref, v_ref, z_ref, u_ref):
    x = x_ref[...]
    h = (_rms(x, g1_ref[...]) * (1.0 + scale_ref[...]) + shift_ref[...]).astype(BF16)
    z = _dot(h, w1_ref[...])

    zz = _dot(z[:, 0:F_WIDTH].astype(BF16), dft_ref[...])
    z_ref[0] = zz[:, :F_WIDTH].astype(BF16)
    z_ref[1] = zz[:, F_WIDTH:].astype(BF16)

    ta = ta_ref[...]
    tb = tb_ref[...]
    lane = lax.broadcasted_iota(jnp.int32, (1, HEAD_PAD), 1)
    nope = (lane < QK_NOPE_DIM).astype(F32)

    c0 = F_WIDTH
    cqn = _rms(z[:, c0:c0 + Q_LORA_RANK], qg_ref[...]).astype(BF16)
    qa = _dot(cqn, wqa_ref[...])
    qb = _dot(cqn, wqb_ref[...])
    qta = (ta + nope) * Q_SCALE
    qtb = tb * Q_SCALE
    for hd in range(MLA_HEADS):
        sl = slice(hd * HEAD_PAD, (hd + 1) * HEAD_PAD)
        q_ref[:, sl] = (qa[:, sl] * qta + qb[:, sl] * qtb).astype(BF16)

    c1 = c0 + Q_LORA_RANK
    ckvn = _rms(z[:, c1:c1 + KV_LORA_RANK], kvg_ref[...]).astype(BF16)
    kn = _dot(ckvn, wk_ref[...])
    c2 = c1 + KV_LORA_RANK
    kr = z[:, c2:c2 + HEAD_PAD] * ta + z[:, c2 + HEAD_PAD:c2 + 2 * HEAD_PAD] * tb
    for hd in range(MLA_HEADS):
        sl = slice(hd * HEAD_PAD, (hd + 1) * HEAD_PAD)
        k_ref[:, sl] = (kn[:, sl] + kr).astype(BF16)
    vt = lax.dot_general(wv_ref[...], ckvn, (((1,), (1,)), ((), ())), preferred_element_type=F32)
    vt = vt + vone_ref[...]
    for hd in range(MLA_HEADS):
        v_ref[hd] = vt[hd * V_ROWS:(hd + 1) * V_ROWS, :].astype(BF16)

    c3 = c2 + 2 * HEAD_PAD
    a = z[:, c3:c3 + CONV_WIDTH]
    g = z[:, c3 + CONV_WIDTH:c3 + 2 * CONV_WIDTH]
    u_ref[...] = (a * jax.nn.sigmoid(g)).astype(BF16)


def _prep(x, mods, mod_row, g1, w, ta, tb, tm):
    bsz, s, d = x.shape
    n1 = w["w1"].shape[1]
    row = lambda j: pl.BlockSpec((None, None, 1, d), lambda b, i: (mod_row(b), j, 0, 0))
    tok = lambda width: pl.BlockSpec((None, tm, width), lambda b, i: (b, i, 0))
    return pl.pallas_call(
        _prep_kernel,
        out_shape=(
            jax.ShapeDtypeStruct((bsz, s, HP), BF16),
            jax.ShapeDtypeStruct((bsz, s, HP), BF16),
            jax.ShapeDtypeStruct((bsz, MLA_HEADS, s // tm, V_ROWS, tm), BF16),
            jax.ShapeDtypeStruct((bsz, 2, s, F_WIDTH), BF16),
            jax.ShapeDtypeStruct((bsz, s, CONV_WIDTH), BF16),
        ),
        grid=(bsz, s // tm),
        in_specs=[
            tok(d), row(0), row(1), _const_spec((1, d)),
            _const_spec((d, n1)),
            _const_spec((1, Q_LORA_RANK)), _const_spec((Q_LORA_RANK, HP)), _const_spec((Q_LORA_RANK, HP)),
            _const_spec((1, KV_LORA_RANK)), _const_spec((KV_LORA_RANK, HP)),
            _const_spec((MLA_HEADS * V_ROWS, KV_LORA_RANK)),
            _const_spec((F_WIDTH, 2 * F_WIDTH)),
            pl.BlockSpec((tm, HEAD_PAD), lambda b, i: (i, 0)),
            pl.BlockSpec((tm, HEAD_PAD), lambda b, i: (i, 0)),
            _const_spec((MLA_HEADS * V_ROWS, 1)),
        ],
        out_specs=(
            tok(HP), tok(HP),
            pl.BlockSpec((None, MLA_HEADS, None, V_ROWS, tm), lambda b, i: (b, 0, i, 0, 0)),
            pl.BlockSpec((None, 2, tm, F_WIDTH), lambda b, i: (b, 0, i, 0)),
            tok(CONV_WIDTH),
        ),
        compiler_params=_cparams(2),
        name="prep",
    )(x, mods, mods, g1, w["w1"], w["qg"], w["wqa"], w["wqb"], w["kvg"], w["wk"], w["wv"],
      w["dft_c"], ta, tb, w["vone"])


def _attn_kernel(*refs, tk, n_x_blocks):
    if n_x_blocks:
        q_ref, kc_ref, vc_ref, kx_ref, vx_ref, o_ref, m_sc, acc_sc, ex_sc = refs
    else:
        q_ref, kc_ref, vc_ref, o_ref, m_sc, acc_sc = refs
    qt = q_ref[...].astype(F32).T.astype(BF16)

    def exact_tile(k, vt, first=False):
        s = _dot(k, qt)
        m_tile = jnp.max(s, axis=0, keepdims=True)
        m_new = m_tile if first else jnp.maximum(m_sc[...], m_tile)
        p = jnp.exp2(s - m_new).astype(BF16)
        pv = _dot(vt, p)
        acc_sc[...] = pv if first else jnp.exp2(m_sc[...] - m_new) * acc_sc[...] + pv
        m_sc[...] = m_new

    exact_tile(kc_ref[...], vc_ref[0], first=True)
    if n_x_blocks:
        n = n_x_blocks
        v_sub = tk // vx_ref.shape[2]

        def k_tile(t):
            return kx_ref[pl.ds(pl.multiple_of(t * tk, tk), tk), :]

        def v_tile(t):
            return jnp.concatenate([vx_ref[v_sub * t + i] for i in range(v_sub)], axis=1)

        def lazy_tile(t):
            s = _dot(k_tile(t), qt)
            m_old = m_sc[...]
            p = jnp.exp2(s - m_old).astype(BF16)
            m_tile = jnp.max(s, axis=0, keepdims=True)
            m_new = jnp.maximum(m_old, m_tile)
            acc_sc[...] = (acc_sc[...] + _dot(v_tile(t), p)) * jnp.exp2(m_old - m_new)
            m_sc[...] = m_new
            ex_sc[...] = jnp.maximum(ex_sc[...], m_tile - m_old)

        ex_sc[...] = jnp.zeros(ex_sc.shape, F32)
        n_peel = n % TILES_PER_PASS
        for t in range(n_peel):
            lazy_tile(t)

        def body(j, carry):
            t0 = n_peel + TILES_PER_PASS * j
            for u in range(TILES_PER_PASS):
                lazy_tile(t0 + u)
            return carry
        lax.fori_loop(0, (n - n_peel) // TILES_PER_PASS, body, 0)

        @pl.when(jnp.max(ex_sc[...]) > LAZY_HEADROOM)
        def _():
            exact_tile(kc_ref[...], vc_ref[0], first=True)

            def exact_body(t, carry):
                exact_tile(k_tile(t), v_tile(t))
                return carry
            lax.fori_loop(0, n, exact_body, 0)


    acc = acc_sc[...]
    out_t = acc / acc[ONES_LANE:ONES_LANE + 1, :]
    pad = jnp.zeros((HEAD_PAD - V_ROWS, out_t.shape[1]), F32)
    o_ref[...] = jnp.concatenate([out_t, pad], axis=0).T.astype(BF16)


def _attention(q, kc, vc, kx, vx, tq, tk):
    bsz, sq, _ = q.shape
    n_ctx = kc.shape[1]
    qspec = pl.BlockSpec((None, tq, HEAD_PAD), lambda b, h, i: (b, i, h))
    kspec = lambda n: pl.BlockSpec((None, n, HEAD_PAD), lambda b, h, i: (b, 0, h))
    vspec = lambda v: pl.BlockSpec((None, None) + v.shape[2:], lambda b, h, i: (b, h, 0, 0, 0))
    assert vc.shape[2:] == (1, V_ROWS, n_ctx)
    args = [q, kc, vc]
    in_specs = [qspec, kspec(n_ctx), vspec(vc)]
    n_x_blocks = 0

    scratch = [pltpu.VMEM((1, tq), F32), pltpu.VMEM((V_ROWS, tq), F32)]
    if kx is not None:
        s = kx.shape[1]
        n_x_blocks = s // tk
        assert vx.shape[3] == V_ROWS and tk % vx.shape[4] == 0 and vx.shape[2] * vx.shape[4] == s
        args += [kx, vx]
        in_specs += [kspec(s), vspec(vx)]
        scratch += [pltpu.VMEM((1, tq), F32)]
    return pl.pallas_call(
        functools.partial(_attn_kernel, tk=tk, n_x_blocks=n_x_blocks),
        out_shape=jax.ShapeDtypeStruct((bsz, sq, HP), BF16),
        grid=(bsz, MLA_HEADS, sq // tq),
        in_specs=in_specs,
        out_specs=qspec,
        scratch_shapes=scratch,
        compiler_params=_cparams(3),
        name="attn_x" if kx is not None else "attn_ctx",
    )(*args)


def _fft1_kernel(z_ref, m_ref, o_ref, *, g, n1):
    c = F_WIDTH
    for j in range(g):
        zz = jnp.concatenate([z_ref[0, :, j * c:(j + 1) * c], z_ref[1, :, j * c:(j + 1) * c]], axis=0)
        r = _dot(m_ref[j], zz)
        o_ref[0, j] = r[:n1].astype(BF16)
        o_ref[1, j] = r[n1:].astype(BF16)


def _fft2_kernel(w_ref, x_ref, o_ref, *, scale):
    o_ref[...] = (_dot(w_ref[...], x_ref[...]) * scale).astype(BF16)


def _dft_cos_sin(rows, cols, period):
    r = (rows[:, None] * cols[None, :]) % period
    ang = r.astype(F32) * (2.0 * math.pi / period)
    return jnp.cos(ang), jnp.sin(ang)


def _fourier_seq(z, consts):
    bsz, _, s, c = z.shape
    scale = 1.0 / math.sqrt(s * F_GROUP_DIM)
    if s <= 2 * FFT_N2:
        w2 = consts["w_direct"]
        x2 = z.reshape(bsz, 2 * s, c)
        n_out, tn = s, c
    else:
        n1, n2 = s // FFT_N2, FFT_N2
        g = 8
        x2 = pl.pallas_call(
            functools.partial(_fft1_kernel, g=g, n1=n1),
            out_shape=jax.ShapeDtypeStruct((bsz, 2, n2, n1, c), BF16),
            grid=(bsz, n2 // g),
            in_specs=[
                pl.BlockSpec((None, 2, n1, g * c), lambda b, j: (b, 0, 0, j)),
                pl.BlockSpec((g, 2 * n1, 2 * n1), lambda b, j: (j, 0, 0)),
            ],
            out_specs=pl.BlockSpec((None, 2, g, n1, c), lambda b, j: (b, 0, j, 0, 0)),
            compiler_params=_cparams(2),
            name="fft_stage1",
        )(z.reshape(bsz, 2, n1, n2 * c), consts["m1"])
        x2 = x2.reshape(bsz, 2 * n2, n1 * c)
        w2 = consts["w2"]
        n_out, tn = n2, min(2048, n1 * c)
    width = x2.shape[2]
    y = pl.pallas_call(
        functools.partial(_fft2_kernel, scale=scale),
        out_shape=jax.ShapeDtypeStruct((bsz, n_out, width), BF16),
        grid=(bsz, width // tn),
        in_specs=[
            _const_spec(w2.shape),
            pl.BlockSpec((None, x2.shape[1], tn), lambda b, j: (b, 0, j)),
        ],
        out_specs=pl.BlockSpec((None, n_out, tn), lambda b, j: (b, 0, j)),
        compiler_params=_cparams(2),
        name="fft_stage2",
    )(w2, x2)
    return y.reshape(bsz, s, c)


def _conv_kernel(uc_ref, up_ref, un_ref, wdw_ref, bdw_ref, lng_ref, lnb_ref, wpw_ref, o_ref, ext_sc, *, tm, rc):
    i = pl.program_id(1)
    last = pl.num_programs(1) - 1
    ext_sc[0:CONV_HALO] = jnp.where(i > 0, up_ref[...].astype(F32), 0.0)
    ext_sc[CONV_HALO:CONV_HALO + tm] = uc_ref[...].astype(F32)
    ext_sc[CONV_HALO + tm:2 * CONV_HALO + tm] = jnp.where(i < last, un_ref[...].astype(F32), 0.0)
    pad = CONV_KERNEL // 2
    for r0 in range(0, tm, rc):
        acc = jnp.zeros((rc, CONV_WIDTH), F32)
        for k in range(CONV_KERNEL):
            start = CONV_HALO + r0 + k - pad
            acc = acc + ext_sc[start:start + rc, :] * wdw_ref[k:k + 1, :]
        acc = acc + bdw_ref[...]
        mu = jnp.mean(acc, axis=-1, keepdims=True)
        xc = acc - mu
        var = jnp.mean(xc * xc, axis=-1, keepdims=True)
        y = xc * lax.rsqrt(var + EPS) * lng_ref[...] + lnb_ref[...]
        y = y * jax.nn.sigmoid(y)
        o_ref[r0:r0 + rc, :] = _dot(y.astype(BF16), wpw_ref[...]).astype(BF16)


def _conv_module(u, w, tm):
    bsz, s, c = u.shape
    hb = tm // CONV_HALO
    n_halo = s // CONV_HALO
    return pl.pallas_call(
        functools.partial(_conv_kernel, tm=tm, rc=64),
        out_shape=jax.ShapeDtypeStruct((bsz, s, c), BF16),
        grid=(bsz, s // tm),
        in_specs=[
            pl.BlockSpec((None, tm, c), lambda b, i: (b, i, 0)),
            pl.BlockSpec((None, CONV_HALO, c), lambda b, i: (b, jnp.maximum(i * hb - 1, 0), 0)),
            pl.BlockSpec((None, CONV_HALO, c), lambda b, i: (b, jnp.minimum((i + 1) * hb, n_halo - 1), 0)),
            _const_spec((32, c)), _const_spec((1, c)), _const_spec((1, c)), _const_spec((1, c)),
            _const_spec((c, c)),
        ],
        out_specs=pl.BlockSpec((None, tm, c), lambda b, i: (b, i, 0)),
        scratch_shapes=[pltpu.VMEM((tm + 2 * CONV_HALO, c), F32)],
        compiler_params=_cparams(2),
        name="conv",
    )(u, u, u, w["w_dw"], w["b_dw"], w["ln_g"], w["ln_b"], w["w_pw2"])


def _mix_kernel(x_ref, gate_ref, yf_ref, at_ref, yc_ref, wf_ref, wof_ref, woa_ref, woc_ref, o_ref):
    yf = _dot(yf_ref[...], wf_ref[...]).astype(BF16)
    mix = _dot(yf, wof_ref[...]) + _dot(at_ref[...], woa_ref[...]) + _dot(yc_ref[...], woc_ref[...])
    o_ref[...] = x_ref[...] + gate_ref[...] * mix


def _mix(x, mods, mod_row, yf, attn, yc, w, tm):
    bsz, s, d = x.shape
    tok = lambda width: pl.BlockSpec((None, tm, width), lambda b, i: (b, i, 0))
    return pl.pallas_call(
        _mix_kernel,
        out_shape=jax.ShapeDtypeStruct((bsz, s, d), F32),
        grid=(bsz, s // tm),
        in_specs=[
            tok(d),
            pl.BlockSpec((None, None, 1, d), lambda b, i: (mod_row(b), 2, 0, 0)),
            tok(F_WIDTH), tok(HP), tok(CONV_WIDTH),
            _const_spec((F_WIDTH, F_WIDTH)), _const_spec((F_WIDTH, d)), _const_spec((HP, d)),
            _const_spec((CONV_WIDTH, d)),
        ],
        out_specs=tok(d),
        compiler_params=_cparams(2),
        name="mix",
    )(x, mods, yf, attn, yc, w["w_f"], w["wo_f"], w["wo_a"], w["wo_c"])


def _mlp_kernel(x_ref, shift_ref, scale_ref, gate_ref, g2_ref, w1_ref, w2_ref, gf_ref, o_ref, *, ff_chunk, final_norm):
    x = x_ref[...]
    h = (_rms(x, g2_ref[...]) * (1.0 + scale_ref[...]) + shift_ref[...]).astype(BF16)
    d_ff = w1_ref.shape[1]
    acc = jnp.zeros(x.shape, F32)
    for c0 in range(0, d_ff, ff_chunk):
        a = jnp.maximum(_dot(h, w1_ref[:, c0:c0 + ff_chunk]), 0.0)
        acc = acc + _dot((a * a).astype(BF16), w2_ref[c0:c0 + ff_chunk, :])
    y = x + gate_ref[...] * acc
    if final_norm:
        y = _rms(y, gf_ref[...])
    o_ref[...] = y


def _mlp(x, mods, mod_row, g2, w1, w2, gf, tm, final_norm):
    bsz, s, d = x.shape
    d_ff = w1.shape[1]
    row = lambda j: pl.BlockSpec((None, None, 1, d), lambda b, i: (mod_row(b), j, 0, 0))
    tok = pl.BlockSpec((None, tm, d), lambda b, i: (b, i, 0))
    return pl.pallas_call(
        functools.partial(_mlp_kernel, ff_chunk=1024, final_norm=final_norm),
        out_shape=jax.ShapeDtypeStruct((bsz, s, d), F32),
        grid=(bsz, s // tm),
        in_specs=[tok, row(3), row(4), row(5), _const_spec((1, d)),
                  _const_spec((d, d_ff)), _const_spec((d_ff, d)), _const_spec((1, d))],
        out_specs=tok,
        compiler_params=_cparams(2),
        name="mlp",
    )(x, mods, mods, mods, g2, w1, w2, gf)


def _rope_tables(s):
    t = jnp.arange(s)
    row = (t // GRID_W).astype(F32)
    col = (t % GRID_W).astype(F32)
    inv = ROPE_BASE ** (-jnp.arange(0, ROPE_AXIS_DIM, 2, dtype=F32) / ROPE_AXIS_DIM)
    ang = jnp.concatenate([row[:, None] * inv, col[:, None] * inv], axis=-1)
    cos, sin = jnp.cos(ang), jnp.sin(ang)
    z_lo = jnp.zeros((s, QK_NOPE_DIM), F32)
    z_hi = jnp.zeros((s, HEAD_PAD - QK_NOPE_DIM - QK_ROPE_DIM), F32)
    ta = jnp.concatenate([z_lo, cos, cos, z_hi], axis=-1)
    tb = jnp.concatenate([z_lo, -sin, sin, z_hi], axis=-1)
    return ta, tb


def _identity_rope_tables(s):
    lane = np.arange(HEAD_PAD)
    ta = ((lane >= QK_NOPE_DIM) & (lane < QK_NOPE_DIM + QK_ROPE_DIM)).astype(np.float32)
    return jnp.asarray(np.tile(ta, (s, 1))), jnp.zeros((s, HEAD_PAD), F32)


def _fourier_consts(s, n_ctx):
    ar = lambda n: jnp.arange(n, dtype=jnp.int32)
    cc, sc = _dft_cos_sin(ar(F_GROUP_DIM), ar(F_GROUP_DIM), F_GROUP_DIM)
    eye = jnp.eye(F_GROUPS, dtype=F32)
    dft_c = jnp.concatenate([jnp.kron(eye, cc), -jnp.kron(eye, sc)], axis=1).astype(BF16)
    consts = {"dft_c": dft_c}

    def direct(n):
        c, sn = _dft_cos_sin(ar(n), ar(n), n)
        return jnp.concatenate([c, sn], axis=1).astype(BF16)

    def two_stage(n):
        n1, n2 = n // FFT_N2, FFT_N2
        pos = ar(n1)[None, :] * n2 + ar(n2)[:, None]
        r = (ar(n1)[None, :, None] * pos[:, None, :]) % n
        ang = r.astype(F32) * (2.0 * math.pi / n)
        c, sn = jnp.cos(ang), jnp.sin(ang)
        m1 = jnp.concatenate([jnp.concatenate([c, sn], axis=2),
                              jnp.concatenate([-sn, c], axis=2)], axis=1).astype(BF16)
        return m1, direct(n2)

    for n, tag in ((s, "x"), (n_ctx, "ctx")):
        if n <= 2 * FFT_N2:
            consts[tag] = {"w_direct": direct(n)}
        else:
            m1, w2 = two_stage(n)
            consts[tag] = {"m1": m1, "w2": w2}
    return consts


def _pad_heads(w, width):
    k = w.shape[0]
    w = w.reshape(k, MLA_HEADS, width)
    return jnp.pad(w, ((0, 0), (0, 0), (0, HEAD_PAD - width))).reshape(k, HP)


def _layer_weights(l, p, dft_c):
    d = p["w_in"].shape[1]
    w_in = p["w_in"][l]
    s0, s1, s2 = F_WIDTH, F_WIDTH + Q_LORA_RANK, F_WIDTH + Q_LORA_RANK + KV_LORA_RANK
    s3 = s2 + QK_ROPE_DIM
    half = QK_ROPE_DIM // 2
    w_kr = w_in[:, s2:s3]
    w_kr_sw = jnp.concatenate([w_kr[:, half:], w_kr[:, :half]], axis=1)
    lo = jnp.zeros((d, QK_NOPE_DIM), F32)
    hi = jnp.zeros((d, HEAD_PAD - QK_NOPE_DIM - QK_ROPE_DIM), F32)
    w1 = jnp.concatenate([w_in[:, :s2], lo, w_kr, hi, lo, w_kr_sw, hi, w_in[:, s3:]], axis=1).astype(BF16)

    qd = QK_NOPE_DIM + QK_ROPE_DIM
    w_uq = p["w_uq"][l].reshape(Q_LORA_RANK, MLA_HEADS, qd)
    rope = w_uq[:, :, QK_NOPE_DIM:]
    rope_sw = jnp.concatenate([rope[:, :, half:], rope[:, :, :half]], axis=2)
    wqa = _pad_heads(w_uq.reshape(Q_LORA_RANK, MLA_HEADS * qd), qd).astype(BF16)
    wqb = jnp.pad(rope_sw, ((0, 0), (0, 0), (QK_NOPE_DIM, HEAD_PAD - qd))).reshape(Q_LORA_RANK, HP).astype(BF16)

    w_ukv = p["w_ukv"][l].reshape(KV_LORA_RANK, MLA_HEADS, QK_NOPE_DIM + V_HEAD_DIM)
    wk = _pad_heads(w_ukv[:, :, :QK_NOPE_DIM].reshape(KV_LORA_RANK, -1), QK_NOPE_DIM).astype(BF16)
    wv = jnp.pad(w_ukv[:, :, QK_NOPE_DIM:], ((0, 0), (0, 0), (0, V_ROWS - V_HEAD_DIM)))
    wv = wv.reshape(KV_LORA_RANK, MLA_HEADS * V_ROWS).T.astype(BF16)
    vone = np.zeros((MLA_HEADS * V_ROWS, 1), np.float32)
    vone[ONES_LANE::V_ROWS, 0] = 1.0

    w_o = p["w_o"][l]
    a0, a1 = F_WIDTH, F_WIDTH + MLA_HEADS * V_HEAD_DIM
    wo_a = jnp.pad(w_o[a0:a1].reshape(MLA_HEADS, V_HEAD_DIM, d),
                   ((0, 0), (0, HEAD_PAD - V_HEAD_DIM), (0, 0))).reshape(HP, d).astype(BF16)
    w_dw = jnp.pad(p["w_dw"][l], ((0, 32 - CONV_KERNEL), (0, 0)))
    r2 = lambda v: v.reshape(1, -1)
    return {
        "w1": w1, "qg": r2(p["q_norm_g"][l]), "wqa": wqa, "wqb": wqb,
        "kvg": r2(p["kv_norm_g"][l]), "wk": wk, "wv": wv, "dft_c": dft_c, "vone": jnp.asarray(vone),
        "w_dw": w_dw, "b_dw": r2(p["b_dw"][l]), "ln_g": r2(p["conv_ln_g"][l]), "ln_b": r2(p["conv_ln_b"][l]),
        "w_pw2": p["w_pw2"][l].astype(BF16),
        "w_f": p["w_fourier"][l].astype(BF16), "wo_f": w_o[:a0].astype(BF16), "wo_a": wo_a,
        "wo_c": w_o[a1:].astype(BF16),
        "g1": r2(p["norm1_g"][l]), "g2": r2(p["norm2_g"][l]),
        "w_mlp1": p["w_mlp1"][l].astype(BF16), "w_mlp2": p["w_mlp2"][l].astype(BF16),
    }


def _token_tile(s):
    for tm in (512, 256, 128):
        if s % tm == 0:
            return tm
    raise ValueError(f"sequence length {s} must be a multiple of 128")


def kernel(x, c, ctx, c_ctx, w_mod, b_mod, norm1_g, w_in, q_norm_g, w_uq, kv_norm_g, w_ukv, w_fourier,
           w_dw, b_dw, conv_ln_g, conv_ln_b, w_pw2, w_o, norm2_g, w_mlp1, w_mlp2, final_norm_g):
    bsz, s, d = x.shape
    n_ctx = ctx.shape[1]
    depth = w_mod.shape[0]
    assert bsz < 8 and s % GRID_W == 0 and s % LANES == 0 and n_ctx % LANES == 0
    p = dict(w_in=w_in, w_uq=w_uq, w_ukv=w_ukv, q_norm_g=q_norm_g, kv_norm_g=kv_norm_g,
             w_fourier=w_fourier, w_dw=w_dw, b_dw=b_dw, conv_ln_g=conv_ln_g, conv_ln_b=conv_ln_b,
             w_pw2=w_pw2, w_o=w_o, norm1_g=norm1_g, norm2_g=norm2_g, w_mlp1=w_mlp1, w_mlp2=w_mlp2)

    c_rows = jnp.zeros((8, d), F32).at[:bsz].set(c).at[bsz].set(c_ctx)
    mods_all = _modulation(c_rows, w_mod, b_mod).reshape(depth, 8, N_MOD, 1, d)
    x_row = lambda b: b
    ctx_row = lambda b: bsz

    tm_x, tm_c = _token_tile(s), _token_tile(n_ctx)
    tq = next(t for t in (ATTN_QUERY_TILE, tm_x) if s % t == 0)
    tk = next(t for t in (ATTN_KEY_TILE, tm_x) if s % t == 0 and t % tm_x == 0)
    ta_x, tb_x = _rope_tables(s)
    ta_c, tb_c = _identity_rope_tables(n_ctx)
    fc = _fourier_consts(s, n_ctx)
    gf = final_norm_g.reshape(1, d)

    for l in range(depth):
        last = l == depth - 1
        w = _layer_weights(l, p, fc["dft_c"])
        mods = mods_all[l]

        qx, kx, vx, zx, ux = _prep(x, mods, x_row, w["g1"], w, ta_x, tb_x, tm_x)
        qc, kc, vc, zc, uc = _prep(ctx, mods, ctx_row, w["g1"], w, ta_c, tb_c, tm_c)

        attn_x = _attention(qx, kc, vc, kx, vx, tq, tk)
        yf_x = _fourier_seq(zx, fc["x"])
        yc_x = _conv_module(ux, w, tm_x)
        x = _mix(x, mods, x_row, yf_x, attn_x, yc_x, w, tm_x)
        x = _mlp(x, mods, x_row, w["g2"], w["w_mlp1"], w["w_mlp2"], gf, tm_x, final_norm=last)

        if not last:
            attn_c = _attention(qc, kc, vc, None, None, tm_c, tk)
            yf_c = _fourier_seq(zc, fc["ctx"])
            yc_c = _conv_module(uc, w, tm_c)
            ctx = _mix(ctx, mods, ctx_row, yf_c, attn_c, yc_c, w, tm_c)
            ctx = _mlp(ctx, mods, ctx_row, w["g2"], w["w_mlp1"], w["w_mlp2"], gf, tm_c, final_norm=False)
    return x
```

```python
import functools
import math

import numpy as np
import jax
import jax.numpy as jnp
from jax import lax
from jax.experimental import pallas as pl
from jax.experimental.pallas import tpu as pltpu

F32 = jnp.float32
BF16 = jnp.bfloat16

GRID_W = 64
F_GROUPS = 4
F_GROUP_DIM = 64
F_WIDTH = F_GROUPS * F_GROUP_DIM
MLA_HEADS = 8
QK_NOPE_DIM = 64
QK_ROPE_DIM = 32
V_HEAD_DIM = 64
Q_LORA_RANK = 256
KV_LORA_RANK = 128
CONV_WIDTH = 256
CONV_KERNEL = 31
N_MOD = 6
ROPE_BASE = 10000.0
ROPE_AXIS_DIM = QK_ROPE_DIM // 2
ATTN_SCALE = (QK_NOPE_DIM + QK_ROPE_DIM) ** -0.5
EPS = 1e-6

LANES = 128
HEAD_PAD = LANES
HP = MLA_HEADS * HEAD_PAD
ONES_LANE = V_HEAD_DIM
V_ROWS = 80
CONV_HALO = 16
FFT_N2 = 128
VMEM_LIMIT = 56 * 1024 * 1024
TILES_PER_PASS = 10
LAZY_HEADROOM = 60.0
ATTN_KEY_TILE = 512
ATTN_QUERY_TILE = 1024
Q_SCALE = ATTN_SCALE * math.log2(math.e)


def _cparams(n_axes, flags=None):
    return pltpu.CompilerParams(dimension_semantics=("parallel",) * n_axes,
                                vmem_limit_bytes=VMEM_LIMIT, flags=flags)


def _const_spec(shape):
    nd = len(shape)
    return pl.BlockSpec(shape, lambda *_: (0,) * nd, pipeline_mode=pl.Buffered(1))


def _dot(a, b):
    return jnp.dot(a, b, preferred_element_type=F32)


def _split_bf16(v):
    hi = v.astype(BF16)
    lo = (v - hi.astype(F32)).astype(BF16)
    return hi, lo


def _mod_kernel(c_ref, w_ref, b_ref, o_ref):
    c = c_ref[...]
    s = c * jax.nn.sigmoid(c)
    s_hi, s_lo = _split_bf16(s)
    w_hi, w_lo = _split_bf16(w_ref[...])
    o_ref[...] = _dot(s_hi, w_hi) + _dot(s_hi, w_lo) + _dot(s_lo, w_hi) + b_ref[...]


def _modulation(c_rows, w_mod, b_mod):
    n_layers, d, n6 = w_mod.shape
    tn = 1536
    return pl.pallas_call(
        _mod_kernel,
        out_shape=jax.ShapeDtypeStruct((n_layers, 8, n6), F32),
        grid=(n_layers, n6 // tn),
        in_specs=[
            pl.BlockSpec((8, d), lambda l, j: (0, 0)),
            pl.BlockSpec((None, d, tn), lambda l, j: (l, 0, j)),
            pl.BlockSpec((None, 1, tn), lambda l, j: (l, 0, j)),
        ],
        out_specs=pl.BlockSpec((None, 8, tn), lambda l, j: (l, 0, j)),
        compiler_params=_cparams(2),
        name="modulation",
    )(c_rows, w_mod, b_mod.reshape(n_layers, 1, n6))


def _rms(v, g):
    return v * lax.rsqrt(jnp.mean(v * v, axis=-1, keepdims=True) + EPS) * g


def _prep_kernel(x_ref, shift_ref, scale_ref, g1_ref, w1_ref, qg_ref, wqa_ref, wqb_ref,
                 kvg_ref, wk_ref, wv_ref, dft_ref, ta_ref, tb_ref, vone_ref,
                 q_ref, k_ref, v_ref, z_ref, u_ref):
    x = x_ref[...]
    h = (_rms(x, g1_ref[...]) * (1.0 + scale_ref[...]) + shift_ref[...]).astype(BF16)
    z = _dot(h, w1_ref[...])

    zz = _dot(z[:, 0:F_WIDTH].astype(BF16), dft_ref[...])
    z_ref[0] = zz[:, :F_WIDTH]
    z_ref[1] = zz[:, F_WIDTH:]

    ta = ta_ref[...]
    tb = tb_ref[...]
    lane = lax.broadcasted_iota(jnp.int32, (1, HEAD_PAD), 1)
    nope = (lane < QK_NOPE_DIM).astype(F32)

    c0 = F_WIDTH
    cqn = _rms(z[:, c0:c0 + Q_LORA_RANK], qg_ref[...]).astype(BF16)
    qa = _dot(cqn, wqa_ref[...])
    qb = _dot(cqn, wqb_ref[...])
    qta = (ta + nope) * Q_SCALE
    qtb = tb * Q_SCALE
    for hd in range(MLA_HEADS):
        sl = slice(hd * HEAD_PAD, (hd + 1) * HEAD_PAD)
        q_ref[:, sl] = (qa[:, sl] * qta + qb[:, sl] * qtb).astype(BF16)

    c1 = c0 + Q_LORA_RANK
    ckvn = _rms(z[:, c1:c1 + KV_LORA_RANK], kvg_ref[...]).astype(BF16)
    kn = _dot(ckvn, wk_ref[...])
    c2 = c1 + KV_LORA_RANK
    kr = z[:, c2:c2 + HEAD_PAD] * ta + z[:, c2 + HEAD_PAD:c2 + 2 * HEAD_PAD] * tb
    for hd in range(MLA_HEADS):
        sl = slice(hd * HEAD_PAD, (hd + 1) * HEAD_PAD)
        k_ref[:, sl] = (kn[:, sl] + kr).astype(BF16)
    vt = lax.dot_general(wv_ref[...], ckvn, (((1,), (1,)), ((), ())), preferred_element_type=F32)
    vt = vt + vone_ref[...]
    for hd in range(MLA_HEADS):
        v_ref[hd] = vt[hd * V_ROWS:(hd + 1) * V_ROWS, :].astype(BF16)

    c3 = c2 + 2 * HEAD_PAD
    a = z[:, c3:c3 + CONV_WIDTH]
    g = z[:, c3 + CONV_WIDTH:c3 + 2 * CONV_WIDTH]
    u_ref[...] = (a * jax.nn.sigmoid(g)).astype(BF16)


def _prep(x, mods, mod_row, g1, w, ta, tb, tm):
    bsz, s, d = x.shape
    n1 = w["w1"].shape[1]
    row = lambda j: pl.BlockSpec((None, None, 1, d), lambda b, i: (mod_row(b), j, 0, 0))
    tok = lambda width: pl.BlockSpec((None, tm, width), lambda b, i: (b, i, 0))
    return pl.pallas_call(
        _prep_kernel,
        out_shape=(
            jax.ShapeDtypeStruct((bsz, s, HP), BF16),
            jax.ShapeDtypeStruct((bsz, s, HP), BF16),
            jax.ShapeDtypeStruct((bsz, MLA_HEADS, s // tm, V_ROWS, tm), BF16),
            jax.ShapeDtypeStruct((bsz, 2, s, F_WIDTH), F32),
            jax.ShapeDtypeStruct((bsz, s, CONV_WIDTH), BF16),
        ),
        grid=(bsz, s // tm),
        in_specs=[
            tok(d), row(0), row(1), _const_spec((1, d)),
            _const_spec((d, n1)),
            _const_spec((1, Q_LORA_RANK)), _const_spec((Q_LORA_RANK, HP)), _const_spec((Q_LORA_RANK, HP)),
            _const_spec((1, KV_LORA_RANK)), _const_spec((KV_LORA_RANK, HP)),
            _const_spec((MLA_HEADS * V_ROWS, KV_LORA_RANK)),
            _const_spec((F_WIDTH, 2 * F_WIDTH)),
            pl.BlockSpec((tm, HEAD_PAD), lambda b, i: (i, 0)),
            pl.BlockSpec((tm, HEAD_PAD), lambda b, i: (i, 0)),
            _const_spec((MLA_HEADS * V_ROWS, 1)),
        ],
        out_specs=(
            tok(HP), tok(HP),
            pl.BlockSpec((None, MLA_HEADS, None, V_ROWS, tm), lambda b, i: (b, 0, i, 0, 0)),
            pl.BlockSpec((None, 2, tm, F_WIDTH), lambda b, i: (b, 0, i, 0)),
            tok(CONV_WIDTH),
        ),
        compiler_params=_cparams(2),
        name="prep",
    )(x, mods, mods, g1, w["w1"], w["qg"], w["wqa"], w["wqb"], w["kvg"], w["wk"], w["wv"],
      w["dft_c"], ta, tb, w["vone"])


def _attn_kernel(*refs, tk, n_x_blocks):
    if n_x_blocks:
        q_ref, kc_ref, vc_ref, kx_ref, vx_ref, o_ref, m_sc, acc_sc, ex_sc = refs
    else:
        q_ref, kc_ref, vc_ref, o_ref, m_sc, acc_sc = refs
    qt = q_ref[...].astype(F32).T.astype(BF16)

    def exact_tile(k, vt, first=False):
        s = _dot(k, qt)
        m_tile = jnp.max(s, axis=0, keepdims=True)
        m_new = m_tile if first else jnp.maximum(m_sc[...], m_tile)
        p = jnp.exp2(s - m_new).astype(BF16)
        pv = _dot(vt, p)
        acc_sc[...] = pv if first else jnp.exp2(m_sc[...] - m_new) * acc_sc[...] + pv
        m_sc[...] = m_new

    exact_tile(kc_ref[...], vc_ref[0], first=True)
    if n_x_blocks:
        n = n_x_blocks
        v_sub = tk // vx_ref.shape[2]

        def k_tile(t):
            return kx_ref[pl.ds(pl.multiple_of(t * tk, tk), tk), :]

        def v_tile(t):
            return jnp.concatenate([vx_ref[v_sub * t + i] for i in range(v_sub)], axis=1)

        def lazy_tile(t):
            s = _dot(k_tile(t), qt)
            m_old = m_sc[...]
            p = jnp.exp2(s - m_old).astype(BF16)
            m_tile = jnp.max(s, axis=0, keepdims=True)
            m_new = jnp.maximum(m_old, m_tile)
            acc_sc[...] = (acc_sc[...] + _dot(v_tile(t), p)) * jnp.exp2(m_old - m_new)
            m_sc[...] = m_new
            ex_sc[...] = jnp.maximum(ex_sc[...], m_tile - m_old)

        ex_sc[...] = jnp.zeros(ex_sc.shape, F32)
        n_peel = n % TILES_PER_PASS
        for t in range(n_peel):
            lazy_tile(t)

        def body(j, carry):
            t0 = n_peel + TILES_PER_PASS * j
            for u in range(TILES_PER_PASS):
                lazy_tile(t0 + u)
            return carry
        lax.fori_loop(0, (n - n_peel) // TILES_PER_PASS, body, 0)

        @pl.when(jnp.max(ex_sc[...]) > LAZY_HEADROOM)
        def _():
            exact_tile(kc_ref[...], vc_ref[0], first=True)

            def exact_body(t, carry):
                exact_tile(k_tile(t), v_tile(t))
                return carry
            lax.fori_loop(0, n, exact_body, 0)


    acc = acc_sc[...]
    out_t = acc / acc[ONES_LANE:ONES_LANE + 1, :]
    pad = jnp.zeros((HEAD_PAD - V_ROWS, out_t.shape[1]), F32)
    o_ref[...] = jnp.concatenate([out_t, pad], axis=0).T.astype(BF16)


def _attention(q, kc, vc, kx, vx, tq, tk):
    bsz, sq, _ = q.shape
    n_ctx = kc.shape[1]
    qspec = pl.BlockSpec((None, tq, HEAD_PAD), lambda b, h, i: (b, i, h))
    kspec = lambda n: pl.BlockSpec((None, n, HEAD_PAD), lambda b, h, i: (b, 0, h))
    vspec = lambda v: pl.BlockSpec((None, None) + v.shape[2:], lambda b, h, i: (b, h, 0, 0, 0))
    assert vc.shape[2:] == (1, V_ROWS, n_ctx)
    args = [q, kc, vc]
    in_specs = [qspec, kspec(n_ctx), vspec(vc)]
    n_x_blocks = 0

    scratch = [pltpu.VMEM((1, tq), F32), pltpu.VMEM((V_ROWS, tq), F32)]
    if kx is not None:
        s = kx.shape[1]
        n_x_blocks = s // tk
        assert vx.shape[3] == V_ROWS and tk % vx.shape[4] == 0 and vx.shape[2] * vx.shape[4] == s
        args += [kx, vx]
        in_specs += [kspec(s), vspec(vx)]
        scratch += [pltpu.VMEM((1, tq), F32)]
    return pl.pallas_call(
        functools.partial(_attn_kernel, tk=tk, n_x_blocks=n_x_blocks),
        out_shape=jax.ShapeDtypeStruct((bsz, sq, HP), BF16),
        grid=(bsz, MLA_HEADS, sq // tq),
        in_specs=in_specs,
        out_specs=qspec,
        scratch_shapes=scratch,
        compiler_params=_cparams(3),
        name="attn_x" if kx is not None else "attn_ctx",
    )(*args)


def _fft1_kernel(z_ref, m_ref, o_ref, *, g, n1):
    for j in range(g):
        zz = jnp.concatenate([z_ref[0, :, j, :], z_ref[1, :, j, :]], axis=0).astype(BF16)
        r = _dot(m_ref[j], zz)
        o_ref[0, j] = r[:n1]
        o_ref[1, j] = r[n1:]


def _fft2_kernel(w_ref, x_ref, o_ref, *, g, scale):
    n2 = x_ref.shape[1]
    for j in range(g):
        xx = jnp.concatenate([x_ref[0, :, j, :], x_ref[1, :, j, :]], axis=0).astype(BF16)
        o_ref[:, j, :] = _dot(w_ref[...], xx) * scale


def _dft_direct_kernel(w_ref, x_ref, o_ref, *, scale):
    o_ref[...] = _dot(w_ref[...], x_ref[...].astype(BF16)) * scale


def _dft_cos_sin(rows, cols, period):
    r = (rows[:, None] * cols[None, :]) % period
    ang = r.astype(F32) * (2.0 * math.pi / period)
    return jnp.cos(ang), jnp.sin(ang)


def _fourier_seq(z, consts):
    bsz, _, s, c = z.shape
    scale = 1.0 / math.sqrt(s * F_GROUP_DIM)
    if s <= 2 * FFT_N2:
        w = consts["w_direct"]
        return pl.pallas_call(
            functools.partial(_dft_direct_kernel, scale=scale),
            out_shape=jax.ShapeDtypeStruct((bsz, s, c), F32),
            grid=(bsz,),
            in_specs=[_const_spec(w.shape), pl.BlockSpec((None, 2 * s, c), lambda b: (b, 0, 0))],
            out_specs=pl.BlockSpec((None, s, c), lambda b: (b, 0, 0)),
            compiler_params=_cparams(1),
            name="dft_direct",
        )(w, z.reshape(bsz, 2 * s, c))
    n1, n2 = s // FFT_N2, FFT_N2
    g = 8
    x2 = pl.pallas_call(
        functools.partial(_fft1_kernel, g=g, n1=n1),
        out_shape=jax.ShapeDtypeStruct((bsz, 2, n2, n1, c), F32),
        grid=(bsz, n2 // g),
        in_specs=[
            pl.BlockSpec((None, 2, n1, g, c), lambda b, j: (b, 0, 0, j, 0)),
            pl.BlockSpec((g, 2 * n1, 2 * n1), lambda b, j: (j, 0, 0)),
        ],
        out_specs=pl.BlockSpec((None, 2, g, n1, c), lambda b, j: (b, 0, j, 0, 0)),
        compiler_params=_cparams(2),
        name="fft_stage1",
    )(z.reshape(bsz, 2, n1, n2, c), consts["m1"])
    w2 = consts["w2"]
    y = pl.pallas_call(
        functools.partial(_fft2_kernel, g=g, scale=scale),
        out_shape=jax.ShapeDtypeStruct((bsz, n2, n1, c), F32),
        grid=(bsz, n1 // g),
        in_specs=[
            _const_spec(w2.shape),
            pl.BlockSpec((None, 2, n2, g, c), lambda b, j: (b, 0, 0, j, 0)),
        ],
        out_specs=pl.BlockSpec((None, n2, g, c), lambda b, j: (b, 0, j, 0)),
        compiler_params=_cparams(2),
        name="fft_stage2",
    )(w2, x2)
    return y.reshape(bsz, s, c)


def _conv_kernel(uc_ref, up_ref, un_ref, wdw_ref, bdw_ref, lng_ref, lnb_ref, wpw_ref, o_ref, ext_sc, sh_sc,
                 *, tm, rc):
    i = pl.program_id(1)
    last = pl.num_programs(1) - 1
    ext_sc[0:CONV_HALO] = jnp.where(i > 0, up_ref[...].astype(F32), 0.0)
    ext_sc[CONV_HALO:CONV_HALO + tm] = uc_ref[...].astype(F32)
    ext_sc[CONV_HALO + tm:2 * CONV_HALO + tm] = jnp.where(i < last, un_ref[...].astype(F32), 0.0)
    first = CONV_HALO - CONV_KERNEL // 2
    sub = sh_sc.shape[0]
    for r in range(sub):
        sh_sc[r] = ext_sc[r:r + sh_sc.shape[1], :]
    for r0 in range(0, tm, rc):
        acc = jnp.zeros((rc, CONV_WIDTH), F32)
        for k in range(CONV_KERNEL):
            r = (first + k) % sub
            lo = r0 + (first + k) - r
            acc = acc + sh_sc[r, lo:lo + rc, :] * wdw_ref[k:k + 1, :]
        acc = acc + bdw_ref[...]
        mu = jnp.mean(acc, axis=-1, keepdims=True)
        xc = acc - mu
        var = jnp.mean(xc * xc, axis=-1, keepdims=True)
        y = xc * lax.rsqrt(var + EPS) * lng_ref[...] + lnb_ref[...]
        y = y * jax.nn.sigmoid(y)
        o_ref[r0:r0 + rc, :] = _dot(y.astype(BF16), wpw_ref[...]).astype(BF16)


def _conv_module(u, w, tm):
    bsz, s, c = u.shape
    hb = tm // CONV_HALO
    n_halo = s // CONV_HALO
    return pl.pallas_call(
        functools.partial(_conv_kernel, tm=tm, rc=64),
        out_shape=jax.ShapeDtypeStruct((bsz, s, c), BF16),
        grid=(bsz, s // tm),
        in_specs=[
            pl.BlockSpec((None, tm, c), lambda b, i: (b, i, 0)),
            pl.BlockSpec((None, CONV_HALO, c), lambda b, i: (b, jnp.maximum(i * hb - 1, 0), 0)),
            pl.BlockSpec((None, CONV_HALO, c), lambda b, i: (b, jnp.minimum((i + 1) * hb, n_halo - 1), 0)),
            _const_spec((32, c)), _const_spec((1, c)), _const_spec((1, c)), _const_spec((1, c)),
            _const_spec((c, c)),
        ],
        out_specs=pl.BlockSpec((None, tm, c), lambda b, i: (b, i, 0)),
        scratch_shapes=[pltpu.VMEM((tm + 2 * CONV_HALO, c), F32),
                        pltpu.VMEM((8, tm + 2 * CONV_HALO - 8, c), F32)],
        compiler_params=_cparams(2),
        name="conv",
    )(u, u, u, w["w_dw"], w["b_dw"], w["ln_g"], w["ln_b"], w["w_pw2"])


def _mix_mlp_kernel(x_ref, gate1_ref, shift_ref, scale_ref, gate2_ref, yf_ref, at_ref, yc_ref,
                    wf_ref, wof_ref, woa_ref, woc_ref, g2_ref, w1_ref, w2_ref, gf_ref, o_ref,
                    *, ff_chunk, final_norm):
    yf = _dot(yf_ref[...].astype(BF16), wf_ref[...]).astype(BF16)
    mix = _dot(yf, wof_ref[...]) + _dot(at_ref[...], woa_ref[...]) + _dot(yc_ref[...], woc_ref[...])
    x = x_ref[...] + gate1_ref[...] * mix
    h = (_rms(x, g2_ref[...]) * (1.0 + scale_ref[...]) + shift_ref[...]).astype(BF16)
    d_ff = w1_ref.shape[1]
    acc = jnp.zeros(x.shape, F32)
    for c0 in range(0, d_ff, ff_chunk):
        a = jnp.maximum(_dot(h, w1_ref[:, c0:c0 + ff_chunk]), 0.0)
        acc = acc + _dot((a * a).astype(BF16), w2_ref[c0:c0 + ff_chunk, :])
    y = x + gate2_ref[...] * acc
    if final_norm:
        y = _rms(y, gf_ref[...])
    o_ref[...] = y


def _mix_mlp(x, mods, mod_row, yf, attn, yc, w, gf, tm, final_norm):
    bsz, s, d = x.shape
    d_ff = w["w_mlp1"].shape[1]
    row = lambda j: pl.BlockSpec((None, None, 1, d), lambda b, i: (mod_row(b), j, 0, 0))
    tok = lambda width: pl.BlockSpec((None, tm, width), lambda b, i: (b, i, 0))
    return pl.pallas_call(
        functools.partial(_mix_mlp_kernel, ff_chunk=1024, final_norm=final_norm),
        out_shape=jax.ShapeDtypeStruct((bsz, s, d), F32),
        grid=(bsz, s // tm),
        in_specs=[
            tok(d), row(2), row(3), row(4), row(5),
            tok(F_WIDTH), tok(HP), tok(CONV_WIDTH),
            _const_spec((F_WIDTH, F_WIDTH)), _const_spec((F_WIDTH, d)), _const_spec((HP, d)),
            _const_spec((CONV_WIDTH, d)),
            _const_spec((1, d)), _const_spec((d, d_ff)), _const_spec((d_ff, d)), _const_spec((1, d)),
        ],
        out_specs=tok(d),
        compiler_params=_cparams(2),
        name="mix_mlp",
    )(x, mods, mods, mods, mods, yf, attn, yc, w["w_f"], w["wo_f"], w["wo_a"], w["wo_c"],
      w["g2"], w["w_mlp1"], w["w_mlp2"], gf)


def _rope_tables(s):
    t = jnp.arange(s)
    row = (t // GRID_W).astype(F32)
    col = (t % GRID_W).astype(F32)
    inv = ROPE_BASE ** (-jnp.arange(0, ROPE_AXIS_DIM, 2, dtype=F32) / ROPE_AXIS_DIM)
    ang = jnp.concatenate([row[:, None] * inv, col[:, None] * inv], axis=-1)
    cos, sin = jnp.cos(ang), jnp.sin(ang)
    z_lo = jnp.zeros((s, QK_NOPE_DIM), F32)
    z_hi = jnp.zeros((s, HEAD_PAD - QK_NOPE_DIM - QK_ROPE_DIM), F32)
    ta = jnp.concatenate([z_lo, cos, cos, z_hi], axis=-1)
    tb = jnp.concatenate([z_lo, -sin, sin, z_hi], axis=-1)
    return ta, tb


def _identity_rope_tables(s):
    lane = np.arange(HEAD_PAD)
    ta = ((lane >= QK_NOPE_DIM) & (lane < QK_NOPE_DIM + QK_ROPE_DIM)).astype(np.float32)
    return jnp.asarray(np.tile(ta, (s, 1))), jnp.zeros((s, HEAD_PAD), F32)


def _fourier_consts(s, n_ctx):
    ar = lambda n: jnp.arange(n, dtype=jnp.int32)
    cc, sc = _dft_cos_sin(ar(F_GROUP_DIM), ar(F_GROUP_DIM), F_GROUP_DIM)
    eye = jnp.eye(F_GROUPS, dtype=F32)
    dft_c = jnp.concatenate([jnp.kron(eye, cc), -jnp.kron(eye, sc)], axis=1).astype(BF16)
    consts = {"dft_c": dft_c}

    def direct(n):
        c, sn = _dft_cos_sin(ar(n), ar(n), n)
        return jnp.concatenate([c, sn], axis=1).astype(BF16)

    def two_stage(n):
        n1, n2 = n // FFT_N2, FFT_N2
        pos = ar(n1)[None, :] * n2 + ar(n2)[:, None]
        r = (ar(n1)[None, :, None] * pos[:, None, :]) % n
        ang = r.astype(F32) * (2.0 * math.pi / n)
        c, sn = jnp.cos(ang), jnp.sin(ang)
        m1 = jnp.concatenate([jnp.concatenate([c, sn], axis=2),
                              jnp.concatenate([-sn, c], axis=2)], axis=1).astype(BF16)
        return m1, direct(n2)

    for n, tag in ((s, "x"), (n_ctx, "ctx")):
        if n <= 2 * FFT_N2:
            consts[tag] = {"w_direct": direct(n)}
        else:
            m1, w2 = two_stage(n)
            consts[tag] = {"m1": m1, "w2": w2}
    return consts


def _pad_heads(w, width):
    k = w.shape[0]
    w = w.reshape(k, MLA_HEADS, width)
    return jnp.pad(w, ((0, 0), (0, 0), (0, HEAD_PAD - width))).reshape(k, HP)


def _layer_weights(l, p, dft_c):
    d = p["w_in"].shape[1]
    w_in = p["w_in"][l]
    s0, s1, s2 = F_WIDTH, F_WIDTH + Q_LORA_RANK, F_WIDTH + Q_LORA_RANK + KV_LORA_RANK
    s3 = s2 + QK_ROPE_DIM
    half = QK_ROPE_DIM // 2
    w_kr = w_in[:, s2:s3]
    w_kr_sw = jnp.concatenate([w_kr[:, half:], w_kr[:, :half]], axis=1)
    lo = jnp.zeros((d, QK_NOPE_DIM), F32)
    hi = jnp.zeros((d, HEAD_PAD - QK_NOPE_DIM - QK_ROPE_DIM), F32)
    w1 = jnp.concatenate([w_in[:, :s2], lo, w_kr, hi, lo, w_kr_sw, hi, w_in[:, s3:]], axis=1).astype(BF16)

    qd = QK_NOPE_DIM + QK_ROPE_DIM
    w_uq = p["w_uq"][l].reshape(Q_LORA_RANK, MLA_HEADS, qd)
    rope = w_uq[:, :, QK_NOPE_DIM:]
    rope_sw = jnp.concatenate([rope[:, :, half:], rope[:, :, :half]], axis=2)
    wqa = _pad_heads(w_uq.reshape(Q_LORA_RANK, MLA_HEADS * qd), qd).astype(BF16)
    wqb = jnp.pad(rope_sw, ((0, 0), (0, 0), (QK_NOPE_DIM, HEAD_PAD - qd))).reshape(Q_LORA_RANK, HP).astype(BF16)

    w_ukv = p["w_ukv"][l].reshape(KV_LORA_RANK, MLA_HEADS, QK_NOPE_DIM + V_HEAD_DIM)
    wk = _pad_heads(w_ukv[:, :, :QK_NOPE_DIM].reshape(KV_LORA_RANK, -1), QK_NOPE_DIM).astype(BF16)
    wv = jnp.pad(w_ukv[:, :, QK_NOPE_DIM:], ((0, 0), (0, 0), (0, V_ROWS - V_HEAD_DIM)))
    wv = wv.reshape(KV_LORA_RANK, MLA_HEADS * V_ROWS).T.astype(BF16)
    vone = np.zeros((MLA_HEADS * V_ROWS, 1), np.float32)
    vone[ONES_LANE::V_ROWS, 0] = 1.0

    w_o = p["w_o"][l]
    a0, a1 = F_WIDTH, F_WIDTH + MLA_HEADS * V_HEAD_DIM
    wo_a = jnp.pad(w_o[a0:a1].reshape(MLA_HEADS, V_HEAD_DIM, d),
                   ((0, 0), (0, HEAD_PAD - V_HEAD_DIM), (0, 0))).reshape(HP, d).astype(BF16)
    w_dw = jnp.pad(p["w_dw"][l], ((0, 32 - CONV_KERNEL), (0, 0)))
    r2 = lambda v: v.reshape(1, -1)
    return {
        "w1": w1, "qg": r2(p["q_norm_g"][l]), "wqa": wqa, "wqb": wqb,
        "kvg": r2(p["kv_norm_g"][l]), "wk": wk, "wv": wv, "dft_c": dft_c, "vone": jnp.asarray(vone),
        "w_dw": w_dw, "b_dw": r2(p["b_dw"][l]), "ln_g": r2(p["conv_ln_g"][l]), "ln_b": r2(p["conv_ln_b"][l]),
        "w_pw2": p["w_pw2"][l].astype(BF16),
        "w_f": p["w_fourier"][l].astype(BF16), "wo_f": w_o[:a0].astype(BF16), "wo_a": wo_a,
        "wo_c": w_o[a1:].astype(BF16),
        "g1": r2(p["norm1_g"][l]), "g2": r2(p["norm2_g"][l]),
        "w_mlp1": p["w_mlp1"][l].astype(BF16), "w_mlp2": p["w_mlp2"][l].astype(BF16),
    }


def _token_tile(s):
    for tm in (512, 256, 128):
        if s % tm == 0:
            return tm
    raise ValueError(f"sequence length {s} must be a multiple of 128")


def kernel(x, c, ctx, c_ctx, w_mod, b_mod, norm1_g, w_in, q_norm_g, w_uq, kv_norm_g, w_ukv, w_fourier,
           w_dw, b_dw, conv_ln_g, conv_ln_b, w_pw2, w_o, norm2_g, w_mlp1, w_mlp2, final_norm_g):
    bsz, s, d = x.shape
    n_ctx = ctx.shape[1]
    depth = w_mod.shape[0]
    assert bsz < 8 and s % GRID_W == 0 and s % LANES == 0 and n_ctx % LANES == 0
    p = dict(w_in=w_in, w_uq=w_uq, w_ukv=w_ukv, q_norm_g=q_norm_g, kv_norm_g=kv_norm_g,
             w_fourier=w_fourier, w_dw=w_dw, b_dw=b_dw, conv_ln_g=conv_ln_g, conv_ln_b=conv_ln_b,
             w_pw2=w_pw2, w_o=w_o, norm1_g=norm1_g, norm2_g=norm2_g, w_mlp1=w_mlp1, w_mlp2=w_mlp2)

    c_rows = jnp.zeros((8, d), F32).at[:bsz].set(c).at[bsz].set(c_ctx)
    mods_all = _modulation(c_rows, w_mod, b_mod).reshape(depth, 8, N_MOD, 1, d)
    x_row = lambda b: b
    ctx_row = lambda b: bsz

    tm_x, tm_c = _token_tile(s), _token_tile(n_ctx)
    tq = next(t for t in (ATTN_QUERY_TILE, tm_x) if s % t == 0)
    tk = next(t for t in (ATTN_KEY_TILE, tm_x) if s % t == 0 and t % tm_x == 0)
    ta_x, tb_x = _rope_tables(s)
    ta_c, tb_c = _identity_rope_tables(n_ctx)
    fc = _fourier_consts(s, n_ctx)
    gf = final_norm_g.reshape(1, d)

    for l in range(depth):
        last = l == depth - 1
        w = _layer_weights(l, p, fc["dft_c"])
        mods = mods_all[l]

        qx, kx, vx, zx, ux = _prep(x, mods, x_row, w["g1"], w, ta_x, tb_x, tm_x)
        qc, kc, vc, zc, uc = _prep(ctx, mods, ctx_row, w["g1"], w, ta_c, tb_c, tm_c)

        attn_x = _attention(qx, kc, vc, kx, vx, tq, tk)
        yf_x = _fourier_seq(zx, fc["x"])
        yc_x = _conv_module(ux, w, tm_x)
        x = _mix_mlp(x, mods, x_row, yf_x, attn_x, yc_x, w, gf, tm_x, final_norm=last)

        if not last:
            attn_c = _attention(qc, kc, vc, None, None, tm_c, tk)
            yf_c = _fourier_seq(zc, fc["ctx"])
            yc_c = _conv_module(uc, w, tm_c)
            ctx = _mix_mlp(ctx, mods, ctx_row, yf_c, attn_c, yc_c, w, gf, tm_c, final_norm=False)
    return x
```

```python
import functools
import math

import numpy as np
import jax
import jax.numpy as jnp
from jax import lax
from jax.experimental import pallas as pl
from jax.experimental.pallas import tpu as pltpu

F32 = jnp.float32
BF16 = jnp.bfloat16

GRID_W = 64
F_GROUPS = 4
F_GROUP_DIM = 64
F_WIDTH = F_GROUPS * F_GROUP_DIM
MLA_HEADS = 8
QK_NOPE_DIM = 64
QK_ROPE_DIM = 32
V_HEAD_DIM = 64
Q_LORA_RANK = 256
KV_LORA_RANK = 128
CONV_WIDTH = 256
CONV_KERNEL = 31
N_MOD = 6
ROPE_BASE = 10000.0
ROPE_AXIS_DIM = QK_ROPE_DIM // 2
ATTN_SCALE = (QK_NOPE_DIM + QK_ROPE_DIM) ** -0.5
EPS = 1e-6

LANES = 128
HEAD_PAD = LANES
HP = MLA_HEADS * HEAD_PAD
ONES_LANE = V_HEAD_DIM
V_ROWS = 80
CONV_HALO = 16
FFT_N2 = 128
VMEM_LIMIT = 56 * 1024 * 1024
TILES_PER_PASS = 8
LAZY_HEADROOM = 60.0
ATTN_KEY_TILE = 512
ATTN_QUERY_TILE = 1024
Q_SCALE = ATTN_SCALE * math.log2(math.e)


def _cparams(n_axes, flags=None):
    return pltpu.CompilerParams(dimension_semantics=("parallel",) * n_axes,
                                vmem_limit_bytes=VMEM_LIMIT, flags=flags)


def _const_spec(shape):
    nd = len(shape)
    return pl.BlockSpec(shape, lambda *_: (0,) * nd, pipeline_mode=pl.Buffered(1))


def _dot(a, b):
    return jnp.dot(a, b, preferred_element_type=F32)


def _split_bf16(v):
    hi = v.astype(BF16)
    lo = (v - hi.astype(F32)).astype(BF16)
    return hi, lo


def _mod_kernel(c_ref, w_ref, b_ref, o_ref):
    c = c_ref[...]
    s = c * jax.nn.sigmoid(c)
    s_hi, s_lo = _split_bf16(s)
    w_hi, w_lo = _split_bf16(w_ref[...])
    o_ref[...] = _dot(s_hi, w_hi) + _dot(s_hi, w_lo) + _dot(s_lo, w_hi) + b_ref[...]


def _modulation(c_rows, w_mod, b_mod):
    n_layers, d, n6 = w_mod.shape
    tn = 1536
    return pl.pallas_call(
        _mod_kernel,
        out_shape=jax.ShapeDtypeStruct((n_layers, 8, n6), F32),
        grid=(n_layers, n6 // tn),
        in_specs=[
            pl.BlockSpec((8, d), lambda l, j: (0, 0)),
            pl.BlockSpec((None, d, tn), lambda l, j: (l, 0, j)),
            pl.BlockSpec((None, 1, tn), lambda l, j: (l, 0, j)),
        ],
        out_specs=pl.BlockSpec((None, 8, tn), lambda l, j: (l, 0, j)),
        compiler_params=_cparams(2),
        name="modulation",
    )(c_rows, w_mod, b_mod.reshape(n_layers, 1, n6))


def _rms(v, g):
    return v * lax.rsqrt(jnp.mean(v * v, axis=-1, keepdims=True) + EPS) * g


def _prep_kernel(x_ref, shift_ref, scale_ref, g1_ref, w1_ref, qg_ref, wqa_ref, wqb_ref,
                 kvg_ref, wk_ref, wv_ref, dft_ref, ta_ref, tb_ref, vone_ref,
                 q_ref, k_ref, v_ref, z_ref, u_ref):
    x = x_ref[...]
    h = (_rms(x, g1_ref[...]) * (1.0 + scale_ref[...]) + shift_ref[...]).astype(BF16)
    z = _dot(h, w1_ref[...])

    zz = _dot(z[:, 0:F_WIDTH].astype(BF16), dft_ref[...])
    z_ref[0] = zz[:, :F_WIDTH]
    z_ref[1] = zz[:, F_WIDTH:]

    ta = ta_ref[...]
    tb = tb_ref[...]
    lane = lax.broadcasted_iota(jnp.int32, (1, HEAD_PAD), 1)
    nope = (lane < QK_NOPE_DIM).astype(F32)

    c0 = F_WIDTH
    cqn = _rms(z[:, c0:c0 + Q_LORA_RANK], qg_ref[...]).astype(BF16)
    qa = _dot(cqn, wqa_ref[...])
    qb = _dot(cqn, wqb_ref[...])
    qta = (ta + nope) * Q_SCALE
    qtb = tb * Q_SCALE
    for hd in range(MLA_HEADS):
        sl = slice(hd * HEAD_PAD, (hd + 1) * HEAD_PAD)
        q_ref[:, sl] = (qa[:, sl] * qta + qb[:, sl] * qtb).astype(BF16)

    c1 = c0 + Q_LORA_RANK
    ckvn = _rms(z[:, c1:c1 + KV_LORA_RANK], kvg_ref[...]).astype(BF16)
    kn = _dot(ckvn, wk_ref[...])
    c2 = c1 + KV_LORA_RANK
    kr = z[:, c2:c2 + HEAD_PAD] * ta + z[:, c2 + HEAD_PAD:c2 + 2 * HEAD_PAD] * tb
    for hd in range(MLA_HEADS):
        sl = slice(hd * HEAD_PAD, (hd + 1) * HEAD_PAD)
        k_ref[:, sl] = (kn[:, sl] + kr).astype(BF16)
    vt = lax.dot_general(wv_ref[...], ckvn, (((1,), (1,)), ((), ())), preferred_element_type=F32)
    vt = vt + vone_ref[...]
    for hd in range(MLA_HEADS):
        v_ref[hd] = vt[hd * V_ROWS:(hd + 1) * V_ROWS, :].astype(BF16)

    c3 = c2 + 2 * HEAD_PAD
    a = z[:, c3:c3 + CONV_WIDTH]
    g = z[:, c3 + CONV_WIDTH:c3 + 2 * CONV_WIDTH]
    u_ref[...] = (a * jax.nn.sigmoid(g)).astype(BF16)


def _prep(x, mods, mod_row, g1, w, ta, tb, tm):
    bsz, s, d = x.shape
    n1 = w["w1"].shape[1]
    row = lambda j: pl.BlockSpec((None, None, 1, d), lambda b, i: (mod_row(b), j, 0, 0))
    tok = lambda width: pl.BlockSpec((None, tm, width), lambda b, i: (b, i, 0))
    return pl.pallas_call(
        _prep_kernel,
        out_shape=(
            jax.ShapeDtypeStruct((bsz, s, HP), BF16),
            jax.ShapeDtypeStruct((bsz, s, HP), BF16),
            jax.ShapeDtypeStruct((bsz, MLA_HEADS, s // tm, V_ROWS, tm), BF16),
            jax.ShapeDtypeStruct((bsz, 2, s, F_WIDTH), F32),
            jax.ShapeDtypeStruct((bsz, s, CONV_WIDTH), BF16),
        ),
        grid=(bsz, s // tm),
        in_specs=[
            tok(d), row(0), row(1), _const_spec((1, d)),
            _const_spec((d, n1)),
            _const_spec((1, Q_LORA_RANK)), _const_spec((Q_LORA_RANK, HP)), _const_spec((Q_LORA_RANK, HP)),
            _const_spec((1, KV_LORA_RANK)), _const_spec((KV_LORA_RANK, HP)),
            _const_spec((MLA_HEADS * V_ROWS, KV_LORA_RANK)),
            _const_spec((F_WIDTH, 2 * F_WIDTH)),
            pl.BlockSpec((tm, HEAD_PAD), lambda b, i: (i, 0)),
            pl.BlockSpec((tm, HEAD_PAD), lambda b, i: (i, 0)),
            _const_spec((MLA_HEADS * V_ROWS, 1)),
        ],
        out_specs=(
            tok(HP), tok(HP),
            pl.BlockSpec((None, MLA_HEADS, None, V_ROWS, tm), lambda b, i: (b, 0, i, 0, 0)),
            pl.BlockSpec((None, 2, tm, F_WIDTH), lambda b, i: (b, 0, i, 0)),
            tok(CONV_WIDTH),
        ),
        compiler_params=_cparams(2),
        name="prep",
    )(x, mods, mods, g1, w["w1"], w["qg"], w["wqa"], w["wqb"], w["kvg"], w["wk"], w["wv"],
      w["dft_c"], ta, tb, w["vone"])


def _attn_kernel(*refs, tk, n_x_blocks):
    if n_x_blocks:
        q_ref, kc_ref, vc_ref, kx_ref, vx_ref, o_ref, m_sc, acc_sc, ex_sc = refs
    else:
        q_ref, kc_ref, vc_ref, o_ref, m_sc, acc_sc = refs
    qt = q_ref[...].astype(F32).T.astype(BF16)

    def exact_tile(k, vt, first=False):
        s = _dot(k, qt)
        m_tile = jnp.max(s, axis=0, keepdims=True)
        m_new = m_tile if first else jnp.maximum(m_sc[...], m_tile)
        p = jnp.exp2(s - m_new).astype(BF16)
        pv = _dot(vt, p)
        acc_sc[...] = pv if first else jnp.exp2(m_sc[...] - m_new) * acc_sc[...] + pv
        m_sc[...] = m_new

    exact_tile(kc_ref[...], vc_ref[0], first=True)
    if n_x_blocks:
        n = n_x_blocks
        v_sub = tk // vx_ref.shape[2]

        def k_tile(t):
            return kx_ref[pl.ds(pl.multiple_of(t * tk, tk), tk), :]

        def v_tile(t):
            return jnp.concatenate([vx_ref[v_sub * t + i] for i in range(v_sub)], axis=1)

        def lazy_tile(t):
            s = _dot(k_tile(t), qt)
            m_old = m_sc[...]
            p = jnp.exp2(s - m_old).astype(BF16)
            m_tile = jnp.max(s, axis=0, keepdims=True)
            m_new = jnp.maximum(m_old, m_tile)
            acc_sc[...] = (acc_sc[...] + _dot(v_tile(t), p)) * jnp.exp2(m_old - m_new)
            m_sc[...] = m_new
            ex_sc[...] = jnp.maximum(ex_sc[...], m_tile - m_old)

        ex_sc[...] = jnp.zeros(ex_sc.shape, F32)
        n_peel = n % TILES_PER_PASS
        for t in range(n_peel):
            lazy_tile(t)

        def body(j, carry):
            t0 = n_peel + TILES_PER_PASS * j
            for u in range(TILES_PER_PASS):
                lazy_tile(t0 + u)
            return carry
        lax.fori_loop(0, (n - n_peel) // TILES_PER_PASS, body, 0)

        @pl.when(jnp.max(ex_sc[...]) > LAZY_HEADROOM)
        def _():
            exact_tile(kc_ref[...], vc_ref[0], first=True)

            def exact_body(t, carry):
                exact_tile(k_tile(t), v_tile(t))
                return carry
            lax.fori_loop(0, n, exact_body, 0)


    acc = acc_sc[...]
    out_t = acc / acc[ONES_LANE:ONES_LANE + 1, :]
    pad = jnp.zeros((HEAD_PAD - V_ROWS, out_t.shape[1]), F32)
    o_ref[...] = jnp.concatenate([out_t, pad], axis=0).T.astype(BF16)


def _attention(q, kc, vc, kx, vx, tq, tk):
    bsz, sq, _ = q.shape
    n_ctx = kc.shape[1]
    qspec = pl.BlockSpec((None, tq, HEAD_PAD), lambda b, h, i: (b, i, h))
    kspec = lambda n: pl.BlockSpec((None, n, HEAD_PAD), lambda b, h, i: (b, 0, h))
    vspec = lambda v: pl.BlockSpec((None, None) + v.shape[2:], lambda b, h, i: (b, h, 0, 0, 0))
    assert vc.shape[2:] == (1, V_ROWS, n_ctx)
    args = [q, kc, vc]
    in_specs = [qspec, kspec(n_ctx), vspec(vc)]
    n_x_blocks = 0

    scratch = [pltpu.VMEM((1, tq), F32), pltpu.VMEM((V_ROWS, tq), F32)]
    if kx is not None:
        s = kx.shape[1]
        n_x_blocks = s // tk
        assert vx.shape[3] == V_ROWS and tk % vx.shape[4] == 0 and vx.shape[2] * vx.shape[4] == s
        args += [kx, vx]
        in_specs += [kspec(s), vspec(vx)]
        scratch += [pltpu.VMEM((1, tq), F32)]
    return pl.pallas_call(
        functools.partial(_attn_kernel, tk=tk, n_x_blocks=n_x_blocks),
        out_shape=jax.ShapeDtypeStruct((bsz, sq, HP), BF16),
        grid=(bsz, MLA_HEADS, sq // tq),
        in_specs=in_specs,
        out_specs=qspec,
        scratch_shapes=scratch,
        compiler_params=_cparams(3),
        name="attn_x" if kx is not None else "attn_ctx",
    )(*args)


def _fft1_kernel(z_ref, m_ref, o_ref, *, g, n1):
    for j in range(g):
        zz = jnp.concatenate([z_ref[0, :, j, :], z_ref[1, :, j, :]], axis=0).astype(BF16)
        r = _dot(m_ref[j], zz)
        o_ref[0, j] = r[:n1]
        o_ref[1, j] = r[n1:]


def _fft2_kernel(w_ref, x_ref, o_ref, *, g, scale):
    n2 = x_ref.shape[1]
    for j in range(g):
        xx = jnp.concatenate([x_ref[0, :, j, :], x_ref[1, :, j, :]], axis=0).astype(BF16)
        o_ref[:, j, :] = _dot(w_ref[...], xx) * scale


def _dft_direct_kernel(w_ref, x_ref, o_ref, *, scale):
    o_ref[...] = _dot(w_ref[...], x_ref[...].astype(BF16)) * scale


def _dft_cos_sin(rows, cols, period):
    r = (rows[:, None] * cols[None, :]) % period
    ang = r.astype(F32) * (2.0 * math.pi / period)
    return jnp.cos(ang), jnp.sin(ang)


def _fourier_seq(z, consts):
    bsz, _, s, c = z.shape
    scale = 1.0 / math.sqrt(s * F_GROUP_DIM)
    if s <= 2 * FFT_N2:
        w = consts["w_direct"]
        return pl.pallas_call(
            functools.partial(_dft_direct_kernel, scale=scale),
            out_shape=jax.ShapeDtypeStruct((bsz, s, c), F32),
            grid=(bsz,),
            in_specs=[_const_spec(w.shape), pl.BlockSpec((None, 2 * s, c), lambda b: (b, 0, 0))],
            out_specs=pl.BlockSpec((None, s, c), lambda b: (b, 0, 0)),
            compiler_params=_cparams(1),
            name="dft_direct",
        )(w, z.reshape(bsz, 2 * s, c))
    n1, n2 = s // FFT_N2, FFT_N2
    g = 8
    x2 = pl.pallas_call(
        functools.partial(_fft1_kernel, g=g, n1=n1),
        out_shape=jax.ShapeDtypeStruct((bsz, 2, n2, n1, c), F32),
        grid=(bsz, n2 // g),
        in_specs=[
            pl.BlockSpec((None, 2, n1, g, c), lambda b, j: (b, 0, 0, j, 0)),
            pl.BlockSpec((g, 2 * n1, 2 * n1), lambda b, j: (j, 0, 0)),
        ],
        out_specs=pl.BlockSpec((None, 2, g, n1, c), lambda b, j: (b, 0, j, 0, 0)),
        compiler_params=_cparams(2),
        name="fft_stage1",
    )(z.reshape(bsz, 2, n1, n2, c), consts["m1"])
    w2 = consts["w2"]
    y = pl.pallas_call(
        functools.partial(_fft2_kernel, g=g, scale=scale),
        out_shape=jax.ShapeDtypeStruct((bsz, n2, n1, c), F32),
        grid=(bsz, n1 // g),
        in_specs=[
            _const_spec(w2.shape),
            pl.BlockSpec((None, 2, n2, g, c), lambda b, j: (b, 0, 0, j, 0)),
        ],
        out_specs=pl.BlockSpec((None, n2, g, c), lambda b, j: (b, 0, j, 0)),
        compiler_params=_cparams(2),
        name="fft_stage2",
    )(w2, x2)
    return y.reshape(bsz, s, c)


def _conv_kernel(uc_ref, up_ref, un_ref, wdw_ref, bdw_ref, lng_ref, lnb_ref, wpw_ref, o_ref, ext_sc, sh_sc,
                 *, tm, rc):
    i = pl.program_id(1)
    last = pl.num_programs(1) - 1
    ext_sc[0:CONV_HALO] = jnp.where(i > 0, up_ref[...].astype(F32), 0.0)
    ext_sc[CONV_HALO:CONV_HALO + tm] = uc_ref[...].astype(F32)
    ext_sc[CONV_HALO + tm:2 * CONV_HALO + tm] = jnp.where(i < last, un_ref[...].astype(F32), 0.0)
    first = CONV_HALO - CONV_KERNEL // 2
    sub = sh_sc.shape[0]
    for r in range(sub):
        sh_sc[r] = ext_sc[r:r + sh_sc.shape[1], :]
    for r0 in range(0, tm, rc):
        acc = jnp.zeros((rc, CONV_WIDTH), F32)
        for k in range(CONV_KERNEL):
            r = (first + k) % sub
            lo = r0 + (first + k) - r
            acc = acc + sh_sc[r, lo:lo + rc, :] * wdw_ref[k:k + 1, :]
        acc = acc + bdw_ref[...]
        mu = jnp.mean(acc, axis=-1, keepdims=True)
        xc = acc - mu
        var = jnp.mean(xc * xc, axis=-1, keepdims=True)
        y = xc * lax.rsqrt(var + EPS) * lng_ref[...] + lnb_ref[...]
        y = y * jax.nn.sigmoid(y)
        o_ref[r0:r0 + rc, :] = _dot(y.astype(BF16), wpw_ref[...]).astype(BF16)


def _conv_module(u, w, tm):
    bsz, s, c = u.shape
    hb = tm // CONV_HALO
    n_halo = s // CONV_HALO
    return pl.pallas_call(
        functools.partial(_conv_kernel, tm=tm, rc=64),
        out_shape=jax.ShapeDtypeStruct((bsz, s, c), BF16),
        grid=(bsz, s // tm),
        in_specs=[
            pl.BlockSpec((None, tm, c), lambda b, i: (b, i, 0)),
            pl.BlockSpec((None, CONV_HALO, c), lambda b, i: (b, jnp.maximum(i * hb - 1, 0), 0)),
            pl.BlockSpec((None, CONV_HALO, c), lambda b, i: (b, jnp.minimum((i + 1) * hb, n_halo - 1), 0)),
            _const_spec((32, c)), _const_spec((1, c)), _const_spec((1, c)), _const_spec((1, c)),
            _const_spec((c, c)),
        ],
        out_specs=pl.BlockSpec((None, tm, c), lambda b, i: (b, i, 0)),
        scratch_shapes=[pltpu.VMEM((tm + 2 * CONV_HALO, c), F32),
                        pltpu.VMEM((8, tm + 2 * CONV_HALO - 8, c), F32)],
        compiler_params=_cparams(2),
        name="conv",
    )(u, u, u, w["w_dw"], w["b_dw"], w["ln_g"], w["ln_b"], w["w_pw2"])


def _mix_mlp_kernel(x_ref, gate1_ref, shift_ref, scale_ref, gate2_ref, yf_ref, at_ref, yc_ref,
                    wf_ref, wof_ref, woa_ref, woc_ref, g2_ref, w1_ref, w2_ref, gf_ref, o_ref,
                    *, ff_chunk, final_norm):
    yf = _dot(yf_ref[...].astype(BF16), wf_ref[...]).astype(BF16)
    mix = _dot(yf, wof_ref[...]) + _dot(at_ref[...], woa_ref[...]) + _dot(yc_ref[...], woc_ref[...])
    x = x_ref[...] + gate1_ref[...] * mix
    h = (_rms(x, g2_ref[...]) * (1.0 + scale_ref[...]) + shift_ref[...]).astype(BF16)
    d_ff = w1_ref.shape[1]
    acc = jnp.zeros(x.shape, F32)
    for c0 in range(0, d_ff, ff_chunk):
        a = jnp.maximum(_dot(h, w1_ref[:, c0:c0 + ff_chunk]), 0.0)
        acc = acc + _dot((a * a).astype(BF16), w2_ref[c0:c0 + ff_chunk, :])
    y = x + gate2_ref[...] * acc
    if final_norm:
        y = _rms(y, gf_ref[...])
    o_ref[...] = y


def _mix_mlp(x, mods, mod_row, yf, attn, yc, w, gf, tm, final_norm):
    bsz, s, d = x.shape
    d_ff = w["w_mlp1"].shape[1]
    row = lambda j: pl.BlockSpec((None, None, 1, d), lambda b, i: (mod_row(b), j, 0, 0))
    tok = lambda width: pl.BlockSpec((None, tm, width), lambda b, i: (b, i, 0))
    return pl.pallas_call(
        functools.partial(_mix_mlp_kernel, ff_chunk=1024, final_norm=final_norm),
        out_shape=jax.ShapeDtypeStruct((bsz, s, d), F32),
        grid=(bsz, s // tm),
        in_specs=[
            tok(d), row(2), row(3), row(4), row(5),
            tok(F_WIDTH), tok(HP), tok(CONV_WIDTH),
            _const_spec((F_WIDTH, F_WIDTH)), _const_spec((F_WIDTH, d)), _const_spec((HP, d)),
            _const_spec((CONV_WIDTH, d)),
            _const_spec((1, d)), _const_spec((d, d_ff)), _const_spec((d_ff, d)), _const_spec((1, d)),
        ],
        out_specs=tok(d),
        compiler_params=_cparams(2),
        name="mix_mlp",
    )(x, mods, mods, mods, mods, yf, attn, yc, w["w_f"], w["wo_f"], w["wo_a"], w["wo_c"],
      w["g2"], w["w_mlp1"], w["w_mlp2"], gf)


def _rope_tables(s):
    t = jnp.arange(s)
    row = (t // GRID_W).astype(F32)
    col = (t % GRID_W).astype(F32)
    inv = ROPE_BASE ** (-jnp.arange(0, ROPE_AXIS_DIM, 2, dtype=F32) / ROPE_AXIS_DIM)
    ang = jnp.concatenate([row[:, None] * inv, col[:, None] * inv], axis=-1)
    cos, sin = jnp.cos(ang), jnp.sin(ang)
    z_lo = jnp.zeros((s, QK_NOPE_DIM), F32)
    z_hi = jnp.zeros((s, HEAD_PAD - QK_NOPE_DIM - QK_ROPE_DIM), F32)
    ta = jnp.concatenate([z_lo, cos, cos, z_hi], axis=-1)
    tb = jnp.concatenate([z_lo, -sin, sin, z_hi], axis=-1)
    return ta, tb


def _identity_rope_tables(s):
    lane = np.arange(HEAD_PAD)
    ta = ((lane >= QK_NOPE_DIM) & (lane < QK_NOPE_DIM + QK_ROPE_DIM)).astype(np.float32)
    return jnp.asarray(np.tile(ta, (s, 1))), jnp.zeros((s, HEAD_PAD), F32)


def _fourier_consts(s, n_ctx):
    ar = lambda n: jnp.arange(n, dtype=jnp.int32)
    cc, sc = _dft_cos_sin(ar(F_GROUP_DIM), ar(F_GROUP_DIM), F_GROUP_DIM)
    eye = jnp.eye(F_GROUPS, dtype=F32)
    dft_c = jnp.concatenate([jnp.kron(eye, cc), -jnp.kron(eye, sc)], axis=1).astype(BF16)
    consts = {"dft_c": dft_c}

    def direct(n):
        c, sn = _dft_cos_sin(ar(n), ar(n), n)
        return jnp.concatenate([c, sn], axis=1).astype(BF16)

    def two_stage(n):
        n1, n2 = n // FFT_N2, FFT_N2
        pos = ar(n1)[None, :] * n2 + ar(n2)[:, None]
        r = (ar(n1)[None, :, None] * pos[:, None, :]) % n
        ang = r.astype(F32) * (2.0 * math.pi / n)
        c, sn = jnp.cos(ang), jnp.sin(ang)
        m1 = jnp.concatenate([jnp.concatenate([c, sn], axis=2),
                              jnp.concatenate([-sn, c], axis=2)], axis=1).astype(BF16)
        return m1, direct(n2)

    for n, tag in ((s, "x"), (n_ctx, "ctx")):
        if n <= 2 * FFT_N2:
            consts[tag] = {"w_direct": direct(n)}
        else:
            m1, w2 = two_stage(n)
            consts[tag] = {"m1": m1, "w2": w2}
    return consts


def _pad_heads(w, width):
    k = w.shape[0]
    w = w.reshape(k, MLA_HEADS, width)
    return jnp.pad(w, ((0, 0), (0, 0), (0, HEAD_PAD - width))).reshape(k, HP)


def _layer_weights(l, p, dft_c):
    d = p["w_in"].shape[1]
    w_in = p["w_in"][l]
    s0, s1, s2 = F_WIDTH, F_WIDTH + Q_LORA_RANK, F_WIDTH + Q_LORA_RANK + KV_LORA_RANK
    s3 = s2 + QK_ROPE_DIM
    half = QK_ROPE_DIM // 2
    w_kr = w_in[:, s2:s3]
    w_kr_sw = jnp.concatenate([w_kr[:, half:], w_kr[:, :half]], axis=1)
    lo = jnp.zeros((d, QK_NOPE_DIM), F32)
    hi = jnp.zeros((d, HEAD_PAD - QK_NOPE_DIM - QK_ROPE_DIM), F32)
    w1 = jnp.concatenate([w_in[:, :s2], lo, w_kr, hi, lo, w_kr_sw, hi, w_in[:, s3:]], axis=1).astype(BF16)

    qd = QK_NOPE_DIM + QK_ROPE_DIM
    w_uq = p["w_uq"][l].reshape(Q_LORA_RANK, MLA_HEADS, qd)
    rope = w_uq[:, :, QK_NOPE_DIM:]
    rope_sw = jnp.concatenate([rope[:, :, half:], rope[:, :, :half]], axis=2)
    wqa = _pad_heads(w_uq.reshape(Q_LORA_RANK, MLA_HEADS * qd), qd).astype(BF16)
    wqb = jnp.pad(rope_sw, ((0, 0), (0, 0), (QK_NOPE_DIM, HEAD_PAD - qd))).reshape(Q_LORA_RANK, HP).astype(BF16)

    w_ukv = p["w_ukv"][l].reshape(KV_LORA_RANK, MLA_HEADS, QK_NOPE_DIM + V_HEAD_DIM)
    wk = _pad_heads(w_ukv[:, :, :QK_NOPE_DIM].reshape(KV_LORA_RANK, -1), QK_NOPE_DIM).astype(BF16)
    wv = jnp.pad(w_ukv[:, :, QK_NOPE_DIM:], ((0, 0), (0, 0), (0, V_ROWS - V_HEAD_DIM)))
    wv = wv.reshape(KV_LORA_RANK, MLA_HEADS * V_ROWS).T.astype(BF16)
    vone = np.zeros((MLA_HEADS * V_ROWS, 1), np.float32)
    vone[ONES_LANE::V_ROWS, 0] = 1.0

    w_o = p["w_o"][l]
    a0, a1 = F_WIDTH, F_WIDTH + MLA_HEADS * V_HEAD_DIM
    wo_a = jnp.pad(w_o[a0:a1].reshape(MLA_HEADS, V_HEAD_DIM, d),
                   ((0, 0), (0, HEAD_PAD - V_HEAD_DIM), (0, 0))).reshape(HP, d).astype(BF16)
    w_dw = jnp.pad(p["w_dw"][l], ((0, 32 - CONV_KERNEL), (0, 0)))
    r2 = lambda v: v.reshape(1, -1)
    return {
        "w1": w1, "qg": r2(p["q_norm_g"][l]), "wqa": wqa, "wqb": wqb,
        "kvg": r2(p["kv_norm_g"][l]), "wk": wk, "wv": wv, "dft_c": dft_c, "vone": jnp.asarray(vone),
        "w_dw": w_dw, "b_dw": r2(p["b_dw"][l]), "ln_g": r2(p["conv_ln_g"][l]), "ln_b": r2(p["conv_ln_b"][l]),
        "w_pw2": p["w_pw2"][l].astype(BF16),
        "w_f": p["w_fourier"][l].astype(BF16), "wo_f": w_o[:a0].astype(BF16), "wo_a": wo_a,
        "wo_c": w_o[a1:].astype(BF16),
        "g1": r2(p["norm1_g"][l]), "g2": r2(p["norm2_g"][l]),
        "w_mlp1": p["w_mlp1"][l].astype(BF16), "w_mlp2": p["w_mlp2"][l].astype(BF16),
    }


def _token_tile(s):
    for tm in (512, 256, 128):
        if s % tm == 0:
            return tm
    raise ValueError(f"sequence length {s} must be a multiple of 128")


def kernel(x, c, ctx, c_ctx, w_mod, b_mod, norm1_g, w_in, q_norm_g, w_uq, kv_norm_g, w_ukv, w_fourier,
           w_dw, b_dw, conv_ln_g, conv_ln_b, w_pw2, w_o, norm2_g, w_mlp1, w_mlp2, final_norm_g):
    bsz, s, d = x.shape
    n_ctx = ctx.shape[1]
    depth = w_mod.shape[0]
    assert bsz < 8 and s % GRID_W == 0 and s % LANES == 0 and n_ctx % LANES == 0
    p = dict(w_in=w_in, w_uq=w_uq, w_ukv=w_ukv, q_norm_g=q_norm_g, kv_norm_g=kv_norm_g,
             w_fourier=w_fourier, w_dw=w_dw, b_dw=b_dw, conv_ln_g=conv_ln_g, conv_ln_b=conv_ln_b,
             w_pw2=w_pw2, w_o=w_o, norm1_g=norm1_g, norm2_g=norm2_g, w_mlp1=w_mlp1, w_mlp2=w_mlp2)

    c_rows = jnp.zeros((8, d), F32).at[:bsz].set(c).at[bsz].set(c_ctx)
    mods_all = _modulation(c_rows, w_mod, b_mod).reshape(depth, 8, N_MOD, 1, d)
    x_row = lambda b: b
    ctx_row = lambda b: bsz

    tm_x, tm_c = _token_tile(s), _token_tile(n_ctx)
    tq = next(t for t in (ATTN_QUERY_TILE, tm_x) if s % t == 0)
    tk = next(t for t in (ATTN_KEY_TILE, tm_x) if s % t == 0 and t % tm_x == 0)
    ta_x, tb_x = _rope_tables(s)
    ta_c, tb_c = _identity_rope_tables(n_ctx)
    fc = _fourier_consts(s, n_ctx)
    gf = final_norm_g.reshape(1, d)

    for l in range(depth):
        last = l == depth - 1
        w = _layer_weights(l, p, fc["dft_c"])
        mods = mods_all[l]

        qx, kx, vx, zx, ux = _prep(x, mods, x_row, w["g1"], w, ta_x, tb_x, tm_x)
        qc, kc, vc, zc, uc = _prep(ctx, mods, ctx_row, w["g1"], w, ta_c, tb_c, tm_c)

        attn_x = _attention(qx, kc, vc, kx, vx, tq, tk)
        yf_x = _fourier_seq(zx, fc["x"])
        yc_x = _conv_module(ux, w, tm_x)
        x = _mix_mlp(x, mods, x_row, yf_x, attn_x, yc_x, w, gf, tm_x, final_norm=last)

        if not last:
            attn_c = _attention(qc, kc, vc, None, None, tm_c, tk)
            yf_c = _fourier_seq(zc, fc["ctx"])
            yc_c = _conv_module(uc, w, tm_c)
            ctx = _mix_mlp(ctx, mods, ctx_row, yf_c, attn_c, yc_c, w, gf, tm_c, final_norm=False)
    return x
```

```python
import functools
import math

import numpy as np
import jax
import jax.numpy as jnp
from jax import lax
from jax.experimental import pallas as pl
from jax.experimental.pallas import tpu as pltpu

F32 = jnp.float32
BF16 = jnp.bfloat16

GRID_W = 64
F_GROUPS = 4
F_GROUP_DIM = 64
F_WIDTH = F_GROUPS * F_GROUP_DIM
MLA_HEADS = 8
QK_NOPE_DIM = 64
QK_ROPE_DIM = 32
V_HEAD_DIM = 64
Q_LORA_RANK = 256
KV_LORA_RANK = 128
CONV_WIDTH = 256
CONV_KERNEL = 31
N_MOD = 6
ROPE_BASE = 10000.0
ROPE_AXIS_DIM = QK_ROPE_DIM // 2
ATTN_SCALE = (QK_NOPE_DIM + QK_ROPE_DIM) ** -0.5
EPS = 1e-6

LANES = 128
HEAD_PAD = LANES
HP = MLA_HEADS * HEAD_PAD
V_ROWS = V_HEAD_DIM
CONV_HALO = 16
FFT_N2 = 128
VMEM_LIMIT = 56 * 1024 * 1024
TILES_PER_PASS = 10
LAZY_HEADROOM = 60.0
ATTN_KEY_TILE = 512
ATTN_QUERY_TILE = 1024
Q_SCALE = ATTN_SCALE * math.log2(math.e)


def _cparams(n_axes, flags=None):
    return pltpu.CompilerParams(dimension_semantics=("parallel",) * n_axes,
                                vmem_limit_bytes=VMEM_LIMIT, flags=flags)


def _const_spec(shape):
    nd = len(shape)
    return pl.BlockSpec(shape, lambda *_: (0,) * nd, pipeline_mode=pl.Buffered(1))


def _dot(a, b):
    return jnp.dot(a, b, preferred_element_type=F32)


def _split_bf16(v):
    hi = v.astype(BF16)
    lo = (v - hi.astype(F32)).astype(BF16)
    return hi, lo


def _mod_kernel(c_ref, w_ref, b_ref, o_ref):
    c = c_ref[...]
    s = c * jax.nn.sigmoid(c)
    s_hi, s_lo = _split_bf16(s)
    w_hi, w_lo = _split_bf16(w_ref[...])
    o_ref[...] = _dot(s_hi, w_hi) + _dot(s_hi, w_lo) + _dot(s_lo, w_hi) + b_ref[...]


def _modulation(c_rows, w_mod, b_mod):
    n_layers, d, n6 = w_mod.shape
    tn = 1536
    return pl.pallas_call(
        _mod_kernel,
        out_shape=jax.ShapeDtypeStruct((n_layers, 8, n6), F32),
        grid=(n_layers, n6 // tn),
        in_specs=[
            pl.BlockSpec((8, d), lambda l, j: (0, 0)),
            pl.BlockSpec((None, d, tn), lambda l, j: (l, 0, j)),
            pl.BlockSpec((None, 1, tn), lambda l, j: (l, 0, j)),
        ],
        out_specs=pl.BlockSpec((None, 8, tn), lambda l, j: (l, 0, j)),
        compiler_params=_cparams(2),
        name="modulation",
    )(c_rows, w_mod, b_mod.reshape(n_layers, 1, n6))


def _rms(v, g):
    return v * lax.rsqrt(jnp.mean(v * v, axis=-1, keepdims=True) + EPS) * g


def _prep_kernel(x_ref, shift_ref, scale_ref, g1_ref, w1_ref, qg_ref, wqa_ref, wqb_ref,
                 kvg_ref, wk_ref, wv_ref, dft_ref, ta_ref, tb_ref,
                 q_ref, k_ref, v_ref, z_ref, u_ref):
    x = x_ref[...]
    h = (_rms(x, g1_ref[...]) * (1.0 + scale_ref[...]) + shift_ref[...]).astype(BF16)
    z = _dot(h, w1_ref[...])

    zz = _dot(z[:, 0:F_WIDTH].astype(BF16), dft_ref[...])
    z_ref[0] = zz[:, :F_WIDTH]
    z_ref[1] = zz[:, F_WIDTH:]

    ta = ta_ref[...]
    tb = tb_ref[...]
    lane = lax.broadcasted_iota(jnp.int32, (1, HEAD_PAD), 1)
    nope = (lane < QK_NOPE_DIM).astype(F32)

    c0 = F_WIDTH
    cqn = _rms(z[:, c0:c0 + Q_LORA_RANK], qg_ref[...]).astype(BF16)
    qa = _dot(cqn, wqa_ref[...])
    qb = _dot(cqn, wqb_ref[...])
    qta = (ta + nope) * Q_SCALE
    qtb = tb * Q_SCALE
    for hd in range(MLA_HEADS):
        sl = slice(hd * HEAD_PAD, (hd + 1) * HEAD_PAD)
        q_ref[:, sl] = (qa[:, sl] * qta + qb[:, sl] * qtb).astype(BF16)

    c1 = c0 + Q_LORA_RANK
    ckvn = _rms(z[:, c1:c1 + KV_LORA_RANK], kvg_ref[...]).astype(BF16)
    kn = _dot(ckvn, wk_ref[...])
    c2 = c1 + KV_LORA_RANK
    kr = z[:, c2:c2 + HEAD_PAD] * ta + z[:, c2 + HEAD_PAD:c2 + 2 * HEAD_PAD] * tb
    for hd in range(MLA_HEADS):
        sl = slice(hd * HEAD_PAD, (hd + 1) * HEAD_PAD)
        k_ref[:, sl] = (kn[:, sl] + kr).astype(BF16)
    vt = lax.dot_general(wv_ref[...], ckvn, (((1,), (1,)), ((), ())), preferred_element_type=F32)
    for hd in range(MLA_HEADS):
        v_ref[hd] = vt[hd * V_ROWS:(hd + 1) * V_ROWS, :].astype(BF16)

    c3 = c2 + 2 * HEAD_PAD
    a = z[:, c3:c3 + CONV_WIDTH]
    g = z[:, c3 + CONV_WIDTH:c3 + 2 * CONV_WIDTH]
    u_ref[...] = (a * jax.nn.sigmoid(g)).astype(BF16)


def _prep(x, mods, mod_row, g1, w, ta, tb, tm):
    bsz, s, d = x.shape
    n1 = w["w1"].shape[1]
    row = lambda j: pl.BlockSpec((None, None, 1, d), lambda b, i: (mod_row(b), j, 0, 0))
    tok = lambda width: pl.BlockSpec((None, tm, width), lambda b, i: (b, i, 0))
    return pl.pallas_call(
        _prep_kernel,
        out_shape=(
            jax.ShapeDtypeStruct((bsz, s, HP), BF16),
            jax.ShapeDtypeStruct((bsz, s, HP), BF16),
            jax.ShapeDtypeStruct((bsz, MLA_HEADS, s // tm, V_ROWS, tm), BF16),
            jax.ShapeDtypeStruct((bsz, 2, s, F_WIDTH), F32),
            jax.ShapeDtypeStruct((bsz, s, CONV_WIDTH), BF16),
        ),
        grid=(bsz, s // tm),
        in_specs=[
            tok(d), row(0), row(1), _const_spec((1, d)),
            _const_spec((d, n1)),
            _const_spec((1, Q_LORA_RANK)), _const_spec((Q_LORA_RANK, HP)), _const_spec((Q_LORA_RANK, HP)),
            _const_spec((1, KV_LORA_RANK)), _const_spec((KV_LORA_RANK, HP)),
            _const_spec((MLA_HEADS * V_ROWS, KV_LORA_RANK)),
            _const_spec((F_WIDTH, 2 * F_WIDTH)),
            pl.BlockSpec((tm, HEAD_PAD), lambda b, i: (i, 0)),
            pl.BlockSpec((tm, HEAD_PAD), lambda b, i: (i, 0)),
        ],
        out_specs=(
            tok(HP), tok(HP),
            pl.BlockSpec((None, MLA_HEADS, None, V_ROWS, tm), lambda b, i: (b, 0, i, 0, 0)),
            pl.BlockSpec((None, 2, tm, F_WIDTH), lambda b, i: (b, 0, i, 0)),
            tok(CONV_WIDTH),
        ),
        compiler_params=_cparams(2),
        name="prep",
    )(x, mods, mods, g1, w["w1"], w["qg"], w["wqa"], w["wqb"], w["kvg"], w["wk"], w["wv"],
      w["dft_c"], ta, tb)


def _attn_kernel(*refs, tk, n_x_blocks):
    if n_x_blocks:
        q_ref, kc_ref, vc_ref, kx_ref, vx_ref, o_ref, m_sc, acc_sc, l_sc, ex_sc = refs
    else:
        q_ref, kc_ref, vc_ref, o_ref, m_sc, acc_sc, l_sc = refs
    qt = q_ref[...].astype(F32).T.astype(BF16)

    def key_sums(p):
        return jnp.sum(p.reshape(p.shape[0] // 8, 8, p.shape[1]), axis=0)

    def exact_tile(k, vt, first=False):
        s = _dot(k, qt)
        m_tile = jnp.max(s, axis=0, keepdims=True)
        m_new = m_tile if first else jnp.maximum(m_sc[...], m_tile)
        p = jnp.exp2(s - m_new)
        pv = _dot(vt, p.astype(BF16))
        if first:
            acc_sc[...] = pv
            l_sc[...] = key_sums(p)
        else:
            alpha = jnp.exp2(m_sc[...] - m_new)
            acc_sc[...] = alpha * acc_sc[...] + pv
            l_sc[...] = alpha * l_sc[...] + key_sums(p)
        m_sc[...] = m_new

    exact_tile(kc_ref[...], vc_ref[0], first=True)
    if n_x_blocks:
        n = n_x_blocks
        v_sub = tk // vx_ref.shape[2]

        def k_tile(t):
            return kx_ref[pl.ds(pl.multiple_of(t * tk, tk), tk), :]

        def v_tile(t):
            return jnp.concatenate([vx_ref[v_sub * t + i] for i in range(v_sub)], axis=1)

        def lazy_tile(t):
            s = _dot(k_tile(t), qt)
            m_old = m_sc[...]
            p = jnp.exp2(s - m_old)
            m_tile = jnp.max(s, axis=0, keepdims=True)
            m_new = jnp.maximum(m_old, m_tile)
            alpha = jnp.exp2(m_old - m_new)
            acc_sc[...] = (acc_sc[...] + _dot(v_tile(t), p.astype(BF16))) * alpha
            l_sc[...] = (l_sc[...] + key_sums(p)) * alpha
            m_sc[...] = m_new
            ex_sc[...] = jnp.maximum(ex_sc[...], m_tile - m_old)

        ex_sc[...] = jnp.zeros(ex_sc.shape, F32)
        n_peel = n % TILES_PER_PASS
        for t in range(n_peel):
            lazy_tile(t)

        def body(j, carry):
            t0 = n_peel + TILES_PER_PASS * j
            for u in range(TILES_PER_PASS):
                lazy_tile(t0 + u)
            return carry
        lax.fori_loop(0, (n - n_peel) // TILES_PER_PASS, body, 0)

        @pl.when(jnp.max(ex_sc[...]) > LAZY_HEADROOM)
        def _():
            exact_tile(kc_ref[...], vc_ref[0], first=True)

            def exact_body(t, carry):
                exact_tile(k_tile(t), v_tile(t))
                return carry
            lax.fori_loop(0, n, exact_body, 0)


    acc = acc_sc[...]
    out_t = acc / jnp.sum(l_sc[...], axis=0, keepdims=True)
    pad = jnp.zeros((HEAD_PAD - V_ROWS, out_t.shape[1]), F32)
    o_ref[...] = jnp.concatenate([out_t, pad], axis=0).T.astype(BF16)


def _attention(q, kc, vc, kx, vx, tq, tk):
    bsz, sq, _ = q.shape
    n_ctx = kc.shape[1]
    qspec = pl.BlockSpec((None, tq, HEAD_PAD), lambda b, h, i: (b, i, h))
    kspec = lambda n: pl.BlockSpec((None, n, HEAD_PAD), lambda b, h, i: (b, 0, h))
    vspec = lambda v: pl.BlockSpec((None, None) + v.shape[2:], lambda b, h, i: (b, h, 0, 0, 0))
    assert vc.shape[2:] == (1, V_ROWS, n_ctx)
    args = [q, kc, vc]
    in_specs = [qspec, kspec(n_ctx), vspec(vc)]
    n_x_blocks = 0

    scratch = [pltpu.VMEM((1, tq), F32), pltpu.VMEM((V_ROWS, tq), F32), pltpu.VMEM((8, tq), F32)]
    if kx is not None:
        s = kx.shape[1]
        n_x_blocks = s // tk
        assert vx.shape[3] == V_ROWS and tk % vx.shape[4] == 0 and vx.shape[2] * vx.shape[4] == s
        args += [kx, vx]
        in_specs += [kspec(s), vspec(vx)]
        scratch += [pltpu.VMEM((1, tq), F32)]
    return pl.pallas_call(
        functools.partial(_attn_kernel, tk=tk, n_x_blocks=n_x_blocks),
        out_shape=jax.ShapeDtypeStruct((bsz, sq, HP), BF16),
        grid=(bsz, MLA_HEADS, sq // tq),
        in_specs=in_specs,
        out_specs=qspec,
        scratch_shapes=scratch,
        compiler_params=_cparams(3),
        name="attn_x" if kx is not None else "attn_ctx",
    )(*args)


def _fft1_kernel(z_ref, m_ref, o_ref, *, g, n1):
    for j in range(g):
        zz = jnp.concatenate([z_ref[0, :, j, :], z_ref[1, :, j, :]], axis=0).astype(BF16)
        r = _dot(m_ref[j], zz)
        o_ref[0, j] = r[:n1]
        o_ref[1, j] = r[n1:]


def _fft2_kernel(w_ref, x_ref, o_ref, *, g, scale):
    n2 = x_ref.shape[1]
    for j in range(g):
        xx = jnp.concatenate([x_ref[0, :, j, :], x_ref[1, :, j, :]], axis=0).astype(BF16)
        o_ref[:, j, :] = _dot(w_ref[...], xx) * scale


def _dft_direct_kernel(w_ref, x_ref, o_ref, *, scale):
    o_ref[...] = _dot(w_ref[...], x_ref[...].astype(BF16)) * scale


def _dft_cos_sin(rows, cols, period):
    r = (rows[:, None] * cols[None, :]) % period
    ang = r.astype(F32) * (2.0 * math.pi / period)
    return jnp.cos(ang), jnp.sin(ang)


def _fourier_seq(z, consts):
    bsz, _, s, c = z.shape
    scale = 1.0 / math.sqrt(s * F_GROUP_DIM)
    if s <= 2 * FFT_N2:
        w = consts["w_direct"]
        return pl.pallas_call(
            functools.partial(_dft_direct_kernel, scale=scale),
            out_shape=jax.ShapeDtypeStruct((bsz, s, c), F32),
            grid=(bsz,),
            in_specs=[_const_spec(w.shape), pl.BlockSpec((None, 2 * s, c), lambda b: (b, 0, 0))],
            out_specs=pl.BlockSpec((None, s, c), lambda b: (b, 0, 0)),
            compiler_params=_cparams(1),
            name="dft_direct",
        )(w, z.reshape(bsz, 2 * s, c))
    n1, n2 = s // FFT_N2, FFT_N2
    g = 8
    x2 = pl.pallas_call(
        functools.partial(_fft1_kernel, g=g, n1=n1),
        out_shape=jax.ShapeDtypeStruct((bsz, 2, n2, n1, c), F32),
        grid=(bsz, n2 // g),
        in_specs=[
            pl.BlockSpec((None, 2, n1, g, c), lambda b, j: (b, 0, 0, j, 0)),
            pl.BlockSpec((g, 2 * n1, 2 * n1), lambda b, j: (j, 0, 0)),
        ],
        out_specs=pl.BlockSpec((None, 2, g, n1, c), lambda b, j: (b, 0, j, 0, 0)),
        compiler_params=_cparams(2),
        name="fft_stage1",
    )(z.reshape(bsz, 2, n1, n2, c), consts["m1"])
    w2 = consts["w2"]
    y = pl.pallas_call(
        functools.partial(_fft2_kernel, g=g, scale=scale),
        out_shape=jax.ShapeDtypeStruct((bsz, n2, n1, c), F32),
        grid=(bsz, n1 // g),
        in_specs=[
            _const_spec(w2.shape),
            pl.BlockSpec((None, 2, n2, g, c), lambda b, j: (b, 0, 0, j, 0)),
        ],
        out_specs=pl.BlockSpec((None, n2, g, c), lambda b, j: (b, 0, j, 0)),
        compiler_params=_cparams(2),
        name="fft_stage2",
    )(w2, x2)
    return y.reshape(bsz, s, c)


def _conv_kernel(uc_ref, up_ref, un_ref, wdw_ref, bdw_ref, lng_ref, lnb_ref, wpw_ref, o_ref, ext_sc, sh_sc,
                 *, tm, rc):
    i = pl.program_id(1)
    last = pl.num_programs(1) - 1
    ext_sc[0:CONV_HALO] = jnp.where(i > 0, up_ref[...].astype(F32), 0.0)
    ext_sc[CONV_HALO:CONV_HALO + tm] = uc_ref[...].astype(F32)
    ext_sc[CONV_HALO + tm:2 * CONV_HALO + tm] = jnp.where(i < last, un_ref[...].astype(F32), 0.0)
    first = CONV_HALO - CONV_KERNEL // 2
    sub = sh_sc.shape[0]
    for r in range(sub):
        sh_sc[r] = ext_sc[r:r + sh_sc.shape[1], :]
    for r0 in range(0, tm, rc):
        acc = jnp.zeros((rc, CONV_WIDTH), F32)
        for k in range(CONV_KERNEL):
            r = (first + k) % sub
            lo = r0 + (first + k) - r
            acc = acc + sh_sc[r, lo:lo + rc, :] * wdw_ref[k:k + 1, :]
        acc = acc + bdw_ref[...]
        mu = jnp.mean(acc, axis=-1, keepdims=True)
        xc = acc - mu
        var = jnp.mean(xc * xc, axis=-1, keepdims=True)
        y = xc * lax.rsqrt(var + EPS) * lng_ref[...] + lnb_ref[...]
        y = y * jax.nn.sigmoid(y)
        o_ref[r0:r0 + rc, :] = _dot(y.astype(BF16), wpw_ref[...]).astype(BF16)


def _conv_module(u, w, tm):
    bsz, s, c = u.shape
    hb = tm // CONV_HALO
    n_halo = s // CONV_HALO
    return pl.pallas_call(
        functools.partial(_conv_kernel, tm=tm, rc=64),
        out_shape=jax.ShapeDtypeStruct((bsz, s, c), BF16),
        grid=(bsz, s // tm),
        in_specs=[
            pl.BlockSpec((None, tm, c), lambda b, i: (b, i, 0)),
            pl.BlockSpec((None, CONV_HALO, c), lambda b, i: (b, jnp.maximum(i * hb - 1, 0), 0)),
            pl.BlockSpec((None, CONV_HALO, c), lambda b, i: (b, jnp.minimum((i + 1) * hb, n_halo - 1), 0)),
            _const_spec((32, c)), _const_spec((1, c)), _const_spec((1, c)), _const_spec((1, c)),
            _const_spec((c, c)),
        ],
        out_specs=pl.BlockSpec((None, tm, c), lambda b, i: (b, i, 0)),
        scratch_shapes=[pltpu.VMEM((tm + 2 * CONV_HALO, c), F32),
                        pltpu.VMEM((8, tm + 2 * CONV_HALO - 8, c), F32)],
        compiler_params=_cparams(2),
        name="conv",
    )(u, u, u, w["w_dw"], w["b_dw"], w["ln_g"], w["ln_b"], w["w_pw2"])


def _mix_mlp_kernel(x_ref, gate1_ref, shift_ref, scale_ref, gate2_ref, yf_ref, at_ref, yc_ref,
                    wf_ref, wof_ref, woa_ref, woc_ref, g2_ref, w1_ref, w2_ref, gf_ref, o_ref,
                    *, ff_chunk, final_norm):
    yf = _dot(yf_ref[...].astype(BF16), wf_ref[...]).astype(BF16)
    mix = _dot(yf, wof_ref[...]) + _dot(at_ref[...], woa_ref[...]) + _dot(yc_ref[...], woc_ref[...])
    x = x_ref[...] + gate1_ref[...] * mix
    h = (_rms(x, g2_ref[...]) * (1.0 + scale_ref[...]) + shift_ref[...]).astype(BF16)
    d_ff = w1_ref.shape[1]
    acc = jnp.zeros(x.shape, F32)
    for c0 in range(0, d_ff, ff_chunk):
        a = jnp.maximum(_dot(h, w1_ref[:, c0:c0 + ff_chunk]), 0.0)
        acc = acc + _dot((a * a).astype(BF16), w2_ref[c0:c0 + ff_chunk, :])
    y = x + gate2_ref[...] * acc
    if final_norm:
        y = _rms(y, gf_ref[...])
    o_ref[...] = y


def _mix_mlp(x, mods, mod_row, yf, attn, yc, w, gf, tm, final_norm):
    bsz, s, d = x.shape
    d_ff = w["w_mlp1"].shape[1]
    row = lambda j: pl.BlockSpec((None, None, 1, d), lambda b, i: (mod_row(b), j, 0, 0))
    tok = lambda width: pl.BlockSpec((None, tm, width), lambda b, i: (b, i, 0))
    return pl.pallas_call(
        functools.partial(_mix_mlp_kernel, ff_chunk=1024, final_norm=final_norm),
        out_shape=jax.ShapeDtypeStruct((bsz, s, d), F32),
        grid=(bsz, s // tm),
        in_specs=[
            tok(d), row(2), row(3), row(4), row(5),
            tok(F_WIDTH), tok(HP), tok(CONV_WIDTH),
            _const_spec((F_WIDTH, F_WIDTH)), _const_spec((F_WIDTH, d)), _const_spec((HP, d)),
            _const_spec((CONV_WIDTH, d)),
            _const_spec((1, d)), _const_spec((d, d_ff)), _const_spec((d_ff, d)), _const_spec((1, d)),
        ],
        out_specs=tok(d),
        compiler_params=_cparams(2),
        name="mix_mlp",
    )(x, mods, mods, mods, mods, yf, attn, yc, w["w_f"], w["wo_f"], w["wo_a"], w["wo_c"],
      w["g2"], w["w_mlp1"], w["w_mlp2"], gf)


def _rope_tables(s):
    t = jnp.arange(s)
    row = (t // GRID_W).astype(F32)
    col = (t % GRID_W).astype(F32)
    inv = ROPE_BASE ** (-jnp.arange(0, ROPE_AXIS_DIM, 2, dtype=F32) / ROPE_AXIS_DIM)
    ang = jnp.concatenate([row[:, None] * inv, col[:, None] * inv], axis=-1)
    cos, sin = jnp.cos(ang), jnp.sin(ang)
    z_lo = jnp.zeros((s, QK_NOPE_DIM), F32)
    z_hi = jnp.zeros((s, HEAD_PAD - QK_NOPE_DIM - QK_ROPE_DIM), F32)
    ta = jnp.concatenate([z_lo, cos, cos, z_hi], axis=-1)
    tb = jnp.concatenate([z_lo, -sin, sin, z_hi], axis=-1)
    return ta, tb


def _identity_rope_tables(s):
    lane = np.arange(HEAD_PAD)
    ta = ((lane >= QK_NOPE_DIM) & (lane < QK_NOPE_DIM + QK_ROPE_DIM)).astype(np.float32)
    return jnp.asarray(np.tile(ta, (s, 1))), jnp.zeros((s, HEAD_PAD), F32)


def _fourier_consts(s, n_ctx):
    ar = lambda n: jnp.arange(n, dtype=jnp.int32)
    cc, sc = _dft_cos_sin(ar(F_GROUP_DIM), ar(F_GROUP_DIM), F_GROUP_DIM)
    eye = jnp.eye(F_GROUPS, dtype=F32)
    dft_c = jnp.concatenate([jnp.kron(eye, cc), -jnp.kron(eye, sc)], axis=1).astype(BF16)
    consts = {"dft_c": dft_c}

    def direct(n):
        c, sn = _dft_cos_sin(ar(n), ar(n), n)
        return jnp.concatenate([c, sn], axis=1).astype(BF16)

    def two_stage(n):
        n1, n2 = n // FFT_N2, FFT_N2
        pos = ar(n1)[None, :] * n2 + ar(n2)[:, None]
        r = (ar(n1)[None, :, None] * pos[:, None, :]) % n
        ang = r.astype(F32) * (2.0 * math.pi / n)
        c, sn = jnp.cos(ang), jnp.sin(ang)
        m1 = jnp.concatenate([jnp.concatenate([c, sn], axis=2),
                              jnp.concatenate([-sn, c], axis=2)], axis=1).astype(BF16)
        return m1, direct(n2)

    for n, tag in ((s, "x"), (n_ctx, "ctx")):
        if n <= 2 * FFT_N2:
            consts[tag] = {"w_direct": direct(n)}
        else:
            m1, w2 = two_stage(n)
            consts[tag] = {"m1": m1, "w2": w2}
    return consts


def _pad_heads(w, width):
    k = w.shape[0]
    w = w.reshape(k, MLA_HEADS, width)
    return jnp.pad(w, ((0, 0), (0, 0), (0, HEAD_PAD - width))).reshape(k, HP)


def _layer_weights(l, p, dft_c):
    d = p["w_in"].shape[1]
    w_in = p["w_in"][l]
    s0, s1, s2 = F_WIDTH, F_WIDTH + Q_LORA_RANK, F_WIDTH + Q_LORA_RANK + KV_LORA_RANK
    s3 = s2 + QK_ROPE_DIM
    half = QK_ROPE_DIM // 2
    w_kr = w_in[:, s2:s3]
    w_kr_sw = jnp.concatenate([w_kr[:, half:], w_kr[:, :half]], axis=1)
    lo = jnp.zeros((d, QK_NOPE_DIM), F32)
    hi = jnp.zeros((d, HEAD_PAD - QK_NOPE_DIM - QK_ROPE_DIM), F32)
    w1 = jnp.concatenate([w_in[:, :s2], lo, w_kr, hi, lo, w_kr_sw, hi, w_in[:, s3:]], axis=1).astype(BF16)

    qd = QK_NOPE_DIM + QK_ROPE_DIM
    w_uq = p["w_uq"][l].reshape(Q_LORA_RANK, MLA_HEADS, qd)
    rope = w_uq[:, :, QK_NOPE_DIM:]
    rope_sw = jnp.concatenate([rope[:, :, half:], rope[:, :, :half]], axis=2)
    wqa = _pad_heads(w_uq.reshape(Q_LORA_RANK, MLA_HEADS * qd), qd).astype(BF16)
    wqb = jnp.pad(rope_sw, ((0, 0), (0, 0), (QK_NOPE_DIM, HEAD_PAD - qd))).reshape(Q_LORA_RANK, HP).astype(BF16)

    w_ukv = p["w_ukv"][l].reshape(KV_LORA_RANK, MLA_HEADS, QK_NOPE_DIM + V_HEAD_DIM)
    wk = _pad_heads(w_ukv[:, :, :QK_NOPE_DIM].reshape(KV_LORA_RANK, -1), QK_NOPE_DIM).astype(BF16)
    wv = jnp.pad(w_ukv[:, :, QK_NOPE_DIM:], ((0, 0), (0, 0), (0, V_ROWS - V_HEAD_DIM)))
    wv = wv.reshape(KV_LORA_RANK, MLA_HEADS * V_ROWS).T.astype(BF16)

    w_o = p["w_o"][l]
    a0, a1 = F_WIDTH, F_WIDTH + MLA_HEADS * V_HEAD_DIM
    wo_a = jnp.pad(w_o[a0:a1].reshape(MLA_HEADS, V_HEAD_DIM, d),
                   ((0, 0), (0, HEAD_PAD - V_HEAD_DIM), (0, 0))).reshape(HP, d).astype(BF16)
    w_dw = jnp.pad(p["w_dw"][l], ((0, 32 - CONV_KERNEL), (0, 0)))
    r2 = lambda v: v.reshape(1, -1)
    return {
        "w1": w1, "qg": r2(p["q_norm_g"][l]), "wqa": wqa, "wqb": wqb,
        "kvg": r2(p["kv_norm_g"][l]), "wk": wk, "wv": wv, "dft_c": dft_c,
        "w_dw": w_dw, "b_dw": r2(p["b_dw"][l]), "ln_g": r2(p["conv_ln_g"][l]), "ln_b": r2(p["conv_ln_b"][l]),
        "w_pw2": p["w_pw2"][l].astype(BF16),
        "w_f": p["w_fourier"][l].astype(BF16), "wo_f": w_o[:a0].astype(BF16), "wo_a": wo_a,
        "wo_c": w_o[a1:].astype(BF16),
        "g1": r2(p["norm1_g"][l]), "g2": r2(p["norm2_g"][l]),
        "w_mlp1": p["w_mlp1"][l].astype(BF16), "w_mlp2": p["w_mlp2"][l].astype(BF16),
    }


def _token_tile(s):
    for tm in (512, 256, 128):
        if s % tm == 0:
            return tm
    raise ValueError(f"sequence length {s} must be a multiple of 128")


def kernel(x, c, ctx, c_ctx, w_mod, b_mod, norm1_g, w_in, q_norm_g, w_uq, kv_norm_g, w_ukv, w_fourier,
           w_dw, b_dw, conv_ln_g, conv_ln_b, w_pw2, w_o, norm2_g, w_mlp1, w_mlp2, final_norm_g):
    bsz, s, d = x.shape
    n_ctx = ctx.shape[1]
    depth = w_mod.shape[0]
    assert bsz < 8 and s % GRID_W == 0 and s % LANES == 0 and n_ctx % LANES == 0
    p = dict(w_in=w_in, w_uq=w_uq, w_ukv=w_ukv, q_norm_g=q_norm_g, kv_norm_g=kv_norm_g,
             w_fourier=w_fourier, w_dw=w_dw, b_dw=b_dw, conv_ln_g=conv_ln_g, conv_ln_b=conv_ln_b,
             w_pw2=w_pw2, w_o=w_o, norm1_g=norm1_g, norm2_g=norm2_g, w_mlp1=w_mlp1, w_mlp2=w_mlp2)

    c_rows = jnp.zeros((8, d), F32).at[:bsz].set(c).at[bsz].set(c_ctx)
    mods_all = _modulation(c_rows, w_mod, b_mod).reshape(depth, 8, N_MOD, 1, d)
    x_row = lambda b: b
    ctx_row = lambda b: bsz

    tm_x, tm_c = _token_tile(s), _token_tile(n_ctx)
    tq = next(t for t in (ATTN_QUERY_TILE, tm_x) if s % t == 0)
    tk = next(t for t in (ATTN_KEY_TILE, tm_x) if s % t == 0 and t % tm_x == 0)
    ta_x, tb_x = _rope_tables(s)
    ta_c, tb_c = _identity_rope_tables(n_ctx)
    fc = _fourier_consts(s, n_ctx)
    gf = final_norm_g.reshape(1, d)

    for l in range(depth):
        last = l == depth - 1
        w = _layer_weights(l, p, fc["dft_c"])
        mods = mods_all[l]

        qx, kx, vx, zx, ux = _prep(x, mods, x_row, w["g1"], w, ta_x, tb_x, tm_x)
        qc, kc, vc, zc, uc = _prep(ctx, mods, ctx_row, w["g1"], w, ta_c, tb_c, tm_c)

        attn_x = _attention(qx, kc, vc, kx, vx, tq, tk)
        yf_x = _fourier_seq(zx, fc["x"])
        yc_x = _conv_module(ux, w, tm_x)
        x = _mix_mlp(x, mods, x_row, yf_x, attn_x, yc_x, w, gf, tm_x, final_norm=last)

        if not last:
            attn_c = _attention(qc, kc, vc, None, None, tm_c, tk)
            yf_c = _fourier_seq(zc, fc["ctx"])
            yc_c = _conv_module(uc, w, tm_c)
            ctx = _mix_mlp(ctx, mods, ctx_row, yf_c, attn_c, yc_c, w, gf, tm_c, final_norm=False)
    return x
```

```python
import functools
import math

import numpy as np
import jax
import jax.numpy as jnp
from jax import lax
from jax.experimental import pallas as pl
from jax.experimental.pallas import tpu as pltpu

F32 = jnp.float32
BF16 = jnp.bfloat16

GRID_W = 64
F_GROUPS = 4
F_GROUP_DIM = 64
F_WIDTH = F_GROUPS * F_GROUP_DIM
MLA_HEADS = 8
QK_NOPE_DIM = 64
QK_ROPE_DIM = 32
V_HEAD_DIM = 64
Q_LORA_RANK = 256
KV_LORA_RANK = 128
CONV_WIDTH = 256
CONV_KERNEL = 31
N_MOD = 6
ROPE_BASE = 10000.0
ROPE_AXIS_DIM = QK_ROPE_DIM // 2
ATTN_SCALE = (QK_NOPE_DIM + QK_ROPE_DIM) ** -0.5
EPS = 1e-6

LANES = 128
HEAD_PAD = LANES
HP = MLA_HEADS * HEAD_PAD
ONES_ROW = V_HEAD_DIM
HEADS_PER_STEP = 2
V_ROWS = 80
CONV_HALO = 16
FFT_N2 = 128
VMEM_LIMIT = 56 * 1024 * 1024
TILES_PER_PASS = 10
LAZY_HEADROOM = 60.0
ATTN_KEY_TILE = 512
ATTN_QUERY_TILE = 1024
Q_SCALE = ATTN_SCALE * math.log2(math.e)


def _cparams(n_axes, flags=None):
    return pltpu.CompilerParams(dimension_semantics=("parallel",) * n_axes,
                                vmem_limit_bytes=VMEM_LIMIT, flags=flags)


def _const_spec(shape):
    nd = len(shape)
    return pl.BlockSpec(shape, lambda *_: (0,) * nd, pipeline_mode=pl.Buffered(1))


def _dot(a, b):
    return jnp.dot(a, b, preferred_element_type=F32)


def _split_bf16(v):
    hi = v.astype(BF16)
    lo = (v - hi.astype(F32)).astype(BF16)
    return hi, lo


def _mod_kernel(c_ref, w_ref, b_ref, o_ref):
    c = c_ref[...]
    s = c * jax.nn.sigmoid(c)
    s_hi, s_lo = _split_bf16(s)
    w_hi, w_lo = _split_bf16(w_ref[...])
    o_ref[...] = _dot(s_hi, w_hi) + _dot(s_hi, w_lo) + _dot(s_lo, w_hi) + b_ref[...]


def _modulation(c_rows, w_mod, b_mod):
    n_layers, d, n6 = w_mod.shape
    tn = 1536
    return pl.pallas_call(
        _mod_kernel,
        out_shape=jax.ShapeDtypeStruct((n_layers, 8, n6), F32),
        grid=(n_layers, n6 // tn),
        in_specs=[
            pl.BlockSpec((8, d), lambda l, j: (0, 0)),
            pl.BlockSpec((None, d, tn), lambda l, j: (l, 0, j)),
            pl.BlockSpec((None, 1, tn), lambda l, j: (l, 0, j)),
        ],
        out_specs=pl.BlockSpec((None, 8, tn), lambda l, j: (l, 0, j)),
        compiler_params=_cparams(2),
        name="modulation",
    )(c_rows, w_mod, b_mod.reshape(n_layers, 1, n6))


def _rms(v, g):
    return v * lax.rsqrt(jnp.mean(v * v, axis=-1, keepdims=True) + EPS) * g


def _prep_kernel(x_ref, shift_ref, scale_ref, g1_ref, w1_ref, qg_ref, wqa_ref, wqb_ref,
                 kvg_ref, wk_ref, wv_ref, dft_ref, ta_ref, tb_ref, vone_ref,
                 q_ref, k_ref, v_ref, z_ref, u_ref):
    x = x_ref[...]
    h = (_rms(x, g1_ref[...]) * (1.0 + scale_ref[...]) + shift_ref[...]).astype(BF16)
    z = _dot(h, w1_ref[...])

    zz = _dot(z[:, 0:F_WIDTH].astype(BF16), dft_ref[...])
    z_ref[0] = zz[:, :F_WIDTH]
    z_ref[1] = zz[:, F_WIDTH:]

    ta = ta_ref[...]
    tb = tb_ref[...]
    lane = lax.broadcasted_iota(jnp.int32, (1, HEAD_PAD), 1)
    nope = (lane < QK_NOPE_DIM).astype(F32)

    c0 = F_WIDTH
    cqn = _rms(z[:, c0:c0 + Q_LORA_RANK], qg_ref[...]).astype(BF16)
    qa = _dot(cqn, wqa_ref[...])
    qb = _dot(cqn, wqb_ref[...])
    qta = (ta + nope) * Q_SCALE
    qtb = tb * Q_SCALE
    for hd in range(MLA_HEADS):
        sl = slice(hd * HEAD_PAD, (hd + 1) * HEAD_PAD)
        q_ref[:, sl] = (qa[:, sl] * qta + qb[:, sl] * qtb).astype(BF16)

    c1 = c0 + Q_LORA_RANK
    ckvn = _rms(z[:, c1:c1 + KV_LORA_RANK], kvg_ref[...]).astype(BF16)
    kn = _dot(ckvn, wk_ref[...])
    c2 = c1 + KV_LORA_RANK
    kr = z[:, c2:c2 + HEAD_PAD] * ta + z[:, c2 + HEAD_PAD:c2 + 2 * HEAD_PAD] * tb
    for hd in range(MLA_HEADS):
        sl = slice(hd * HEAD_PAD, (hd + 1) * HEAD_PAD)
        k_ref[:, sl] = (kn[:, sl] + kr).astype(BF16)
    vt = lax.dot_general(wv_ref[...], ckvn, (((1,), (1,)), ((), ())), preferred_element_type=F32)
    vt = vt + vone_ref[...]
    for hd in range(MLA_HEADS):
        v_ref[hd] = vt[hd * V_ROWS:(hd + 1) * V_ROWS, :].astype(BF16)

    c3 = c2 + 2 * HEAD_PAD
    a = z[:, c3:c3 + CONV_WIDTH]
    g = z[:, c3 + CONV_WIDTH:c3 + 2 * CONV_WIDTH]
    u_ref[...] = (a * jax.nn.sigmoid(g)).astype(BF16)


def _prep(x, mods, mod_row, g1, w, ta, tb, tm):
    bsz, s, d = x.shape
    n1 = w["w1"].shape[1]
    row = lambda j: pl.BlockSpec((None, None, 1, d), lambda b, i: (mod_row(b), j, 0, 0))
    tok = lambda width: pl.BlockSpec((None, tm, width), lambda b, i: (b, i, 0))
    return pl.pallas_call(
        _prep_kernel,
        out_shape=(
            jax.ShapeDtypeStruct((bsz, s, HP), BF16),
            jax.ShapeDtypeStruct((bsz, s, HP), BF16),
            jax.ShapeDtypeStruct((bsz, MLA_HEADS, s // tm, V_ROWS, tm), BF16),
            jax.ShapeDtypeStruct((bsz, 2, s, F_WIDTH), F32),
            jax.ShapeDtypeStruct((bsz, s, CONV_WIDTH), BF16),
        ),
        grid=(bsz, s // tm),
        in_specs=[
            tok(d), row(0), row(1), _const_spec((1, d)),
            _const_spec((d, n1)),
            _const_spec((1, Q_LORA_RANK)), _const_spec((Q_LORA_RANK, HP)), _const_spec((Q_LORA_RANK, HP)),
            _const_spec((1, KV_LORA_RANK)), _const_spec((KV_LORA_RANK, HP)),
            _const_spec((MLA_HEADS * V_ROWS, KV_LORA_RANK)),
            _const_spec((F_WIDTH, 2 * F_WIDTH)),
            pl.BlockSpec((tm, HEAD_PAD), lambda b, i: (i, 0)),
            pl.BlockSpec((tm, HEAD_PAD), lambda b, i: (i, 0)),
            _const_spec((MLA_HEADS * V_ROWS, 1)),
        ],
        out_specs=(
            tok(HP), tok(HP),
            pl.BlockSpec((None, MLA_HEADS, None, V_ROWS, tm), lambda b, i: (b, 0, i, 0, 0)),
            pl.BlockSpec((None, 2, tm, F_WIDTH), lambda b, i: (b, 0, i, 0)),
            tok(CONV_WIDTH),
        ),
        compiler_params=_cparams(2),
        name="prep",
    )(x, mods, mods, g1, w["w1"], w["qg"], w["wqa"], w["wqb"], w["kvg"], w["wk"], w["wv"],
      w["dft_c"], ta, tb, w["vone"])


def _attn_kernel(*refs, tk, n_x_blocks):
    if n_x_blocks:
        q_ref, kc_ref, vc_ref, kx_ref, vx_ref, o_ref, m_sc, acc_sc, out_sc, ex_sc = refs
    else:
        q_ref, kc_ref, vc_ref, o_ref, m_sc, acc_sc, out_sc = refs
    for hh in range(HEADS_PER_STEP):
        lanes = slice(hh * HEAD_PAD, (hh + 1) * HEAD_PAD)
        qt = q_ref[:, lanes].astype(F32).T.astype(BF16)

        def exact_tile(k, vt, first=False):
            s = _dot(k, qt)
            m_tile = jnp.max(s, axis=0, keepdims=True)
            m_new = m_tile if first else jnp.maximum(m_sc[...], m_tile)
            p = jnp.exp2(s - m_new).astype(BF16)
            pv = _dot(vt, p)
            acc_sc[...] = pv if first else jnp.exp2(m_sc[...] - m_new) * acc_sc[...] + pv
            m_sc[...] = m_new

        exact_tile(kc_ref[:, lanes], vc_ref[hh, 0], first=True)
        if n_x_blocks:
            n = n_x_blocks
            v_sub = tk // vx_ref.shape[3]

            def k_tile(t):
                return kx_ref[pl.ds(pl.multiple_of(t * tk, tk), tk), lanes]

            def v_tile(t):
                return jnp.concatenate([vx_ref[hh, v_sub * t + i] for i in range(v_sub)], axis=1)

            def lazy_tile(t):
                s = _dot(k_tile(t), qt)
                m_old = m_sc[...]
                p = jnp.exp2(s - m_old).astype(BF16)
                m_tile = jnp.max(s, axis=0, keepdims=True)
                m_new = jnp.maximum(m_old, m_tile)
                acc_sc[...] = (acc_sc[...] + _dot(v_tile(t), p)) * jnp.exp2(m_old - m_new)
                m_sc[...] = m_new
                ex_sc[...] = jnp.maximum(ex_sc[...], m_tile - m_old)

            ex_sc[...] = jnp.zeros(ex_sc.shape, F32)
            n_peel = n % TILES_PER_PASS
            for t in range(n_peel):
                lazy_tile(t)

            def body(j, carry):
                t0 = n_peel + TILES_PER_PASS * j
                for u in range(TILES_PER_PASS):
                    lazy_tile(t0 + u)
                return carry
            lax.fori_loop(0, (n - n_peel) // TILES_PER_PASS, body, 0)

            @pl.when(jnp.max(ex_sc[...]) > LAZY_HEADROOM)
            def _():
                exact_tile(kc_ref[:, lanes], vc_ref[hh, 0], first=True)

                def exact_body(t, carry):
                    exact_tile(k_tile(t), v_tile(t))
                    return carry
                lax.fori_loop(0, n, exact_body, 0)

        acc = acc_sc[...]
        out_sc[hh * V_HEAD_DIM:(hh + 1) * V_HEAD_DIM, :] = acc[:V_HEAD_DIM] / acc[ONES_ROW:ONES_ROW + 1, :]
    o_ref[...] = out_sc[...].T.astype(BF16)


def _attention(q, kc, vc, kx, vx, tq, tk):
    bsz, sq, _ = q.shape
    n_ctx = kc.shape[1]
    hw = HEADS_PER_STEP * HEAD_PAD
    qspec = pl.BlockSpec((None, tq, hw), lambda b, h, i: (b, i, h))
    kspec = lambda n: pl.BlockSpec((None, n, hw), lambda b, h, i: (b, 0, h))
    vspec = lambda v: pl.BlockSpec((None, HEADS_PER_STEP) + v.shape[2:], lambda b, h, i: (b, h, 0, 0, 0))
    assert vc.shape[2:] == (1, V_ROWS, n_ctx)
    args = [q, kc, vc]
    in_specs = [qspec, kspec(n_ctx), vspec(vc)]
    n_x_blocks = 0
    out_w = HEADS_PER_STEP * V_HEAD_DIM
    scratch = [pltpu.VMEM((1, tq), F32), pltpu.VMEM((V_ROWS, tq), F32), pltpu.VMEM((out_w, tq), F32)]
    if kx is not None:
        s = kx.shape[1]
        n_x_blocks = s // tk
        assert vx.shape[3] == V_ROWS and tk % vx.shape[4] == 0 and vx.shape[2] * vx.shape[4] == s
        args += [kx, vx]
        in_specs += [kspec(s), vspec(vx)]
        scratch += [pltpu.VMEM((1, tq), F32)]
    return pl.pallas_call(
        functools.partial(_attn_kernel, tk=tk, n_x_blocks=n_x_blocks),
        out_shape=jax.ShapeDtypeStruct((bsz, sq, MLA_HEADS * V_HEAD_DIM), BF16),
        grid=(bsz, MLA_HEADS // HEADS_PER_STEP, sq // tq),
        in_specs=in_specs,
        out_specs=pl.BlockSpec((None, tq, out_w), lambda b, h, i: (b, i, h)),
        scratch_shapes=scratch,
        compiler_params=_cparams(3),
        name="attn_x" if kx is not None else "attn_ctx",
    )(*args)


def _fft1_kernel(z_ref, m_ref, o_ref, *, g, n1):
    for j in range(g):
        zz = jnp.concatenate([z_ref[0, :, j, :], z_ref[1, :, j, :]], axis=0).astype(BF16)
        r = _dot(m_ref[j], zz)
        o_ref[0, j] = r[:n1]
        o_ref[1, j] = r[n1:]


def _fft2_kernel(w_ref, x_ref, o_ref, *, g, scale):
    for j in range(g):
        xx = jnp.concatenate([x_ref[0, :, j, :], x_ref[1, :, j, :]], axis=0).astype(BF16)
        o_ref[:, j, :] = _dot(w_ref[...], xx) * scale


def _dft_direct_kernel(w_ref, x_ref, o_ref, *, scale):
    o_ref[...] = _dot(w_ref[...], x_ref[...].astype(BF16)) * scale


def _dft_cos_sin(rows, cols, period):
    r = (rows[:, None] * cols[None, :]) % period
    ang = r.astype(F32) * (2.0 * math.pi / period)
    return jnp.cos(ang), jnp.sin(ang)


def _fourier_seq(z, consts):
    bsz, _, s, c = z.shape
    scale = 1.0 / math.sqrt(s * F_GROUP_DIM)
    if s <= 2 * FFT_N2:
        w = consts["w_direct"]
        return pl.pallas_call(
            functools.partial(_dft_direct_kernel, scale=scale),
            out_shape=jax.ShapeDtypeStruct((bsz, s, c), F32),
            grid=(bsz,),
            in_specs=[_const_spec(w.shape), pl.BlockSpec((None, 2 * s, c), lambda b: (b, 0, 0))],
            out_specs=pl.BlockSpec((None, s, c), lambda b: (b, 0, 0)),
            compiler_params=_cparams(1),
            name="dft_direct",
        )(w, z.reshape(bsz, 2 * s, c))
    n1, n2 = s // FFT_N2, FFT_N2
    g = 8
    x2 = pl.pallas_call(
        functools.partial(_fft1_kernel, g=g, n1=n1),
        out_shape=jax.ShapeDtypeStruct((bsz, 2, n2, n1, c), F32),
        grid=(bsz, n2 // g),
        in_specs=[
            pl.BlockSpec((None, 2, n1, g, c), lambda b, j: (b, 0, 0, j, 0)),
            pl.BlockSpec((g, 2 * n1, 2 * n1), lambda b, j: (j, 0, 0)),
        ],
        out_specs=pl.BlockSpec((None, 2, g, n1, c), lambda b, j: (b, 0, j, 0, 0)),
        compiler_params=_cparams(2),
        name="fft_stage1",
    )(z.reshape(bsz, 2, n1, n2, c), consts["m1"])
    w2 = consts["w2"]
    y = pl.pallas_call(
        functools.partial(_fft2_kernel, g=g, scale=scale),
        out_shape=jax.ShapeDtypeStruct((bsz, n2, n1, c), F32),
        grid=(bsz, n1 // g),
        in_specs=[
            _const_spec(w2.shape),
            pl.BlockSpec((None, 2, n2, g, c), lambda b, j: (b, 0, 0, j, 0)),
        ],
        out_specs=pl.BlockSpec((None, n2, g, c), lambda b, j: (b, 0, j, 0)),
        compiler_params=_cparams(2),
        name="fft_stage2",
    )(w2, x2)
    return y.reshape(bsz, s, c)


def _conv_kernel(uc_ref, up_ref, un_ref, wdw_ref, bdw_ref, lng_ref, lnb_ref, wpw_ref, o_ref, ext_sc, sh_sc,
                 *, tm, rc):
    i = pl.program_id(1)
    last = pl.num_programs(1) - 1
    ext_sc[0:CONV_HALO] = jnp.where(i > 0, up_ref[...].astype(F32), 0.0)
    ext_sc[CONV_HALO:CONV_HALO + tm] = uc_ref[...].astype(F32)
    ext_sc[CONV_HALO + tm:2 * CONV_HALO + tm] = jnp.where(i < last, un_ref[...].astype(F32), 0.0)
    first = CONV_HALO - CONV_KERNEL // 2
    sub = sh_sc.shape[0]
    for r in range(sub):
        sh_sc[r] = ext_sc[r:r + sh_sc.shape[1], :]
    for r0 in range(0, tm, rc):
        acc = jnp.zeros((rc, CONV_WIDTH), F32)
        for k in range(CONV_KERNEL):
            r = (first + k) % sub
            lo = r0 + (first + k) - r
            acc = acc + sh_sc[r, lo:lo + rc, :] * wdw_ref[k:k + 1, :]
        acc = acc + bdw_ref[...]
        mu = jnp.mean(acc, axis=-1, keepdims=True)
        xc = acc - mu
        var = jnp.mean(xc * xc, axis=-1, keepdims=True)
        y = xc * lax.rsqrt(var + EPS) * lng_ref[...] + lnb_ref[...]
        y = y * jax.nn.sigmoid(y)
        o_ref[r0:r0 + rc, :] = _dot(y.astype(BF16), wpw_ref[...]).astype(BF16)


def _conv_module(u, w, tm):
    bsz, s, c = u.shape
    hb = tm // CONV_HALO
    n_halo = s // CONV_HALO
    return pl.pallas_call(
        functools.partial(_conv_kernel, tm=tm, rc=64),
        out_shape=jax.ShapeDtypeStruct((bsz, s, c), BF16),
        grid=(bsz, s // tm),
        in_specs=[
            pl.BlockSpec((None, tm, c), lambda b, i: (b, i, 0)),
            pl.BlockSpec((None, CONV_HALO, c), lambda b, i: (b, jnp.maximum(i * hb - 1, 0), 0)),
            pl.BlockSpec((None, CONV_HALO, c), lambda b, i: (b, jnp.minimum((i + 1) * hb, n_halo - 1), 0)),
            _const_spec((32, c)), _const_spec((1, c)), _const_spec((1, c)), _const_spec((1, c)),
            _const_spec((c, c)),
        ],
        out_specs=pl.BlockSpec((None, tm, c), lambda b, i: (b, i, 0)),
        scratch_shapes=[pltpu.VMEM((tm + 2 * CONV_HALO, c), F32),
                        pltpu.VMEM((8, tm + 2 * CONV_HALO - 8, c), F32)],
        compiler_params=_cparams(2),
        name="conv",
    )(u, u, u, w["w_dw"], w["b_dw"], w["ln_g"], w["ln_b"], w["w_pw2"])


def _mix_mlp_kernel(x_ref, gate1_ref, shift_ref, scale_ref, gate2_ref, yf_ref, at_ref, yc_ref,
                    wf_ref, wof_ref, woa_ref, woc_ref, g2_ref, w1_ref, w2_ref, gf_ref, o_ref,
                    *, ff_chunk, final_norm):
    yf = _dot(yf_ref[...].astype(BF16), wf_ref[...]).astype(BF16)
    mix = _dot(yf, wof_ref[...]) + _dot(at_ref[...], woa_ref[...]) + _dot(yc_ref[...], woc_ref[...])
    x = x_ref[...] + gate1_ref[...] * mix
    h = (_rms(x, g2_ref[...]) * (1.0 + scale_ref[...]) + shift_ref[...]).astype(BF16)
    d_ff = w1_ref.shape[1]
    acc = jnp.zeros(x.shape, F32)
    for c0 in range(0, d_ff, ff_chunk):
        a = jnp.maximum(_dot(h, w1_ref[:, c0:c0 + ff_chunk]), 0.0)
        acc = acc + _dot((a * a).astype(BF16), w2_ref[c0:c0 + ff_chunk, :])
    y = x + gate2_ref[...] * acc
    if final_norm:
        y = _rms(y, gf_ref[...])
    o_ref[...] = y


def _mix_mlp(x, mods, mod_row, yf, attn, yc, w, gf, tm, final_norm):
    bsz, s, d = x.shape
    d_ff = w["w_mlp1"].shape[1]
    row = lambda j: pl.BlockSpec((None, None, 1, d), lambda b, i: (mod_row(b), j, 0, 0))
    tok = lambda width: pl.BlockSpec((None, tm, width), lambda b, i: (b, i, 0))
    return pl.pallas_call(
        functools.partial(_mix_mlp_kernel, ff_chunk=1024, final_norm=final_norm),
        out_shape=jax.ShapeDtypeStruct((bsz, s, d), F32),
        grid=(bsz, s // tm),
        in_specs=[
            tok(d), row(2), row(3), row(4), row(5),
            tok(F_WIDTH), tok(MLA_HEADS * V_HEAD_DIM), tok(CONV_WIDTH),
            _const_spec((F_WIDTH, F_WIDTH)), _const_spec((F_WIDTH, d)), _const_spec((MLA_HEADS * V_HEAD_DIM, d)),
            _const_spec((CONV_WIDTH, d)),
            _const_spec((1, d)), _const_spec((d, d_ff)), _const_spec((d_ff, d)), _const_spec((1, d)),
        ],
        out_specs=tok(d),
        compiler_params=_cparams(2),
        name="mix_mlp",
    )(x, mods, mods, mods, mods, yf, attn, yc, w["w_f"], w["wo_f"], w["wo_a"], w["wo_c"],
      w["g2"], w["w_mlp1"], w["w_mlp2"], gf)


def _rope_tables(s):
    t = jnp.arange(s)
    row = (t // GRID_W).astype(F32)
    col = (t % GRID_W).astype(F32)
    inv = ROPE_BASE ** (-jnp.arange(0, ROPE_AXIS_DIM, 2, dtype=F32) / ROPE_AXIS_DIM)
    ang = jnp.concatenate([row[:, None] * inv, col[:, None] * inv], axis=-1)
    cos, sin = jnp.cos(ang), jnp.sin(ang)
    z_lo = jnp.zeros((s, QK_NOPE_DIM), F32)
    z_hi = jnp.zeros((s, HEAD_PAD - QK_NOPE_DIM - QK_ROPE_DIM), F32)
    ta = jnp.concatenate([z_lo, cos, cos, z_hi], axis=-1)
    tb = jnp.concatenate([z_lo, -sin, sin, z_hi], axis=-1)
    return ta, tb


def _identity_rope_tables(s):
    lane = np.arange(HEAD_PAD)
    ta = ((lane >= QK_NOPE_DIM) & (lane < QK_NOPE_DIM + QK_ROPE_DIM)).astype(np.float32)
    return jnp.asarray(np.tile(ta, (s, 1))), jnp.zeros((s, HEAD_PAD), F32)


def _fourier_consts(s, n_ctx):
    ar = lambda n: jnp.arange(n, dtype=jnp.int32)
    cc, sc = _dft_cos_sin(ar(F_GROUP_DIM), ar(F_GROUP_DIM), F_GROUP_DIM)
    eye = jnp.eye(F_GROUPS, dtype=F32)
    dft_c = jnp.concatenate([jnp.kron(eye, cc), -jnp.kron(eye, sc)], axis=1).astype(BF16)
    consts = {"dft_c": dft_c}

    def direct(n):
        c, sn = _dft_cos_sin(ar(n), ar(n), n)
        return jnp.concatenate([c, sn], axis=1).astype(BF16)

    def two_stage(n):
        n1, n2 = n // FFT_N2, FFT_N2
        pos = ar(n1)[None, :] * n2 + ar(n2)[:, None]
        r = (ar(n1)[None, :, None] * pos[:, None, :]) % n
        ang = r.astype(F32) * (2.0 * math.pi / n)
        c, sn = jnp.cos(ang), jnp.sin(ang)
        m1 = jnp.concatenate([jnp.concatenate([c, sn], axis=2),
                              jnp.concatenate([-sn, c], axis=2)], axis=1).astype(BF16)
        return m1, direct(n2)

    for n, tag in ((s, "x"), (n_ctx, "ctx")):
        if n <= 2 * FFT_N2:
            consts[tag] = {"w_direct": direct(n)}
        else:
            m1, w2 = two_stage(n)
            consts[tag] = {"m1": m1, "w2": w2}
    return consts


def _pad_heads(w, width):
    k = w.shape[0]
    w = w.reshape(k, MLA_HEADS, width)
    return jnp.pad(w, ((0, 0), (0, 0), (0, HEAD_PAD - width))).reshape(k, HP)


def _layer_weights(l, p, dft_c):
    d = p["w_in"].shape[1]
    w_in = p["w_in"][l]
    s2 = F_WIDTH + Q_LORA_RANK + KV_LORA_RANK
    s3 = s2 + QK_ROPE_DIM
    half = QK_ROPE_DIM // 2
    w_kr = w_in[:, s2:s3]
    w_kr_sw = jnp.concatenate([w_kr[:, half:], w_kr[:, :half]], axis=1)
    lo = jnp.zeros((d, QK_NOPE_DIM), F32)
    hi = jnp.zeros((d, HEAD_PAD - QK_NOPE_DIM - QK_ROPE_DIM), F32)
    w1 = jnp.concatenate([w_in[:, :s2], lo, w_kr, hi, lo, w_kr_sw, hi, w_in[:, s3:]], axis=1).astype(BF16)

    qd = QK_NOPE_DIM + QK_ROPE_DIM
    w_uq = p["w_uq"][l].reshape(Q_LORA_RANK, MLA_HEADS, qd)
    rope = w_uq[:, :, QK_NOPE_DIM:]
    rope_sw = jnp.concatenate([rope[:, :, half:], rope[:, :, :half]], axis=2)
    wqa = _pad_heads(w_uq.reshape(Q_LORA_RANK, MLA_HEADS * qd), qd).astype(BF16)
    wqb = jnp.pad(rope_sw, ((0, 0), (0, 0), (QK_NOPE_DIM, HEAD_PAD - qd))).reshape(Q_LORA_RANK, HP).astype(BF16)

    w_ukv = p["w_ukv"][l].reshape(KV_LORA_RANK, MLA_HEADS, QK_NOPE_DIM + V_HEAD_DIM)
    wk = _pad_heads(w_ukv[:, :, :QK_NOPE_DIM].reshape(KV_LORA_RANK, -1), QK_NOPE_DIM).astype(BF16)
    wv = jnp.pad(w_ukv[:, :, QK_NOPE_DIM:], ((0, 0), (0, 0), (0, V_ROWS - V_HEAD_DIM)))
    wv = wv.reshape(KV_LORA_RANK, MLA_HEADS * V_ROWS).T.astype(BF16)
    vone = np.zeros((MLA_HEADS * V_ROWS, 1), np.float32)
    vone[ONES_ROW::V_ROWS, 0] = 1.0

    w_o = p["w_o"][l]
    a0, a1 = F_WIDTH, F_WIDTH + MLA_HEADS * V_HEAD_DIM
    w_dw = jnp.pad(p["w_dw"][l], ((0, 32 - CONV_KERNEL), (0, 0)))
    r2 = lambda v: v.reshape(1, -1)
    return {
        "w1": w1, "qg": r2(p["q_norm_g"][l]), "wqa": wqa, "wqb": wqb,
        "kvg": r2(p["kv_norm_g"][l]), "wk": wk, "wv": wv, "dft_c": dft_c, "vone": jnp.asarray(vone),
        "w_dw": w_dw, "b_dw": r2(p["b_dw"][l]), "ln_g": r2(p["conv_ln_g"][l]), "ln_b": r2(p["conv_ln_b"][l]),
        "w_pw2": p["w_pw2"][l].astype(BF16),
        "w_f": p["w_fourier"][l].astype(BF16), "wo_f": w_o[:a0].astype(BF16), "wo_a": w_o[a0:a1].astype(BF16),
        "wo_c": w_o[a1:].astype(BF16),
        "g1": r2(p["norm1_g"][l]), "g2": r2(p["norm2_g"][l]),
        "w_mlp1": p["w_mlp1"][l].astype(BF16), "w_mlp2": p["w_mlp2"][l].astype(BF16),
    }


def _token_tile(s):
    for tm in (512, 256, 128):
        if s % tm == 0:
            return tm
    raise ValueError(f"sequence length {s} must be a multiple of 128")


def kernel(x, c, ctx, c_ctx, w_mod, b_mod, norm1_g, w_in, q_norm_g, w_uq, kv_norm_g, w_ukv, w_fourier,
           w_dw, b_dw, conv_ln_g, conv_ln_b, w_pw2, w_o, norm2_g, w_mlp1, w_mlp2, final_norm_g):
    bsz, s, d = x.shape
    n_ctx = ctx.shape[1]
    depth = w_mod.shape[0]
    assert bsz < 8 and s % GRID_W == 0 and s % LANES == 0 and n_ctx % LANES == 0
    p = dict(w_in=w_in, w_uq=w_uq, w_ukv=w_ukv, q_norm_g=q_norm_g, kv_norm_g=kv_norm_g,
             w_fourier=w_fourier, w_dw=w_dw, b_dw=b_dw, conv_ln_g=conv_ln_g, conv_ln_b=conv_ln_b,
             w_pw2=w_pw2, w_o=w_o, norm1_g=norm1_g, norm2_g=norm2_g, w_mlp1=w_mlp1, w_mlp2=w_mlp2)

    c_rows = jnp.zeros((8, d), F32).at[:bsz].set(c).at[bsz].set(c_ctx)
    mods_all = _modulation(c_rows, w_mod, b_mod).reshape(depth, 8, N_MOD, 1, d)
    x_row = lambda b: b
    ctx_row = lambda b: bsz

    tm_x, tm_c = _token_tile(s), _token_tile(n_ctx)
    tq = next(t for t in (ATTN_QUERY_TILE, tm_x) if s % t == 0)
    tk = next(t for t in (ATTN_KEY_TILE, tm_x) if s % t == 0 and t % tm_x == 0)
    ta_x, tb_x = _rope_tables(s)
    ta_c, tb_c = _identity_rope_tables(n_ctx)
    fc = _fourier_consts(s, n_ctx)
    gf = final_norm_g.reshape(1, d)

    for l in range(depth):
        last = l == depth - 1
        w = _layer_weights(l, p, fc["dft_c"])
        mods = mods_all[l]

        qx, kx, vx, zx, ux = _prep(x, mods, x_row, w["g1"], w, ta_x, tb_x, tm_x)
        qc, kc, vc, zc, uc = _prep(ctx, mods, ctx_row, w["g1"], w, ta_c, tb_c, tm_c)

        attn_x = _attention(qx, kc, vc, kx, vx, tq, tk)
        yf_x = _fourier_seq(zx, fc["x"])
        yc_x = _conv_module(ux, w, tm_x)
        x = _mix_mlp(x, mods, x_row, yf_x, attn_x, yc_x, w, gf, tm_x, final_norm=last)

        if not last:
            attn_c = _attention(qc, kc, vc, None, None, tm_c, tk)
            yf_c = _fourier_seq(zc, fc["ctx"])
            yc_c = _conv_module(uc, w, tm_c)
            ctx = _mix_mlp(ctx, mods, ctx_row, yf_c, attn_c, yc_c, w, gf, tm_c, final_norm=False)
    return x
```

```python
import functools
import math

import numpy as np
import jax
import jax.numpy as jnp
from jax import lax
from jax.experimental import pallas as pl
from jax.experimental.pallas import tpu as pltpu

F32 = jnp.float32
BF16 = jnp.bfloat16

GRID_W = 64
F_GROUPS = 4
F_GROUP_DIM = 64
F_WIDTH = F_GROUPS * F_GROUP_DIM
MLA_HEADS = 8
QK_NOPE_DIM = 64
QK_ROPE_DIM = 32
V_HEAD_DIM = 64
Q_LORA_RANK = 256
KV_LORA_RANK = 128
CONV_WIDTH = 256
CONV_KERNEL = 31
N_MOD = 6
ROPE_BASE = 10000.0
ROPE_AXIS_DIM = QK_ROPE_DIM // 2
ATTN_SCALE = (QK_NOPE_DIM + QK_ROPE_DIM) ** -0.5
EPS = 1e-6

LANES = 128
HEAD_PAD = LANES
HP = MLA_HEADS * HEAD_PAD
ONES_ROW = V_HEAD_DIM
HEADS_PER_STEP = 2
V_ROWS = 80
CONV_HALO = 16
FFT_N2 = 128
VMEM_LIMIT = 56 * 1024 * 1024
TILES_PER_PASS = 10
LAZY_HEADROOM = 60.0
ATTN_KEY_TILE = 512
ATTN_QUERY_TILE = 1024
MLP_TILE = 1024
Q_SCALE = ATTN_SCALE * math.log2(math.e)


def _cparams(n_axes, flags=None):
    return pltpu.CompilerParams(dimension_semantics=("parallel",) * n_axes,
                                vmem_limit_bytes=VMEM_LIMIT, flags=flags)


def _const_spec(shape):
    nd = len(shape)
    return pl.BlockSpec(shape, lambda *_: (0,) * nd, pipeline_mode=pl.Buffered(1))


def _dot(a, b):
    return jnp.dot(a, b, preferred_element_type=F32)


def _split_bf16(v):
    hi = v.astype(BF16)
    lo = (v - hi.astype(F32)).astype(BF16)
    return hi, lo


def _mod_kernel(c_ref, w_ref, b_ref, o_ref):
    c = c_ref[...]
    s = c * jax.nn.sigmoid(c)
    s_hi, s_lo = _split_bf16(s)
    w_hi, w_lo = _split_bf16(w_ref[...])
    o_ref[...] = _dot(s_hi, w_hi) + _dot(s_hi, w_lo) + _dot(s_lo, w_hi) + b_ref[...]


def _modulation(c_rows, w_mod, b_mod):
    n_layers, d, n6 = w_mod.shape
    tn = 1536
    return pl.pallas_call(
        _mod_kernel,
        out_shape=jax.ShapeDtypeStruct((n_layers, 8, n6), F32),
        grid=(n_layers, n6 // tn),
        in_specs=[
            pl.BlockSpec((8, d), lambda l, j: (0, 0)),
            pl.BlockSpec((None, d, tn), lambda l, j: (l, 0, j)),
            pl.BlockSpec((None, 1, tn), lambda l, j: (l, 0, j)),
        ],
        out_specs=pl.BlockSpec((None, 8, tn), lambda l, j: (l, 0, j)),
        compiler_params=_cparams(2),
        name="modulation",
    )(c_rows, w_mod, b_mod.reshape(n_layers, 1, n6))


def _rms(v, g):
    return v * lax.rsqrt(jnp.mean(v * v, axis=-1, keepdims=True) + EPS) * g


def _prep_kernel(x_ref, shift_ref, scale_ref, g1_ref, w1_ref, qg_ref, wqa_ref, wqb_ref,
                 kvg_ref, wk_ref, wv_ref, dft_ref, ta_ref, tb_ref, vone_ref,
                 q_ref, k_ref, v_ref, z_ref, u_ref):
    x = x_ref[...]
    h = (_rms(x, g1_ref[...]) * (1.0 + scale_ref[...]) + shift_ref[...]).astype(BF16)
    z = _dot(h, w1_ref[...])

    zz = _dot(z[:, 0:F_WIDTH].astype(BF16), dft_ref[...])
    z_ref[0] = zz[:, :F_WIDTH]
    z_ref[1] = zz[:, F_WIDTH:]

    ta = ta_ref[...]
    tb = tb_ref[...]
    lane = lax.broadcasted_iota(jnp.int32, (1, HEAD_PAD), 1)
    nope = (lane < QK_NOPE_DIM).astype(F32)

    c0 = F_WIDTH
    cqn = _rms(z[:, c0:c0 + Q_LORA_RANK], qg_ref[...]).astype(BF16)
    qa = _dot(cqn, wqa_ref[...])
    qb = _dot(cqn, wqb_ref[...])
    qta = (ta + nope) * Q_SCALE
    qtb = tb * Q_SCALE
    for hd in range(MLA_HEADS):
        sl = slice(hd * HEAD_PAD, (hd + 1) * HEAD_PAD)
        q_ref[:, sl] = (qa[:, sl] * qta + qb[:, sl] * qtb).astype(BF16)

    c1 = c0 + Q_LORA_RANK
    ckvn = _rms(z[:, c1:c1 + KV_LORA_RANK], kvg_ref[...]).astype(BF16)
    kn = _dot(ckvn, wk_ref[...])
    c2 = c1 + KV_LORA_RANK
    kr = z[:, c2:c2 + HEAD_PAD] * ta + z[:, c2 + HEAD_PAD:c2 + 2 * HEAD_PAD] * tb
    for hd in range(MLA_HEADS):
        sl = slice(hd * HEAD_PAD, (hd + 1) * HEAD_PAD)
        k_ref[:, sl] = (kn[:, sl] + kr).astype(BF16)
    vt = lax.dot_general(wv_ref[...], ckvn, (((1,), (1,)), ((), ())), preferred_element_type=F32)
    vt = vt + vone_ref[...]
    for hd in range(MLA_HEADS):
        v_ref[hd] = vt[hd * V_ROWS:(hd + 1) * V_ROWS, :].astype(BF16)

    c3 = c2 + 2 * HEAD_PAD
    a = z[:, c3:c3 + CONV_WIDTH]
    g = z[:, c3 + CONV_WIDTH:c3 + 2 * CONV_WIDTH]
    u_ref[...] = (a * jax.nn.sigmoid(g)).astype(BF16)


def _prep(x, mods, mod_row, g1, w, ta, tb, tm):
    bsz, s, d = x.shape
    n1 = w["w1"].shape[1]
    row = lambda j: pl.BlockSpec((None, None, 1, d), lambda b, i: (mod_row(b), j, 0, 0))
    tok = lambda width: pl.BlockSpec((None, tm, width), lambda b, i: (b, i, 0))
    return pl.pallas_call(
        _prep_kernel,
        out_shape=(
            jax.ShapeDtypeStruct((bsz, s, HP), BF16),
            jax.ShapeDtypeStruct((bsz, s, HP), BF16),
            jax.ShapeDtypeStruct((bsz, MLA_HEADS, s // tm, V_ROWS, tm), BF16),
            jax.ShapeDtypeStruct((bsz, 2, s, F_WIDTH), F32),
            jax.ShapeDtypeStruct((bsz, s, CONV_WIDTH), BF16),
        ),
        grid=(bsz, s // tm),
        in_specs=[
            tok(d), row(0), row(1), _const_spec((1, d)),
            _const_spec((d, n1)),
            _const_spec((1, Q_LORA_RANK)), _const_spec((Q_LORA_RANK, HP)), _const_spec((Q_LORA_RANK, HP)),
            _const_spec((1, KV_LORA_RANK)), _const_spec((KV_LORA_RANK, HP)),
            _const_spec((MLA_HEADS * V_ROWS, KV_LORA_RANK)),
            _const_spec((F_WIDTH, 2 * F_WIDTH)),
            pl.BlockSpec((tm, HEAD_PAD), lambda b, i: (i, 0)),
            pl.BlockSpec((tm, HEAD_PAD), lambda b, i: (i, 0)),
            _const_spec((MLA_HEADS * V_ROWS, 1)),
        ],
        out_specs=(
            tok(HP), tok(HP),
            pl.BlockSpec((None, MLA_HEADS, None, V_ROWS, tm), lambda b, i: (b, 0, i, 0, 0)),
            pl.BlockSpec((None, 2, tm, F_WIDTH), lambda b, i: (b, 0, i, 0)),
            tok(CONV_WIDTH),
        ),
        compiler_params=_cparams(2),
        name="prep",
    )(x, mods, mods, g1, w["w1"], w["qg"], w["wqa"], w["wqb"], w["kvg"], w["wk"], w["wv"],
      w["dft_c"], ta, tb, w["vone"])


def _attn_kernel(*refs, tk, n_x_blocks):
    if n_x_blocks:
        q_ref, kc_ref, vc_ref, kx_ref, vx_ref, o_ref, m_sc, acc_sc, out_sc, ex_sc = refs
    else:
        q_ref, kc_ref, vc_ref, o_ref, m_sc, acc_sc, out_sc = refs
    for hh in range(HEADS_PER_STEP):
        lanes = slice(hh * HEAD_PAD, (hh + 1) * HEAD_PAD)
        qt = q_ref[:, lanes].astype(F32).T.astype(BF16)

        def exact_tile(k, vt, first=False):
            s = _dot(k, qt)
            m_tile = jnp.max(s, axis=0, keepdims=True)
            m_new = m_tile if first else jnp.maximum(m_sc[...], m_tile)
            p = jnp.exp2(s - m_new).astype(BF16)
            pv = _dot(vt, p)
            acc_sc[...] = pv if first else jnp.exp2(m_sc[...] - m_new) * acc_sc[...] + pv
            m_sc[...] = m_new

        exact_tile(kc_ref[:, lanes], vc_ref[hh, 0], first=True)
        if n_x_blocks:
            n = n_x_blocks
            v_sub = tk // vx_ref.shape[3]

            def k_tile(t):
                return kx_ref[pl.ds(pl.multiple_of(t * tk, tk), tk), lanes]

            def v_tile(t):
                return jnp.concatenate([vx_ref[hh, v_sub * t + i] for i in range(v_sub)], axis=1)

            def lazy_tile(t):
                s = _dot(k_tile(t), qt)
                m_old = m_sc[...]
                p = jnp.exp2(s - m_old).astype(BF16)
                m_tile = jnp.max(s, axis=0, keepdims=True)
                m_new = jnp.maximum(m_old, m_tile)
                acc_sc[...] = (acc_sc[...] + _dot(v_tile(t), p)) * jnp.exp2(m_old - m_new)
                m_sc[...] = m_new
                ex_sc[...] = jnp.maximum(ex_sc[...], m_tile - m_old)

            ex_sc[...] = jnp.zeros(ex_sc.shape, F32)
            n_peel = n % TILES_PER_PASS
            for t in range(n_peel):
                lazy_tile(t)

            def body(j, carry):
                t0 = n_peel + TILES_PER_PASS * j
                for u in range(TILES_PER_PASS):
                    lazy_tile(t0 + u)
                return carry
            lax.fori_loop(0, (n - n_peel) // TILES_PER_PASS, body, 0)

            @pl.when(jnp.max(ex_sc[...]) > LAZY_HEADROOM)
            def _():
                exact_tile(kc_ref[:, lanes], vc_ref[hh, 0], first=True)

                def exact_body(t, carry):
                    exact_tile(k_tile(t), v_tile(t))
                    return carry
                lax.fori_loop(0, n, exact_body, 0)

        acc = acc_sc[...]
        out_sc[hh * V_HEAD_DIM:(hh + 1) * V_HEAD_DIM, :] = acc[:V_HEAD_DIM] / acc[ONES_ROW:ONES_ROW + 1, :]
    o_ref[...] = out_sc[...].T.astype(BF16)


def _attention(q, kc, vc, kx, vx, tq, tk):
    bsz, sq, _ = q.shape
    n_ctx = kc.shape[1]
    hw = HEADS_PER_STEP * HEAD_PAD
    qspec = pl.BlockSpec((None, tq, hw), lambda b, h, i: (b, i, h))
    kspec = lambda n: pl.BlockSpec((None, n, hw), lambda b, h, i: (b, 0, h))
    vspec = lambda v: pl.BlockSpec((None, HEADS_PER_STEP) + v.shape[2:], lambda b, h, i: (b, h, 0, 0, 0))
    assert vc.shape[2:] == (1, V_ROWS, n_ctx)
    args = [q, kc, vc]
    in_specs = [qspec, kspec(n_ctx), vspec(vc)]
    n_x_blocks = 0
    out_w = HEADS_PER_STEP * V_HEAD_DIM
    scratch = [pltpu.VMEM((1, tq), F32), pltpu.VMEM((V_ROWS, tq), F32), pltpu.VMEM((out_w, tq), F32)]
    if kx is not None:
        s = kx.shape[1]
        n_x_blocks = s // tk
        assert vx.shape[3] == V_ROWS and tk % vx.shape[4] == 0 and vx.shape[2] * vx.shape[4] == s
        args += [kx, vx]
        in_specs += [kspec(s), vspec(vx)]
        scratch += [pltpu.VMEM((1, tq), F32)]
    return pl.pallas_call(
        functools.partial(_attn_kernel, tk=tk, n_x_blocks=n_x_blocks),
        out_shape=jax.ShapeDtypeStruct((bsz, sq, MLA_HEADS * V_HEAD_DIM), BF16),
        grid=(bsz, MLA_HEADS // HEADS_PER_STEP, sq // tq),
        in_specs=in_specs,
        out_specs=pl.BlockSpec((None, tq, out_w), lambda b, h, i: (b, i, h)),
        scratch_shapes=scratch,
        compiler_params=_cparams(3),
        name="attn_x" if kx is not None else "attn_ctx",
    )(*args)


def _fft1_kernel(z_ref, m_ref, o_ref, *, g, n1):
    for j in range(g):
        zz = jnp.concatenate([z_ref[0, :, j, :], z_ref[1, :, j, :]], axis=0).astype(BF16)
        r = _dot(m_ref[j], zz)
        o_ref[0, j] = r[:n1]
        o_ref[1, j] = r[n1:]


def _fft2_kernel(w_ref, x_ref, o_ref, *, g, scale):
    for j in range(g):
        xx = jnp.concatenate([x_ref[0, :, j, :], x_ref[1, :, j, :]], axis=0).astype(BF16)
        o_ref[:, j, :] = _dot(w_ref[...], xx) * scale


def _dft_direct_kernel(w_ref, x_ref, o_ref, *, scale):
    o_ref[...] = _dot(w_ref[...], x_ref[...].astype(BF16)) * scale


def _dft_cos_sin(rows, cols, period):
    r = (rows[:, None] * cols[None, :]) % period
    ang = r.astype(F32) * (2.0 * math.pi / period)
    return jnp.cos(ang), jnp.sin(ang)


def _fourier_seq(z, consts):
    bsz, _, s, c = z.shape
    scale = 1.0 / math.sqrt(s * F_GROUP_DIM)
    if s <= 2 * FFT_N2:
        w = consts["w_direct"]
        return pl.pallas_call(
            functools.partial(_dft_direct_kernel, scale=scale),
            out_shape=jax.ShapeDtypeStruct((bsz, s, c), F32),
            grid=(bsz,),
            in_specs=[_const_spec(w.shape), pl.BlockSpec((None, 2 * s, c), lambda b: (b, 0, 0))],
            out_specs=pl.BlockSpec((None, s, c), lambda b: (b, 0, 0)),
            compiler_params=_cparams(1),
            name="dft_direct",
        )(w, z.reshape(bsz, 2 * s, c))
    n1, n2 = s // FFT_N2, FFT_N2
    g = 8
    x2 = pl.pallas_call(
        functools.partial(_fft1_kernel, g=g, n1=n1),
        out_shape=jax.ShapeDtypeStruct((bsz, 2, n2, n1, c), F32),
        grid=(bsz, n2 // g),
        in_specs=[
            pl.BlockSpec((None, 2, n1, g, c), lambda b, j: (b, 0, 0, j, 0)),
            pl.BlockSpec((g, 2 * n1, 2 * n1), lambda b, j: (j, 0, 0)),
        ],
        out_specs=pl.BlockSpec((None, 2, g, n1, c), lambda b, j: (b, 0, j, 0, 0)),
        compiler_params=_cparams(2),
        name="fft_stage1",
    )(z.reshape(bsz, 2, n1, n2, c), consts["m1"])
    w2 = consts["w2"]
    y = pl.pallas_call(
        functools.partial(_fft2_kernel, g=g, scale=scale),
        out_shape=jax.ShapeDtypeStruct((bsz, n2, n1, c), F32),
        grid=(bsz, n1 // g),
        in_specs=[
            _const_spec(w2.shape),
            pl.BlockSpec((None, 2, n2, g, c), lambda b, j: (b, 0, 0, j, 0)),
        ],
        out_specs=pl.BlockSpec((None, n2, g, c), lambda b, j: (b, 0, j, 0)),
        compiler_params=_cparams(2),
        name="fft_stage2",
    )(w2, x2)
    return y.reshape(bsz, s, c)


def _conv_kernel(uc_ref, up_ref, un_ref, wdw_ref, bdw_ref, lng_ref, lnb_ref, wpw_ref, o_ref, ext_sc, sh_sc,
                 *, tm, rc):
    i = pl.program_id(1)
    last = pl.num_programs(1) - 1
    ext_sc[0:CONV_HALO] = jnp.where(i > 0, up_ref[...].astype(F32), 0.0)
    ext_sc[CONV_HALO:CONV_HALO + tm] = uc_ref[...].astype(F32)
    ext_sc[CONV_HALO + tm:2 * CONV_HALO + tm] = jnp.where(i < last, un_ref[...].astype(F32), 0.0)
    first = CONV_HALO - CONV_KERNEL // 2
    sub = sh_sc.shape[0]
    for r in range(sub):
        sh_sc[r] = ext_sc[r:r + sh_sc.shape[1], :]
    for r0 in range(0, tm, rc):
        acc = jnp.zeros((rc, CONV_WIDTH), F32)
        for k in range(CONV_KERNEL):
            r = (first + k) % sub
            lo = r0 + (first + k) - r
            acc = acc + sh_sc[r, lo:lo + rc, :] * wdw_ref[k:k + 1, :]
        acc = acc + bdw_ref[...]
        mu = jnp.mean(acc, axis=-1, keepdims=True)
        xc = acc - mu
        var = jnp.mean(xc * xc, axis=-1, keepdims=True)
        y = xc * lax.rsqrt(var + EPS) * lng_ref[...] + lnb_ref[...]
        y = y * jax.nn.sigmoid(y)
        o_ref[r0:r0 + rc, :] = _dot(y.astype(BF16), wpw_ref[...]).astype(BF16)


def _conv_module(u, w, tm):
    bsz, s, c = u.shape
    hb = tm // CONV_HALO
    n_halo = s // CONV_HALO
    return pl.pallas_call(
        functools.partial(_conv_kernel, tm=tm, rc=64),
        out_shape=jax.ShapeDtypeStruct((bsz, s, c), BF16),
        grid=(bsz, s // tm),
        in_specs=[
            pl.BlockSpec((None, tm, c), lambda b, i: (b, i, 0)),
            pl.BlockSpec((None, CONV_HALO, c), lambda b, i: (b, jnp.maximum(i * hb - 1, 0), 0)),
            pl.BlockSpec((None, CONV_HALO, c), lambda b, i: (b, jnp.minimum((i + 1) * hb, n_halo - 1), 0)),
            _const_spec((32, c)), _const_spec((1, c)), _const_spec((1, c)), _const_spec((1, c)),
            _const_spec((c, c)),
        ],
        out_specs=pl.BlockSpec((None, tm, c), lambda b, i: (b, i, 0)),
        scratch_shapes=[pltpu.VMEM((tm + 2 * CONV_HALO, c), F32),
                        pltpu.VMEM((8, tm + 2 * CONV_HALO - 8, c), F32)],
        compiler_params=_cparams(2),
        name="conv",
    )(u, u, u, w["w_dw"], w["b_dw"], w["ln_g"], w["ln_b"], w["w_pw2"])


def _mix_mlp_kernel(x_ref, gate1_ref, shift_ref, scale_ref, gate2_ref, yf_ref, at_ref, yc_ref,
                    wf_ref, wof_ref, woa_ref, woc_ref, g2_ref, w1_ref, w2_ref, gf_ref, o_ref,
                    *, ff_chunk, final_norm):
    yf = _dot(yf_ref[...].astype(BF16), wf_ref[...]).astype(BF16)
    mix = _dot(yf, wof_ref[...]) + _dot(at_ref[...], woa_ref[...]) + _dot(yc_ref[...], woc_ref[...])
    x = x_ref[...] + gate1_ref[...] * mix
    h = (_rms(x, g2_ref[...]) * (1.0 + scale_ref[...]) + shift_ref[...]).astype(BF16)
    d_ff = w1_ref.shape[1]
    acc = jnp.zeros(x.shape, F32)
    for c0 in range(0, d_ff, ff_chunk):
        a = jnp.maximum(_dot(h, w1_ref[:, c0:c0 + ff_chunk]), 0.0)
        acc = acc + _dot((a * a).astype(BF16), w2_ref[c0:c0 + ff_chunk, :])
    y = x + gate2_ref[...] * acc
    if final_norm:
        y = _rms(y, gf_ref[...])
    o_ref[...] = y


def _mix_mlp(x, mods, mod_row, yf, attn, yc, w, mlp_w, layer, gf, tm, final_norm):
    bsz, s, d = x.shape
    w_mlp1, w_mlp2 = mlp_w
    d_ff = w_mlp1.shape[2]
    layer_spec = lambda r, c: pl.BlockSpec((None, r, c), lambda b, i: (layer, 0, 0), pipeline_mode=pl.Buffered(1))
    row = lambda j: pl.BlockSpec((None, None, 1, d), lambda b, i: (mod_row(b), j, 0, 0))
    tok = lambda width: pl.BlockSpec((None, tm, width), lambda b, i: (b, i, 0))
    return pl.pallas_call(
        functools.partial(_mix_mlp_kernel, ff_chunk=1024, final_norm=final_norm),
        out_shape=jax.ShapeDtypeStruct((bsz, s, d), F32),
        grid=(bsz, s // tm),
        in_specs=[
            tok(d), row(2), row(3), row(4), row(5),
            tok(F_WIDTH), tok(MLA_HEADS * V_HEAD_DIM), tok(CONV_WIDTH),
            _const_spec((F_WIDTH, F_WIDTH)), _const_spec((F_WIDTH, d)), _const_spec((MLA_HEADS * V_HEAD_DIM, d)),
            _const_spec((CONV_WIDTH, d)),
            _const_spec((1, d)), layer_spec(d, d_ff), layer_spec(d_ff, d), _const_spec((1, d)),
        ],
        out_specs=tok(d),
        compiler_params=_cparams(2),
        name="mix_mlp",
    )(x, mods, mods, mods, mods, yf, attn, yc, w["w_f"], w["wo_f"], w["wo_a"], w["wo_c"],
      w["g2"], w_mlp1, w_mlp2, gf)


def _rope_tables(s):
    t = jnp.arange(s)
    row = (t // GRID_W).astype(F32)
    col = (t % GRID_W).astype(F32)
    inv = ROPE_BASE ** (-jnp.arange(0, ROPE_AXIS_DIM, 2, dtype=F32) / ROPE_AXIS_DIM)
    ang = jnp.concatenate([row[:, None] * inv, col[:, None] * inv], axis=-1)
    cos, sin = jnp.cos(ang), jnp.sin(ang)
    z_lo = jnp.zeros((s, QK_NOPE_DIM), F32)
    z_hi = jnp.zeros((s, HEAD_PAD - QK_NOPE_DIM - QK_ROPE_DIM), F32)
    ta = jnp.concatenate([z_lo, cos, cos, z_hi], axis=-1)
    tb = jnp.concatenate([z_lo, -sin, sin, z_hi], axis=-1)
    return ta, tb


def _identity_rope_tables(s):
    lane = np.arange(HEAD_PAD)
    ta = ((lane >= QK_NOPE_DIM) & (lane < QK_NOPE_DIM + QK_ROPE_DIM)).astype(np.float32)
    return jnp.asarray(np.tile(ta, (s, 1))), jnp.zeros((s, HEAD_PAD), F32)


def _fourier_consts(s, n_ctx):
    ar = lambda n: jnp.arange(n, dtype=jnp.int32)
    cc, sc = _dft_cos_sin(ar(F_GROUP_DIM), ar(F_GROUP_DIM), F_GROUP_DIM)
    eye = jnp.eye(F_GROUPS, dtype=F32)
    dft_c = jnp.concatenate([jnp.kron(eye, cc), -jnp.kron(eye, sc)], axis=1).astype(BF16)
    consts = {"dft_c": dft_c}

    def direct(n):
        c, sn = _dft_cos_sin(ar(n), ar(n), n)
        return jnp.concatenate([c, sn], axis=1).astype(BF16)

    def two_stage(n):
        n1, n2 = n // FFT_N2, FFT_N2
        pos = ar(n1)[None, :] * n2 + ar(n2)[:, None]
        r = (ar(n1)[None, :, None] * pos[:, None, :]) % n
        ang = r.astype(F32) * (2.0 * math.pi / n)
        c, sn = jnp.cos(ang), jnp.sin(ang)
        m1 = jnp.concatenate([jnp.concatenate([c, sn], axis=2),
                              jnp.concatenate([-sn, c], axis=2)], axis=1).astype(BF16)
        return m1, direct(n2)

    for n, tag in ((s, "x"), (n_ctx, "ctx")):
        if n <= 2 * FFT_N2:
            consts[tag] = {"w_direct": direct(n)}
        else:
            m1, w2 = two_stage(n)
            consts[tag] = {"m1": m1, "w2": w2}
    return consts


def _pad_heads(w, width):
    k = w.shape[0]
    w = w.reshape(k, MLA_HEADS, width)
    return jnp.pad(w, ((0, 0), (0, 0), (0, HEAD_PAD - width))).reshape(k, HP)


def _layer_weights(l, p, dft_c):
    d = p["w_in"].shape[1]
    w_in = p["w_in"][l]
    s2 = F_WIDTH + Q_LORA_RANK + KV_LORA_RANK
    s3 = s2 + QK_ROPE_DIM
    half = QK_ROPE_DIM // 2
    w_kr = w_in[:, s2:s3]
    w_kr_sw = jnp.concatenate([w_kr[:, half:], w_kr[:, :half]], axis=1)
    lo = jnp.zeros((d, QK_NOPE_DIM), F32)
    hi = jnp.zeros((d, HEAD_PAD - QK_NOPE_DIM - QK_ROPE_DIM), F32)
    w1 = jnp.concatenate([w_in[:, :s2], lo, w_kr, hi, lo, w_kr_sw, hi, w_in[:, s3:]], axis=1).astype(BF16)

    qd = QK_NOPE_DIM + QK_ROPE_DIM
    w_uq = p["w_uq"][l].reshape(Q_LORA_RANK, MLA_HEADS, qd)
    rope = w_uq[:, :, QK_NOPE_DIM:]
    rope_sw = jnp.concatenate([rope[:, :, half:], rope[:, :, :half]], axis=2)
    wqa = _pad_heads(w_uq.reshape(Q_LORA_RANK, MLA_HEADS * qd), qd).astype(BF16)
    wqb = jnp.pad(rope_sw, ((0, 0), (0, 0), (QK_NOPE_DIM, HEAD_PAD - qd))).reshape(Q_LORA_RANK, HP).astype(BF16)

    w_ukv = p["w_ukv"][l].reshape(KV_LORA_RANK, MLA_HEADS, QK_NOPE_DIM + V_HEAD_DIM)
    wk = _pad_heads(w_ukv[:, :, :QK_NOPE_DIM].reshape(KV_LORA_RANK, -1), QK_NOPE_DIM).astype(BF16)
    wv = jnp.pad(w_ukv[:, :, QK_NOPE_DIM:], ((0, 0), (0, 0), (0, V_ROWS - V_HEAD_DIM)))
    wv = wv.reshape(KV_LORA_RANK, MLA_HEADS * V_ROWS).T.astype(BF16)
    vone = np.zeros((MLA_HEADS * V_ROWS, 1), np.float32)
    vone[ONES_ROW::V_ROWS, 0] = 1.0

    w_o = p["w_o"][l]
    a0, a1 = F_WIDTH, F_WIDTH + MLA_HEADS * V_HEAD_DIM
    w_dw = jnp.pad(p["w_dw"][l], ((0, 32 - CONV_KERNEL), (0, 0)))
    r2 = lambda v: v.reshape(1, -1)
    return {
        "w1": w1, "qg": r2(p["q_norm_g"][l]), "wqa": wqa, "wqb": wqb,
        "kvg": r2(p["kv_norm_g"][l]), "wk": wk, "wv": wv, "dft_c": dft_c, "vone": jnp.asarray(vone),
        "w_dw": w_dw, "b_dw": r2(p["b_dw"][l]), "ln_g": r2(p["conv_ln_g"][l]), "ln_b": r2(p["conv_ln_b"][l]),
        "w_pw2": p["w_pw2"][l].astype(BF16),
        "w_f": p["w_fourier"][l].astype(BF16), "wo_f": w_o[:a0].astype(BF16), "wo_a": w_o[a0:a1].astype(BF16),
        "wo_c": w_o[a1:].astype(BF16),
        "g1": r2(p["norm1_g"][l]), "g2": r2(p["norm2_g"][l]),
    }


def _token_tile(s):
    for tm in (512, 256, 128):
        if s % tm == 0:
            return tm
    raise ValueError(f"sequence length {s} must be a multiple of 128")


def kernel(x, c, ctx, c_ctx, w_mod, b_mod, norm1_g, w_in, q_norm_g, w_uq, kv_norm_g, w_ukv, w_fourier,
           w_dw, b_dw, conv_ln_g, conv_ln_b, w_pw2, w_o, norm2_g, w_mlp1, w_mlp2, final_norm_g):
    bsz, s, d = x.shape
    n_ctx = ctx.shape[1]
    depth = w_mod.shape[0]
    assert bsz < 8 and s % GRID_W == 0 and s % LANES == 0 and n_ctx % LANES == 0
    p = dict(w_in=w_in, w_uq=w_uq, w_ukv=w_ukv, q_norm_g=q_norm_g, kv_norm_g=kv_norm_g,
             w_fourier=w_fourier, w_dw=w_dw, b_dw=b_dw, conv_ln_g=conv_ln_g, conv_ln_b=conv_ln_b,
             w_pw2=w_pw2, w_o=w_o, norm1_g=norm1_g, norm2_g=norm2_g)
    mlp_w = (w_mlp1.astype(BF16), w_mlp2.astype(BF16))

    c_rows = jnp.zeros((8, d), F32).at[:bsz].set(c).at[bsz].set(c_ctx)
    mods_all = _modulation(c_rows, w_mod, b_mod).reshape(depth, 8, N_MOD, 1, d)
    x_row = lambda b: b
    ctx_row = lambda b: bsz

    tm_x, tm_c = _token_tile(s), _token_tile(n_ctx)
    tq = next(t for t in (ATTN_QUERY_TILE, tm_x) if s % t == 0)
    tm_mlp = next(t for t in (MLP_TILE, tm_x) if s % t == 0)
    tk = next(t for t in (ATTN_KEY_TILE, tm_x) if s % t == 0 and t % tm_x == 0)
    ta_x, tb_x = _rope_tables(s)
    ta_c, tb_c = _identity_rope_tables(n_ctx)
    fc = _fourier_consts(s, n_ctx)
    gf = final_norm_g.reshape(1, d)

    for l in range(depth):
        last = l == depth - 1
        w = _layer_weights(l, p, fc["dft_c"])
        mods = mods_all[l]

        qx, kx, vx, zx, ux = _prep(x, mods, x_row, w["g1"], w, ta_x, tb_x, tm_x)
        qc, kc, vc, zc, uc = _prep(ctx, mods, ctx_row, w["g1"], w, ta_c, tb_c, tm_c)

        attn_x = _attention(qx, kc, vc, kx, vx, tq, tk)
        yf_x = _fourier_seq(zx, fc["x"])
        yc_x = _conv_module(ux, w, tm_x)
        x = _mix_mlp(x, mods, x_row, yf_x, attn_x, yc_x, w, mlp_w, l, gf, tm_mlp, final_norm=last)

        if not last:
            attn_c = _attention(qc, kc, vc, None, None, tm_c, tk)
            yf_c = _fourier_seq(zc, fc["ctx"])
            yc_c = _conv_module(uc, w, tm_c)
            ctx = _mix_mlp(ctx, mods, ctx_row, yf_c, attn_c, yc_c, w, mlp_w, l, gf, tm_c, final_norm=False)
    return x
```

```python
import functools
import math

import numpy as np
import jax
import jax.numpy as jnp
from jax import lax
from jax.experimental import pallas as pl
from jax.experimental.pallas import tpu as pltpu

F32 = jnp.float32
BF16 = jnp.bfloat16

GRID_W = 64
F_GROUPS = 4
F_GROUP_DIM = 64
F_WIDTH = F_GROUPS * F_GROUP_DIM
MLA_HEADS = 8
QK_NOPE_DIM = 64
QK_ROPE_DIM = 32
V_HEAD_DIM = 64
Q_LORA_RANK = 256
KV_LORA_RANK = 128
CONV_WIDTH = 256
CONV_KERNEL = 31
N_MOD = 6
ROPE_BASE = 10000.0
ROPE_AXIS_DIM = QK_ROPE_DIM // 2
ATTN_SCALE = (QK_NOPE_DIM + QK_ROPE_DIM) ** -0.5
EPS = 1e-6

LANES = 128
HEAD_PAD = LANES
HP = MLA_HEADS * HEAD_PAD
ONES_ROW = V_HEAD_DIM
HEADS_PER_STEP = 2
V_ROWS = 80
CONV_HALO = 16
FFT_N2 = 128
VMEM_LIMIT = 56 * 1024 * 1024
TILES_PER_PASS = 6
LAZY_HEADROOM = 60.0
ATTN_KEY_TILE = 512
ATTN_QUERY_TILE = 2048
MLP_TILE = 1024
Q_SCALE = ATTN_SCALE * math.log2(math.e)


def _cparams(n_axes, flags=None):
    return pltpu.CompilerParams(dimension_semantics=("parallel",) * n_axes,
                                vmem_limit_bytes=VMEM_LIMIT, flags=flags)


def _const_spec(shape):
    nd = len(shape)
    return pl.BlockSpec(shape, lambda *_: (0,) * nd, pipeline_mode=pl.Buffered(1))


def _dot(a, b):
    return jnp.dot(a, b, preferred_element_type=F32)


def _split_bf16(v):
    hi = v.astype(BF16)
    lo = (v - hi.astype(F32)).astype(BF16)
    return hi, lo


def _mod_kernel(c_ref, w_ref, b_ref, o_ref):
    c = c_ref[...]
    s = c * jax.nn.sigmoid(c)
    s_hi, s_lo = _split_bf16(s)
    w_hi, w_lo = _split_bf16(w_ref[...])
    o_ref[...] = _dot(s_hi, w_hi) + _dot(s_hi, w_lo) + _dot(s_lo, w_hi) + b_ref[...]


def _modulation(c_rows, w_mod, b_mod):
    n_layers, d, n6 = w_mod.shape
    tn = 1536
    return pl.pallas_call(
        _mod_kernel,
        out_shape=jax.ShapeDtypeStruct((n_layers, 8, n6), F32),
        grid=(n_layers, n6 // tn),
        in_specs=[
            pl.BlockSpec((8, d), lambda l, j: (0, 0)),
            pl.BlockSpec((None, d, tn), lambda l, j: (l, 0, j)),
            pl.BlockSpec((None, 1, tn), lambda l, j: (l, 0, j)),
        ],
        out_specs=pl.BlockSpec((None, 8, tn), lambda l, j: (l, 0, j)),
        compiler_params=_cparams(2),
        name="modulation",
    )(c_rows, w_mod, b_mod.reshape(n_layers, 1, n6))


def _rms(v, g):
    return v * lax.rsqrt(jnp.mean(v * v, axis=-1, keepdims=True) + EPS) * g


def _prep_kernel(x_ref, shift_ref, scale_ref, g1_ref, w1_ref, qg_ref, wqa_ref, wqb_ref,
                 kvg_ref, wk_ref, wv_ref, dft_ref, ta_ref, tb_ref, vone_ref,
                 q_ref, k_ref, v_ref, z_ref, u_ref):
    x = x_ref[...]
    h = (_rms(x, g1_ref[...]) * (1.0 + scale_ref[...]) + shift_ref[...]).astype(BF16)
    z = _dot(h, w1_ref[...])

    zz = _dot(z[:, 0:F_WIDTH].astype(BF16), dft_ref[...])
    z_ref[0] = zz[:, :F_WIDTH]
    z_ref[1] = zz[:, F_WIDTH:]

    ta = ta_ref[...]
    tb = tb_ref[...]
    lane = lax.broadcasted_iota(jnp.int32, (1, HEAD_PAD), 1)
    nope = (lane < QK_NOPE_DIM).astype(F32)

    c0 = F_WIDTH
    cqn = _rms(z[:, c0:c0 + Q_LORA_RANK], qg_ref[...]).astype(BF16)
    qa = _dot(cqn, wqa_ref[...])
    qb = _dot(cqn, wqb_ref[...])
    qta = (ta + nope) * Q_SCALE
    qtb = tb * Q_SCALE
    for hd in range(MLA_HEADS):
        sl = slice(hd * HEAD_PAD, (hd + 1) * HEAD_PAD)
        q_ref[:, sl] = (qa[:, sl] * qta + qb[:, sl] * qtb).astype(BF16)

    c1 = c0 + Q_LORA_RANK
    ckvn = _rms(z[:, c1:c1 + KV_LORA_RANK], kvg_ref[...]).astype(BF16)
    kn = _dot(ckvn, wk_ref[...])
    c2 = c1 + KV_LORA_RANK
    kr = z[:, c2:c2 + HEAD_PAD] * ta + z[:, c2 + HEAD_PAD:c2 + 2 * HEAD_PAD] * tb
    for hd in range(MLA_HEADS):
        sl = slice(hd * HEAD_PAD, (hd + 1) * HEAD_PAD)
        k_ref[:, sl] = (kn[:, sl] + kr).astype(BF16)
    vt = lax.dot_general(wv_ref[...], ckvn, (((1,), (1,)), ((), ())), preferred_element_type=F32)
    vt = vt + vone_ref[...]
    for hd in range(MLA_HEADS):
        v_ref[hd] = vt[hd * V_ROWS:(hd + 1) * V_ROWS, :].astype(BF16)

    c3 = c2 + 2 * HEAD_PAD
    a = z[:, c3:c3 + CONV_WIDTH]
    g = z[:, c3 + CONV_WIDTH:c3 + 2 * CONV_WIDTH]
    u_ref[...] = (a * jax.nn.sigmoid(g)).astype(BF16)


def _prep(x, mods, mod_row, g1, w, ta, tb, tm):
    bsz, s, d = x.shape
    n1 = w["w1"].shape[1]
    row = lambda j: pl.BlockSpec((None, None, 1, d), lambda b, i: (mod_row(b), j, 0, 0))
    tok = lambda width: pl.BlockSpec((None, tm, width), lambda b, i: (b, i, 0))
    return pl.pallas_call(
        _prep_kernel,
        out_shape=(
            jax.ShapeDtypeStruct((bsz, s, HP), BF16),
            jax.ShapeDtypeStruct((bsz, s, HP), BF16),
            jax.ShapeDtypeStruct((bsz, MLA_HEADS, s // tm, V_ROWS, tm), BF16),
            jax.ShapeDtypeStruct((bsz, 2, s, F_WIDTH), F32),
            jax.ShapeDtypeStruct((bsz, s, CONV_WIDTH), BF16),
        ),
        grid=(bsz, s // tm),
        in_specs=[
            tok(d), row(0), row(1), _const_spec((1, d)),
            _const_spec((d, n1)),
            _const_spec((1, Q_LORA_RANK)), _const_spec((Q_LORA_RANK, HP)), _const_spec((Q_LORA_RANK, HP)),
            _const_spec((1, KV_LORA_RANK)), _const_spec((KV_LORA_RANK, HP)),
            _const_spec((MLA_HEADS * V_ROWS, KV_LORA_RANK)),
            _const_spec((F_WIDTH, 2 * F_WIDTH)),
            pl.BlockSpec((tm, HEAD_PAD), lambda b, i: (i, 0)),
            pl.BlockSpec((tm, HEAD_PAD), lambda b, i: (i, 0)),
            _const_spec((MLA_HEADS * V_ROWS, 1)),
        ],
        out_specs=(
            tok(HP), tok(HP),
            pl.BlockSpec((None, MLA_HEADS, None, V_ROWS, tm), lambda b, i: (b, 0, i, 0, 0)),
            pl.BlockSpec((None, 2, tm, F_WIDTH), lambda b, i: (b, 0, i, 0)),
            tok(CONV_WIDTH),
        ),
        compiler_params=_cparams(2),
        name="prep",
    )(x, mods, mods, g1, w["w1"], w["qg"], w["wqa"], w["wqb"], w["kvg"], w["wk"], w["wv"],
      w["dft_c"], ta, tb, w["vone"])


def _attn_kernel(*refs, tk, n_x_blocks):
    if n_x_blocks:
        q_ref, kc_ref, vc_ref, kx_ref, vx_ref, o_ref, m_sc, acc_sc, out_sc, ex_sc = refs
    else:
        q_ref, kc_ref, vc_ref, o_ref, m_sc, acc_sc, out_sc = refs
    for hh in range(HEADS_PER_STEP):
        lanes = slice(hh * HEAD_PAD, (hh + 1) * HEAD_PAD)
        qt = q_ref[:, lanes].astype(F32).T.astype(BF16)

        def exact_tile(k, vt, first=False):
            s = _dot(k, qt)
            m_tile = jnp.max(s, axis=0, keepdims=True)
            m_new = m_tile if first else jnp.maximum(m_sc[...], m_tile)
            p = jnp.exp2(s - m_new).astype(BF16)
            pv = _dot(vt, p)
            acc_sc[...] = pv if first else jnp.exp2(m_sc[...] - m_new) * acc_sc[...] + pv
            m_sc[...] = m_new

        exact_tile(kc_ref[:, lanes], vc_ref[hh, 0], first=True)
        if n_x_blocks:
            n = n_x_blocks
            v_sub = tk // vx_ref.shape[3]

            def k_tile(t):
                return kx_ref[pl.ds(pl.multiple_of(t * tk, tk), tk), lanes]

            def v_tile(t):
                return jnp.concatenate([vx_ref[hh, v_sub * t + i] for i in range(v_sub)], axis=1)

            def lazy_tile(t):
                s = _dot(k_tile(t), qt)
                m_old = m_sc[...]
                p = jnp.exp2(s - m_old).astype(BF16)
                m_tile = jnp.max(s, axis=0, keepdims=True)
                m_new = jnp.maximum(m_old, m_tile)
                acc_sc[...] = (acc_sc[...] + _dot(v_tile(t), p)) * jnp.exp2(m_old - m_new)
                m_sc[...] = m_new
                ex_sc[...] = jnp.maximum(ex_sc[...], m_tile - m_old)

            ex_sc[...] = jnp.zeros(ex_sc.shape, F32)
            n_peel = n % TILES_PER_PASS
            for t in range(n_peel):
                lazy_tile(t)

            def body(j, carry):
                t0 = n_peel + TILES_PER_PASS * j
                for u in range(TILES_PER_PASS):
                    lazy_tile(t0 + u)
                return carry
            lax.fori_loop(0, (n - n_peel) // TILES_PER_PASS, body, 0)

            @pl.when(jnp.max(ex_sc[...]) > LAZY_HEADROOM)
            def _():
                exact_tile(kc_ref[:, lanes], vc_ref[hh, 0], first=True)

                def exact_body(t, carry):
                    exact_tile(k_tile(t), v_tile(t))
                    return carry
                lax.fori_loop(0, n, exact_body, 0)

        acc = acc_sc[...]
        out_sc[hh * V_HEAD_DIM:(hh + 1) * V_HEAD_DIM, :] = acc[:V_HEAD_DIM] / acc[ONES_ROW:ONES_ROW + 1, :]
    o_ref[...] = out_sc[...].T.astype(BF16)


def _attention(q, kc, vc, kx, vx, tq, tk):
    bsz, sq, _ = q.shape
    n_ctx = kc.shape[1]
    hw = HEADS_PER_STEP * HEAD_PAD
    qspec = pl.BlockSpec((None, tq, hw), lambda b, h, i: (b, i, h))
    kspec = lambda n: pl.BlockSpec((None, n, hw), lambda b, h, i: (b, 0, h))
    vspec = lambda v: pl.BlockSpec((None, HEADS_PER_STEP) + v.shape[2:], lambda b, h, i: (b, h, 0, 0, 0))
    assert vc.shape[2:] == (1, V_ROWS, n_ctx)
    args = [q, kc, vc]
    in_specs = [qspec, kspec(n_ctx), vspec(vc)]
    n_x_blocks = 0
    out_w = HEADS_PER_STEP * V_HEAD_DIM
    scratch = [pltpu.VMEM((1, tq), F32), pltpu.VMEM((V_ROWS, tq), F32), pltpu.VMEM((out_w, tq), F32)]
    if kx is not None:
        s = kx.shape[1]
        n_x_blocks = s // tk
        assert vx.shape[3] == V_ROWS and tk % vx.shape[4] == 0 and vx.shape[2] * vx.shape[4] == s
        args += [kx, vx]
        in_specs += [kspec(s), vspec(vx)]
        scratch += [pltpu.VMEM((1, tq), F32)]
    return pl.pallas_call(
        functools.partial(_attn_kernel, tk=tk, n_x_blocks=n_x_blocks),
        out_shape=jax.ShapeDtypeStruct((bsz, sq, MLA_HEADS * V_HEAD_DIM), BF16),
        grid=(bsz, MLA_HEADS // HEADS_PER_STEP, sq // tq),
        in_specs=in_specs,
        out_specs=pl.BlockSpec((None, tq, out_w), lambda b, h, i: (b, i, h)),
        scratch_shapes=scratch,
        compiler_params=_cparams(3),
        name="attn_x" if kx is not None else "attn_ctx",
    )(*args)


def _fft1_kernel(z_ref, m_ref, o_ref, *, g, n1):
    for j in range(g):
        zz = jnp.concatenate([z_ref[0, :, j, :], z_ref[1, :, j, :]], axis=0).astype(BF16)
        r = _dot(m_ref[j], zz)
        o_ref[0, j] = r[:n1]
        o_ref[1, j] = r[n1:]


def _fft2_kernel(w_ref, x_ref, o_ref, *, g, scale):
    for j in range(g):
        xx = jnp.concatenate([x_ref[0, :, j, :], x_ref[1, :, j, :]], axis=0).astype(BF16)
        o_ref[:, j, :] = _dot(w_ref[...], xx) * scale


def _dft_direct_kernel(w_ref, x_ref, o_ref, *, scale):
    o_ref[...] = _dot(w_ref[...], x_ref[...].astype(BF16)) * scale


def _dft_cos_sin(rows, cols, period):
    r = (rows[:, None] * cols[None, :]) % period
    ang = r.astype(F32) * (2.0 * math.pi / period)
    return jnp.cos(ang), jnp.sin(ang)


def _fourier_seq(z, consts):
    bsz, _, s, c = z.shape
    scale = 1.0 / math.sqrt(s * F_GROUP_DIM)
    if s <= 2 * FFT_N2:
        w = consts["w_direct"]
        return pl.pallas_call(
            functools.partial(_dft_direct_kernel, scale=scale),
            out_shape=jax.ShapeDtypeStruct((bsz, s, c), F32),
            grid=(bsz,),
            in_specs=[_const_spec(w.shape), pl.BlockSpec((None, 2 * s, c), lambda b: (b, 0, 0))],
            out_specs=pl.BlockSpec((None, s, c), lambda b: (b, 0, 0)),
            compiler_params=_cparams(1),
            name="dft_direct",
        )(w, z.reshape(bsz, 2 * s, c))
    n1, n2 = s // FFT_N2, FFT_N2
    g = 8
    x2 = pl.pallas_call(
        functools.partial(_fft1_kernel, g=g, n1=n1),
        out_shape=jax.ShapeDtypeStruct((bsz, 2, n2, n1, c), F32),
        grid=(bsz, n2 // g),
        in_specs=[
            pl.BlockSpec((None, 2, n1, g, c), lambda b, j: (b, 0, 0, j, 0)),
            pl.BlockSpec((g, 2 * n1, 2 * n1), lambda b, j: (j, 0, 0)),
        ],
        out_specs=pl.BlockSpec((None, 2, g, n1, c), lambda b, j: (b, 0, j, 0, 0)),
        compiler_params=_cparams(2),
        name="fft_stage1",
    )(z.reshape(bsz, 2, n1, n2, c), consts["m1"])
    w2 = consts["w2"]
    y = pl.pallas_call(
        functools.partial(_fft2_kernel, g=g, scale=scale),
        out_shape=jax.ShapeDtypeStruct((bsz, n2, n1, c), F32),
        grid=(bsz, n1 // g),
        in_specs=[
            _const_spec(w2.shape),
            pl.BlockSpec((None, 2, n2, g, c), lambda b, j: (b, 0, 0, j, 0)),
        ],
        out_specs=pl.BlockSpec((None, n2, g, c), lambda b, j: (b, 0, j, 0)),
        compiler_params=_cparams(2),
        name="fft_stage2",
    )(w2, x2)
    return y.reshape(bsz, s, c)


def _conv_kernel(uc_ref, up_ref, un_ref, wdw_ref, bdw_ref, lng_ref, lnb_ref, wpw_ref, o_ref, ext_sc, sh_sc,
                 *, tm, rc):
    i = pl.program_id(1)
    last = pl.num_programs(1) - 1
    ext_sc[0:CONV_HALO] = jnp.where(i > 0, up_ref[...].astype(F32), 0.0)
    ext_sc[CONV_HALO:CONV_HALO + tm] = uc_ref[...].astype(F32)
    ext_sc[CONV_HALO + tm:2 * CONV_HALO + tm] = jnp.where(i < last, un_ref[...].astype(F32), 0.0)
    first = CONV_HALO - CONV_KERNEL // 2
    sub = sh_sc.shape[0]
    for r in range(sub):
        sh_sc[r] = ext_sc[r:r + sh_sc.shape[1], :]
    for r0 in range(0, tm, rc):
        acc = jnp.zeros((rc, CONV_WIDTH), F32)
        for k in range(CONV_KERNEL):
            r = (first + k) % sub
            lo = r0 + (first + k) - r
            acc = acc + sh_sc[r, lo:lo + rc, :] * wdw_ref[k:k + 1, :]
        acc = acc + bdw_ref[...]
        mu = jnp.mean(acc, axis=-1, keepdims=True)
        xc = acc - mu
        var = jnp.mean(xc * xc, axis=-1, keepdims=True)
        y = xc * lax.rsqrt(var + EPS) * lng_ref[...] + lnb_ref[...]
        y = y * jax.nn.sigmoid(y)
        o_ref[r0:r0 + rc, :] = _dot(y.astype(BF16), wpw_ref[...]).astype(BF16)


def _conv_module(u, w, tm):
    bsz, s, c = u.shape
    hb = tm // CONV_HALO
    n_halo = s // CONV_HALO
    return pl.pallas_call(
        functools.partial(_conv_kernel, tm=tm, rc=64),
        out_shape=jax.ShapeDtypeStruct((bsz, s, c), BF16),
        grid=(bsz, s // tm),
        in_specs=[
            pl.BlockSpec((None, tm, c), lambda b, i: (b, i, 0)),
            pl.BlockSpec((None, CONV_HALO, c), lambda b, i: (b, jnp.maximum(i * hb - 1, 0), 0)),
            pl.BlockSpec((None, CONV_HALO, c), lambda b, i: (b, jnp.minimum((i + 1) * hb, n_halo - 1), 0)),
            _const_spec((32, c)), _const_spec((1, c)), _const_spec((1, c)), _const_spec((1, c)),
            _const_spec((c, c)),
        ],
        out_specs=pl.BlockSpec((None, tm, c), lambda b, i: (b, i, 0)),
        scratch_shapes=[pltpu.VMEM((tm + 2 * CONV_HALO, c), F32),
                        pltpu.VMEM((8, tm + 2 * CONV_HALO - 8, c), F32)],
        compiler_params=_cparams(2),
        name="conv",
    )(u, u, u, w["w_dw"], w["b_dw"], w["ln_g"], w["ln_b"], w["w_pw2"])


def _mix_mlp_kernel(x_ref, gate1_ref, shift_ref, scale_ref, gate2_ref, yf_ref, at_ref, yc_ref,
                    wf_ref, wof_ref, woa_ref, woc_ref, g2_ref, w1_ref, w2_ref, gf_ref, o_ref,
                    *, ff_chunk, final_norm):
    yf = _dot(yf_ref[...].astype(BF16), wf_ref[...]).astype(BF16)
    mix = _dot(yf, wof_ref[...]) + _dot(at_ref[...], woa_ref[...]) + _dot(yc_ref[...], woc_ref[...])
    x = x_ref[...] + gate1_ref[...] * mix
    h = (_rms(x, g2_ref[...]) * (1.0 + scale_ref[...]) + shift_ref[...]).astype(BF16)
    d_ff = w1_ref.shape[1]
    acc = jnp.zeros(x.shape, F32)
    for c0 in range(0, d_ff, ff_chunk):
        a = jnp.maximum(_dot(h, w1_ref[:, c0:c0 + ff_chunk]), 0.0)
        acc = acc + _dot((a * a).astype(BF16), w2_ref[c0:c0 + ff_chunk, :])
    y = x + gate2_ref[...] * acc
    if final_norm:
        y = _rms(y, gf_ref[...])
    o_ref[...] = y


def _mix_mlp(x, mods, mod_row, yf, attn, yc, w, mlp_w, layer, gf, tm, final_norm):
    bsz, s, d = x.shape
    w_mlp1, w_mlp2 = mlp_w
    d_ff = w_mlp1.shape[2]
    layer_spec = lambda r, c: pl.BlockSpec((None, r, c), lambda b, i: (layer, 0, 0), pipeline_mode=pl.Buffered(1))
    row = lambda j: pl.BlockSpec((None, None, 1, d), lambda b, i: (mod_row(b), j, 0, 0))
    tok = lambda width: pl.BlockSpec((None, tm, width), lambda b, i: (b, i, 0))
    return pl.pallas_call(
        functools.partial(_mix_mlp_kernel, ff_chunk=1024, final_norm=final_norm),
        out_shape=jax.ShapeDtypeStruct((bsz, s, d), F32),
        grid=(bsz, s // tm),
        in_specs=[
            tok(d), row(2), row(3), row(4), row(5),
            tok(F_WIDTH), tok(MLA_HEADS * V_HEAD_DIM), tok(CONV_WIDTH),
            _const_spec((F_WIDTH, F_WIDTH)), _const_spec((F_WIDTH, d)), _const_spec((MLA_HEADS * V_HEAD_DIM, d)),
            _const_spec((CONV_WIDTH, d)),
            _const_spec((1, d)), layer_spec(d, d_ff), layer_spec(d_ff, d), _const_spec((1, d)),
        ],
        out_specs=tok(d),
        compiler_params=_cparams(2),
        name="mix_mlp",
    )(x, mods, mods, mods, mods, yf, attn, yc, w["w_f"], w["wo_f"], w["wo_a"], w["wo_c"],
      w["g2"], w_mlp1, w_mlp2, gf)


def _rope_tables(s):
    t = jnp.arange(s)
    row = (t // GRID_W).astype(F32)
    col = (t % GRID_W).astype(F32)
    inv = ROPE_BASE ** (-jnp.arange(0, ROPE_AXIS_DIM, 2, dtype=F32) / ROPE_AXIS_DIM)
    ang = jnp.concatenate([row[:, None] * inv, col[:, None] * inv], axis=-1)
    cos, sin = jnp.cos(ang), jnp.sin(ang)
    z_lo = jnp.zeros((s, QK_NOPE_DIM), F32)
    z_hi = jnp.zeros((s, HEAD_PAD - QK_NOPE_DIM - QK_ROPE_DIM), F32)
    ta = jnp.concatenate([z_lo, cos, cos, z_hi], axis=-1)
    tb = jnp.concatenate([z_lo, -sin, sin, z_hi], axis=-1)
    return ta, tb


def _identity_rope_tables(s):
    lane = np.arange(HEAD_PAD)
    ta = ((lane >= QK_NOPE_DIM) & (lane < QK_NOPE_DIM + QK_ROPE_DIM)).astype(np.float32)
    return jnp.asarray(np.tile(ta, (s, 1))), jnp.zeros((s, HEAD_PAD), F32)


def _fourier_consts(s, n_ctx):
    ar = lambda n: jnp.arange(n, dtype=jnp.int32)
    cc, sc = _dft_cos_sin(ar(F_GROUP_DIM), ar(F_GROUP_DIM), F_GROUP_DIM)
    eye = jnp.eye(F_GROUPS, dtype=F32)
    dft_c = jnp.concatenate([jnp.kron(eye, cc), -jnp.kron(eye, sc)], axis=1).astype(BF16)
    consts = {"dft_c": dft_c}

    def direct(n):
        c, sn = _dft_cos_sin(ar(n), ar(n), n)
        return jnp.concatenate([c, sn], axis=1).astype(BF16)

    def two_stage(n):
        n1, n2 = n // FFT_N2, FFT_N2
        pos = ar(n1)[None, :] * n2 + ar(n2)[:, None]
        r = (ar(n1)[None, :, None] * pos[:, None, :]) % n
        ang = r.astype(F32) * (2.0 * math.pi / n)
        c, sn = jnp.cos(ang), jnp.sin(ang)
        m1 = jnp.concatenate([jnp.concatenate([c, sn], axis=2),
                              jnp.concatenate([-sn, c], axis=2)], axis=1).astype(BF16)
        return m1, direct(n2)

    for n, tag in ((s, "x"), (n_ctx, "ctx")):
        if n <= 2 * FFT_N2:
            consts[tag] = {"w_direct": direct(n)}
        else:
            m1, w2 = two_stage(n)
            consts[tag] = {"m1": m1, "w2": w2}
    return consts


def _pad_heads(w, width):
    k = w.shape[0]
    w = w.reshape(k, MLA_HEADS, width)
    return jnp.pad(w, ((0, 0), (0, 0), (0, HEAD_PAD - width))).reshape(k, HP)


def _layer_weights(l, p, dft_c):
    d = p["w_in"].shape[1]
    w_in = p["w_in"][l]
    s2 = F_WIDTH + Q_LORA_RANK + KV_LORA_RANK
    s3 = s2 + QK_ROPE_DIM
    half = QK_ROPE_DIM // 2
    w_kr = w_in[:, s2:s3]
    w_kr_sw = jnp.concatenate([w_kr[:, half:], w_kr[:, :half]], axis=1)
    lo = jnp.zeros((d, QK_NOPE_DIM), F32)
    hi = jnp.zeros((d, HEAD_PAD - QK_NOPE_DIM - QK_ROPE_DIM), F32)
    w1 = jnp.concatenate([w_in[:, :s2], lo, w_kr, hi, lo, w_kr_sw, hi, w_in[:, s3:]], axis=1).astype(BF16)

    qd = QK_NOPE_DIM + QK_ROPE_DIM
    w_uq = p["w_uq"][l].reshape(Q_LORA_RANK, MLA_HEADS, qd)
    rope = w_uq[:, :, QK_NOPE_DIM:]
    rope_sw = jnp.concatenate([rope[:, :, half:], rope[:, :, :half]], axis=2)
    wqa = _pad_heads(w_uq.reshape(Q_LORA_RANK, MLA_HEADS * qd), qd).astype(BF16)
    wqb = jnp.pad(rope_sw, ((0, 0), (0, 0), (QK_NOPE_DIM, HEAD_PAD - qd))).reshape(Q_LORA_RANK, HP).astype(BF16)

    w_ukv = p["w_ukv"][l].reshape(KV_LORA_RANK, MLA_HEADS, QK_NOPE_DIM + V_HEAD_DIM)
    wk = _pad_heads(w_ukv[:, :, :QK_NOPE_DIM].reshape(KV_LORA_RANK, -1), QK_NOPE_DIM).astype(BF16)
    wv = jnp.pad(w_ukv[:, :, QK_NOPE_DIM:], ((0, 0), (0, 0), (0, V_ROWS - V_HEAD_DIM)))
    wv = wv.reshape(KV_LORA_RANK, MLA_HEADS * V_ROWS).T.astype(BF16)
    vone = np.zeros((MLA_HEADS * V_ROWS, 1), np.float32)
    vone[ONES_ROW::V_ROWS, 0] = 1.0

    w_o = p["w_o"][l]
    a0, a1 = F_WIDTH, F_WIDTH + MLA_HEADS * V_HEAD_DIM
    w_dw = jnp.pad(p["w_dw"][l], ((0, 32 - CONV_KERNEL), (0, 0)))
    r2 = lambda v: v.reshape(1, -1)
    return {
        "w1": w1, "qg": r2(p["q_norm_g"][l]), "wqa": wqa, "wqb": wqb,
        "kvg": r2(p["kv_norm_g"][l]), "wk": wk, "wv": wv, "dft_c": dft_c, "vone": jnp.asarray(vone),
        "w_dw": w_dw, "b_dw": r2(p["b_dw"][l]), "ln_g": r2(p["conv_ln_g"][l]), "ln_b": r2(p["conv_ln_b"][l]),
        "w_pw2": p["w_pw2"][l].astype(BF16),
        "w_f": p["w_fourier"][l].astype(BF16), "wo_f": w_o[:a0].astype(BF16), "wo_a": w_o[a0:a1].astype(BF16),
        "wo_c": w_o[a1:].astype(BF16),
        "g1": r2(p["norm1_g"][l]), "g2": r2(p["norm2_g"][l]),
    }


def _token_tile(s):
    for tm in (512, 256, 128):
        if s % tm == 0:
            return tm
    raise ValueError(f"sequence length {s} must be a multiple of 128")


def kernel(x, c, ctx, c_ctx, w_mod, b_mod, norm1_g, w_in, q_norm_g, w_uq, kv_norm_g, w_ukv, w_fourier,
           w_dw, b_dw, conv_ln_g, conv_ln_b, w_pw2, w_o, norm2_g, w_mlp1, w_mlp2, final_norm_g):
    bsz, s, d = x.shape
    n_ctx = ctx.shape[1]
    depth = w_mod.shape[0]
    assert bsz < 8 and s % GRID_W == 0 and s % LANES == 0 and n_ctx % LANES == 0
    p = dict(w_in=w_in, w_uq=w_uq, w_ukv=w_ukv, q_norm_g=q_norm_g, kv_norm_g=kv_norm_g,
             w_fourier=w_fourier, w_dw=w_dw, b_dw=b_dw, conv_ln_g=conv_ln_g, conv_ln_b=conv_ln_b,
             w_pw2=w_pw2, w_o=w_o, norm1_g=norm1_g, norm2_g=norm2_g)
    mlp_w = (w_mlp1.astype(BF16), w_mlp2.astype(BF16))

    c_rows = jnp.zeros((8, d), F32).at[:bsz].set(c).at[bsz].set(c_ctx)
    mods_all = _modulation(c_rows, w_mod, b_mod).reshape(depth, 8, N_MOD, 1, d)
    x_row = lambda b: b
    ctx_row = lambda b: bsz

    tm_x, tm_c = _token_tile(s), _token_tile(n_ctx)
    tq = next(t for t in (ATTN_QUERY_TILE, tm_x) if s % t == 0)
    tm_mlp = next(t for t in (MLP_TILE, tm_x) if s % t == 0)
    tk = next(t for t in (ATTN_KEY_TILE, tm_x) if s % t == 0 and t % tm_x == 0)
    ta_x, tb_x = _rope_tables(s)
    ta_c, tb_c = _identity_rope_tables(n_ctx)
    fc = _fourier_consts(s, n_ctx)
    gf = final_norm_g.reshape(1, d)

    for l in range(depth):
        last = l == depth - 1
        w = _layer_weights(l, p, fc["dft_c"])
        mods = mods_all[l]

        qx, kx, vx, zx, ux = _prep(x, mods, x_row, w["g1"], w, ta_x, tb_x, tm_x)
        qc, kc, vc, zc, uc = _prep(ctx, mods, ctx_row, w["g1"], w, ta_c, tb_c, tm_c)

        attn_x = _attention(qx, kc, vc, kx, vx, tq, tk)
        yf_x = _fourier_seq(zx, fc["x"])
        yc_x = _conv_module(ux, w, tm_x)
        x = _mix_mlp(x, mods, x_row, yf_x, attn_x, yc_x, w, mlp_w, l, gf, tm_mlp, final_norm=last)

        if not last:
            attn_c = _attention(qc, kc, vc, None, None, tm_c, tk)
            yf_c = _fourier_seq(zc, fc["ctx"])
            yc_c = _conv_module(uc, w, tm_c)
            ctx = _mix_mlp(ctx, mods, ctx_row, yf_c, attn_c, yc_c, w, mlp_w, l, gf, tm_c, final_norm=False)
    return x
```

```python
import functools
import math

import numpy as np
import jax
import jax.numpy as jnp
from jax import lax
from jax.experimental import pallas as pl
from jax.experimental.pallas import tpu as pltpu

F32 = jnp.float32
BF16 = jnp.bfloat16

GRID_W = 64
F_GROUPS = 4
F_GROUP_DIM = 64
F_WIDTH = F_GROUPS * F_GROUP_DIM
MLA_HEADS = 8
QK_NOPE_DIM = 64
QK_ROPE_DIM = 32
V_HEAD_DIM = 64
Q_LORA_RANK = 256
KV_LORA_RANK = 128
CONV_WIDTH = 256
CONV_KERNEL = 31
N_MOD = 6
ROPE_BASE = 10000.0
ROPE_AXIS_DIM = QK_ROPE_DIM // 2
ATTN_SCALE = (QK_NOPE_DIM + QK_ROPE_DIM) ** -0.5
EPS = 1e-6

LANES = 128
HEAD_PAD = LANES
HP = MLA_HEADS * HEAD_PAD
ONES_ROW = V_HEAD_DIM
HEADS_PER_STEP = 2
V_ROWS = 80
CONV_HALO = 16
FFT_N2 = 128
VMEM_LIMIT = 56 * 1024 * 1024
TILES_PER_PASS = 4
LAZY_HEADROOM = 60.0
ATTN_KEY_TILE = 512
ATTN_QUERY_TILE = 4096
MLP_TILE = 1024
Q_SCALE = ATTN_SCALE * math.log2(math.e)


def _cparams(n_axes, flags=None):
    return pltpu.CompilerParams(dimension_semantics=("parallel",) * n_axes,
                                vmem_limit_bytes=VMEM_LIMIT, flags=flags)


def _const_spec(shape):
    nd = len(shape)
    return pl.BlockSpec(shape, lambda *_: (0,) * nd, pipeline_mode=pl.Buffered(1))


def _dot(a, b):
    return jnp.dot(a, b, preferred_element_type=F32)


def _split_bf16(v):
    hi = v.astype(BF16)
    lo = (v - hi.astype(F32)).astype(BF16)
    return hi, lo


def _mod_kernel(c_ref, w_ref, b_ref, o_ref):
    c = c_ref[...]
    s = c * jax.nn.sigmoid(c)
    s_hi, s_lo = _split_bf16(s)
    w_hi, w_lo = _split_bf16(w_ref[...])
    o_ref[...] = _dot(s_hi, w_hi) + _dot(s_hi, w_lo) + _dot(s_lo, w_hi) + b_ref[...]


def _modulation(c_rows, w_mod, b_mod):
    n_layers, d, n6 = w_mod.shape
    tn = 1536
    return pl.pallas_call(
        _mod_kernel,
        out_shape=jax.ShapeDtypeStruct((n_layers, 8, n6), F32),
        grid=(n_layers, n6 // tn),
        in_specs=[
            pl.BlockSpec((8, d), lambda l, j: (0, 0)),
            pl.BlockSpec((None, d, tn), lambda l, j: (l, 0, j)),
            pl.BlockSpec((None, 1, tn), lambda l, j: (l, 0, j)),
        ],
        out_specs=pl.BlockSpec((None, 8, tn), lambda l, j: (l, 0, j)),
        compiler_params=_cparams(2),
        name="modulation",
    )(c_rows, w_mod, b_mod.reshape(n_layers, 1, n6))


def _rms(v, g):
    return v * lax.rsqrt(jnp.mean(v * v, axis=-1, keepdims=True) + EPS) * g


def _prep_kernel(x_ref, shift_ref, scale_ref, g1_ref, w1_ref, qg_ref, wqa_ref, wqb_ref,
                 kvg_ref, wk_ref, wv_ref, dft_ref, ta_ref, tb_ref, vone_ref,
                 q_ref, k_ref, v_ref, z_ref, u_ref):
    x = x_ref[...]
    h = (_rms(x, g1_ref[...]) * (1.0 + scale_ref[...]) + shift_ref[...]).astype(BF16)
    z = _dot(h, w1_ref[...])

    zz = _dot(z[:, 0:F_WIDTH].astype(BF16), dft_ref[...])
    z_ref[0] = zz[:, :F_WIDTH]
    z_ref[1] = zz[:, F_WIDTH:]

    ta = ta_ref[...]
    tb = tb_ref[...]
    lane = lax.broadcasted_iota(jnp.int32, (1, HEAD_PAD), 1)
    nope = (lane < QK_NOPE_DIM).astype(F32)

    c0 = F_WIDTH
    cqn = _rms(z[:, c0:c0 + Q_LORA_RANK], qg_ref[...]).astype(BF16)
    qa = _dot(cqn, wqa_ref[...])
    qb = _dot(cqn, wqb_ref[...])
    qta = (ta + nope) * Q_SCALE
    qtb = tb * Q_SCALE
    for hd in range(MLA_HEADS):
        sl = slice(hd * HEAD_PAD, (hd + 1) * HEAD_PAD)
        q_ref[:, sl] = (qa[:, sl] * qta + qb[:, sl] * qtb).astype(BF16)

    c1 = c0 + Q_LORA_RANK
    ckvn = _rms(z[:, c1:c1 + KV_LORA_RANK], kvg_ref[...]).astype(BF16)
    kn = _dot(ckvn, wk_ref[...])
    c2 = c1 + KV_LORA_RANK
    kr = z[:, c2:c2 + HEAD_PAD] * ta + z[:, c2 + HEAD_PAD:c2 + 2 * HEAD_PAD] * tb
    for hd in range(MLA_HEADS):
        sl = slice(hd * HEAD_PAD, (hd + 1) * HEAD_PAD)
        k_ref[:, sl] = (kn[:, sl] + kr).astype(BF16)
    vt = lax.dot_general(wv_ref[...], ckvn, (((1,), (1,)), ((), ())), preferred_element_type=F32)
    vt = vt + vone_ref[...]
    for hd in range(MLA_HEADS):
        v_ref[hd] = vt[hd * V_ROWS:(hd + 1) * V_ROWS, :].astype(BF16)

    c3 = c2 + 2 * HEAD_PAD
    a = z[:, c3:c3 + CONV_WIDTH]
    g = z[:, c3 + CONV_WIDTH:c3 + 2 * CONV_WIDTH]
    u_ref[...] = (a * jax.nn.sigmoid(g)).astype(BF16)


def _prep(x, mods, mod_row, g1, w, ta, tb, tm):
    bsz, s, d = x.shape
    n1 = w["w1"].shape[1]
    row = lambda j: pl.BlockSpec((None, None, 1, d), lambda b, i: (mod_row(b), j, 0, 0))
    tok = lambda width: pl.BlockSpec((None, tm, width), lambda b, i: (b, i, 0))
    return pl.pallas_call(
        _prep_kernel,
        out_shape=(
            jax.ShapeDtypeStruct((bsz, s, HP), BF16),
            jax.ShapeDtypeStruct((bsz, s, HP), BF16),
            jax.ShapeDtypeStruct((bsz, MLA_HEADS, s // tm, V_ROWS, tm), BF16),
            jax.ShapeDtypeStruct((bsz, 2, s, F_WIDTH), F32),
            jax.ShapeDtypeStruct((bsz, s, CONV_WIDTH), BF16),
        ),
        grid=(bsz, s // tm),
        in_specs=[
            tok(d), row(0), row(1), _const_spec((1, d)),
            _const_spec((d, n1)),
            _const_spec((1, Q_LORA_RANK)), _const_spec((Q_LORA_RANK, HP)), _const_spec((Q_LORA_RANK, HP)),
            _const_spec((1, KV_LORA_RANK)), _const_spec((KV_LORA_RANK, HP)),
            _const_spec((MLA_HEADS * V_ROWS, KV_LORA_RANK)),
            _const_spec((F_WIDTH, 2 * F_WIDTH)),
            pl.BlockSpec((tm, HEAD_PAD), lambda b, i: (i, 0)),
            pl.BlockSpec((tm, HEAD_PAD), lambda b, i: (i, 0)),
            _const_spec((MLA_HEADS * V_ROWS, 1)),
        ],
        out_specs=(
            tok(HP), tok(HP),
            pl.BlockSpec((None, MLA_HEADS, None, V_ROWS, tm), lambda b, i: (b, 0, i, 0, 0)),
            pl.BlockSpec((None, 2, tm, F_WIDTH), lambda b, i: (b, 0, i, 0)),
            tok(CONV_WIDTH),
        ),
        compiler_params=_cparams(2),
        name="prep",
    )(x, mods, mods, g1, w["w1"], w["qg"], w["wqa"], w["wqb"], w["kvg"], w["wk"], w["wv"],
      w["dft_c"], ta, tb, w["vone"])


def _attn_kernel(*refs, tk, n_x_blocks):
    if n_x_blocks:
        q_ref, kc_ref, vc_ref, kx_ref, vx_ref, o_ref, m_sc, acc_sc, out_sc, ex_sc = refs
    else:
        q_ref, kc_ref, vc_ref, o_ref, m_sc, acc_sc, out_sc = refs
    for hh in range(HEADS_PER_STEP):
        lanes = slice(hh * HEAD_PAD, (hh + 1) * HEAD_PAD)
        qt = q_ref[:, lanes].astype(F32).T.astype(BF16)

        def exact_tile(k, vt, first=False):
            s = _dot(k, qt)
            m_tile = jnp.max(s, axis=0, keepdims=True)
            m_new = m_tile if first else jnp.maximum(m_sc[...], m_tile)
            p = jnp.exp2(s - m_new).astype(BF16)
            pv = _dot(vt, p)
            acc_sc[...] = pv if first else jnp.exp2(m_sc[...] - m_new) * acc_sc[...] + pv
            m_sc[...] = m_new

        exact_tile(kc_ref[:, lanes], vc_ref[hh, 0], first=True)
        if n_x_blocks:
            n = n_x_blocks
            v_sub = tk // vx_ref.shape[3]

            def k_tile(t):
                return kx_ref[pl.ds(pl.multiple_of(t * tk, tk), tk), lanes]

            def v_tile(t):
                return jnp.concatenate([vx_ref[hh, v_sub * t + i] for i in range(v_sub)], axis=1)

            def lazy_tile(t):
                s = _dot(k_tile(t), qt)
                m_old = m_sc[...]
                p = jnp.exp2(s - m_old).astype(BF16)
                m_tile = jnp.max(s, axis=0, keepdims=True)
                m_new = jnp.maximum(m_old, m_tile)
                acc_sc[...] = (acc_sc[...] + _dot(v_tile(t), p)) * jnp.exp2(m_old - m_new)
                m_sc[...] = m_new
                ex_sc[...] = jnp.maximum(ex_sc[...], m_tile - m_old)

            ex_sc[...] = jnp.zeros(ex_sc.shape, F32)
            n_peel = n % TILES_PER_PASS
            for t in range(n_peel):
                lazy_tile(t)

            def body(j, carry):
                t0 = n_peel + TILES_PER_PASS * j
                for u in range(TILES_PER_PASS):
                    lazy_tile(t0 + u)
                return carry
            lax.fori_loop(0, (n - n_peel) // TILES_PER_PASS, body, 0)

            @pl.when(jnp.max(ex_sc[...]) > LAZY_HEADROOM)
            def _():
                exact_tile(kc_ref[:, lanes], vc_ref[hh, 0], first=True)

                def exact_body(t, carry):
                    exact_tile(k_tile(t), v_tile(t))
                    return carry
                lax.fori_loop(0, n, exact_body, 0)

        acc = acc_sc[...]
        out_sc[hh * V_HEAD_DIM:(hh + 1) * V_HEAD_DIM, :] = acc[:V_HEAD_DIM] / acc[ONES_ROW:ONES_ROW + 1, :]
    o_ref[...] = out_sc[...].T.astype(BF16)


def _attention(q, kc, vc, kx, vx, tq, tk):
    bsz, sq, _ = q.shape
    n_ctx = kc.shape[1]
    hw = HEADS_PER_STEP * HEAD_PAD
    qspec = pl.BlockSpec((None, tq, hw), lambda b, h, i: (b, i, h))
    kspec = lambda n: pl.BlockSpec((None, n, hw), lambda b, h, i: (b, 0, h))
    vspec = lambda v: pl.BlockSpec((None, HEADS_PER_STEP) + v.shape[2:], lambda b, h, i: (b, h, 0, 0, 0))
    assert vc.shape[2:] == (1, V_ROWS, n_ctx)
    args = [q, kc, vc]
    in_specs = [qspec, kspec(n_ctx), vspec(vc)]
    n_x_blocks = 0
    out_w = HEADS_PER_STEP * V_HEAD_DIM
    scratch = [pltpu.VMEM((1, tq), F32), pltpu.VMEM((V_ROWS, tq), F32), pltpu.VMEM((out_w, tq), F32)]
    if kx is not None:
        s = kx.shape[1]
        n_x_blocks = s // tk
        assert vx.shape[3] == V_ROWS and tk % vx.shape[4] == 0 and vx.shape[2] * vx.shape[4] == s
        args += [kx, vx]
        in_specs += [kspec(s), vspec(vx)]
        scratch += [pltpu.VMEM((1, tq), F32)]
    return pl.pallas_call(
        functools.partial(_attn_kernel, tk=tk, n_x_blocks=n_x_blocks),
        out_shape=jax.ShapeDtypeStruct((bsz, sq, MLA_HEADS * V_HEAD_DIM), BF16),
        grid=(bsz, MLA_HEADS // HEADS_PER_STEP, sq // tq),
        in_specs=in_specs,
        out_specs=pl.BlockSpec((None, tq, out_w), lambda b, h, i: (b, i, h)),
        scratch_shapes=scratch,
        compiler_params=_cparams(3),
        name="attn_x" if kx is not None else "attn_ctx",
    )(*args)


def _fft1_kernel(z_ref, m_ref, o_ref, *, g, n1):
    for j in range(g):
        zz = jnp.concatenate([z_ref[0, :, j, :], z_ref[1, :, j, :]], axis=0).astype(BF16)
        r = _dot(m_ref[j], zz)
        o_ref[0, j] = r[:n1]
        o_ref[1, j] = r[n1:]


def _fft2_kernel(w_ref, x_ref, o_ref, *, g, scale):
    for j in range(g):
        xx = jnp.concatenate([x_ref[0, :, j, :], x_ref[1, :, j, :]], axis=0).astype(BF16)
        o_ref[:, j, :] = _dot(w_ref[...], xx) * scale


def _dft_direct_kernel(w_ref, x_ref, o_ref, *, scale):
    o_ref[...] = _dot(w_ref[...], x_ref[...].astype(BF16)) * scale


def _dft_cos_sin(rows, cols, period):
    r = (rows[:, None] * cols[None, :]) % period
    ang = r.astype(F32) * (2.0 * math.pi / period)
    return jnp.cos(ang), jnp.sin(ang)


def _fourier_seq(z, consts):
    bsz, _, s, c = z.shape
    scale = 1.0 / math.sqrt(s * F_GROUP_DIM)
    if s <= 2 * FFT_N2:
        w = consts["w_direct"]
        return pl.pallas_call(
            functools.partial(_dft_direct_kernel, scale=scale),
            out_shape=jax.ShapeDtypeStruct((bsz, s, c), F32),
            grid=(bsz,),
            in_specs=[_const_spec(w.shape), pl.BlockSpec((None, 2 * s, c), lambda b: (b, 0, 0))],
            out_specs=pl.BlockSpec((None, s, c), lambda b: (b, 0, 0)),
            compiler_params=_cparams(1),
            name="dft_direct",
        )(w, z.reshape(bsz, 2 * s, c))
    n1, n2 = s // FFT_N2, FFT_N2
    g = 8
    x2 = pl.pallas_call(
        functools.partial(_fft1_kernel, g=g, n1=n1),
        out_shape=jax.ShapeDtypeStruct((bsz, 2, n2, n1, c), F32),
        grid=(bsz, n2 // g),
        in_specs=[
            pl.BlockSpec((None, 2, n1, g, c), lambda b, j: (b, 0, 0, j, 0)),
            pl.BlockSpec((g, 2 * n1, 2 * n1), lambda b, j: (j, 0, 0)),
        ],
        out_specs=pl.BlockSpec((None, 2, g, n1, c), lambda b, j: (b, 0, j, 0, 0)),
        compiler_params=_cparams(2),
        name="fft_stage1",
    )(z.reshape(bsz, 2, n1, n2, c), consts["m1"])
    w2 = consts["w2"]
    y = pl.pallas_call(
        functools.partial(_fft2_kernel, g=g, scale=scale),
        out_shape=jax.ShapeDtypeStruct((bsz, n2, n1, c), F32),
        grid=(bsz, n1 // g),
        in_specs=[
            _const_spec(w2.shape),
            pl.BlockSpec((None, 2, n2, g, c), lambda b, j: (b, 0, 0, j, 0)),
        ],
        out_specs=pl.BlockSpec((None, n2, g, c), lambda b, j: (b, 0, j, 0)),
        compiler_params=_cparams(2),
        name="fft_stage2",
    )(w2, x2)
    return y.reshape(bsz, s, c)


def _conv_kernel(uc_ref, up_ref, un_ref, wdw_ref, bdw_ref, lng_ref, lnb_ref, wpw_ref, o_ref, ext_sc, sh_sc,
                 *, tm, rc):
    i = pl.program_id(1)
    last = pl.num_programs(1) - 1
    ext_sc[0:CONV_HALO] = jnp.where(i > 0, up_ref[...].astype(F32), 0.0)
    ext_sc[CONV_HALO:CONV_HALO + tm] = uc_ref[...].astype(F32)
    ext_sc[CONV_HALO + tm:2 * CONV_HALO + tm] = jnp.where(i < last, un_ref[...].astype(F32), 0.0)
    first = CONV_HALO - CONV_KERNEL // 2
    sub = sh_sc.shape[0]
    for r in range(sub):
        sh_sc[r] = ext_sc[r:r + sh_sc.shape[1], :]
    for r0 in range(0, tm, rc):
        acc = jnp.zeros((rc, CONV_WIDTH), F32)
        for k in range(CONV_KERNEL):
            r = (first + k) % sub
            lo = r0 + (first + k) - r
            acc = acc + sh_sc[r, lo:lo + rc, :] * wdw_ref[k:k + 1, :]
        acc = acc + bdw_ref[...]
        mu = jnp.mean(acc, axis=-1, keepdims=True)
        xc = acc - mu
        var = jnp.mean(xc * xc, axis=-1, keepdims=True)
        y = xc * lax.rsqrt(var + EPS) * lng_ref[...] + lnb_ref[...]
        y = y * jax.nn.sigmoid(y)
        o_ref[r0:r0 + rc, :] = _dot(y.astype(BF16), wpw_ref[...]).astype(BF16)


def _conv_module(u, w, tm):
    bsz, s, c = u.shape
    hb = tm // CONV_HALO
    n_halo = s // CONV_HALO
    return pl.pallas_call(
        functools.partial(_conv_kernel, tm=tm, rc=64),
        out_shape=jax.ShapeDtypeStruct((bsz, s, c), BF16),
        grid=(bsz, s // tm),
        in_specs=[
            pl.BlockSpec((None, tm, c), lambda b, i: (b, i, 0)),
            pl.BlockSpec((None, CONV_HALO, c), lambda b, i: (b, jnp.maximum(i * hb - 1, 0), 0)),
            pl.BlockSpec((None, CONV_HALO, c), lambda b, i: (b, jnp.minimum((i + 1) * hb, n_halo - 1), 0)),
            _const_spec((32, c)), _const_spec((1, c)), _const_spec((1, c)), _const_spec((1, c)),
            _const_spec((c, c)),
        ],
        out_specs=pl.BlockSpec((None, tm, c), lambda b, i: (b, i, 0)),
        scratch_shapes=[pltpu.VMEM((tm + 2 * CONV_HALO, c), F32),
                        pltpu.VMEM((8, tm + 2 * CONV_HALO - 8, c), F32)],
        compiler_params=_cparams(2),
        name="conv",
    )(u, u, u, w["w_dw"], w["b_dw"], w["ln_g"], w["ln_b"], w["w_pw2"])


def _mix_mlp_kernel(x_ref, gate1_ref, shift_ref, scale_ref, gate2_ref, yf_ref, at_ref, yc_ref,
                    wf_ref, wof_ref, woa_ref, woc_ref, g2_ref, w1_ref, w2_ref, gf_ref, o_ref,
                    *, ff_chunk, final_norm):
    yf = _dot(yf_ref[...].astype(BF16), wf_ref[...]).astype(BF16)
    mix = _dot(yf, wof_ref[...]) + _dot(at_ref[...], woa_ref[...]) + _dot(yc_ref[...], woc_ref[...])
    x = x_ref[...] + gate1_ref[...] * mix
    h = (_rms(x, g2_ref[...]) * (1.0 + scale_ref[...]) + shift_ref[...]).astype(BF16)
    d_ff = w1_ref.shape[1]
    acc = jnp.zeros(x.shape, F32)
    for c0 in range(0, d_ff, ff_chunk):
        a = jnp.maximum(_dot(h, w1_ref[:, c0:c0 + ff_chunk]), 0.0)
        acc = acc + _dot((a * a).astype(BF16), w2_ref[c0:c0 + ff_chunk, :])
    y = x + gate2_ref[...] * acc
    if final_norm:
        y = _rms(y, gf_ref[...])
    o_ref[...] = y


def _mix_mlp(x, mods, mod_row, yf, attn, yc, w, mlp_w, layer, gf, tm, final_norm):
    bsz, s, d = x.shape
    w_mlp1, w_mlp2 = mlp_w
    d_ff = w_mlp1.shape[2]
    layer_spec = lambda r, c: pl.BlockSpec((None, r, c), lambda b, i: (layer, 0, 0), pipeline_mode=pl.Buffered(1))
    row = lambda j: pl.BlockSpec((None, None, 1, d), lambda b, i: (mod_row(b), j, 0, 0))
    tok = lambda width: pl.BlockSpec((None, tm, width), lambda b, i: (b, i, 0))
    return pl.pallas_call(
        functools.partial(_mix_mlp_kernel, ff_chunk=1024, final_norm=final_norm),
        out_shape=jax.ShapeDtypeStruct((bsz, s, d), F32),
        grid=(bsz, s // tm),
        in_specs=[
            tok(d), row(2), row(3), row(4), row(5),
            tok(F_WIDTH), tok(MLA_HEADS * V_HEAD_DIM), tok(CONV_WIDTH),
            _const_spec((F_WIDTH, F_WIDTH)), _const_spec((F_WIDTH, d)), _const_spec((MLA_HEADS * V_HEAD_DIM, d)),
            _const_spec((CONV_WIDTH, d)),
            _const_spec((1, d)), layer_spec(d, d_ff), layer_spec(d_ff, d), _const_spec((1, d)),
        ],
        out_specs=tok(d),
        compiler_params=_cparams(2),
        name="mix_mlp",
    )(x, mods, mods, mods, mods, yf, attn, yc, w["w_f"], w["wo_f"], w["wo_a"], w["wo_c"],
      w["g2"], w_mlp1, w_mlp2, gf)


def _rope_tables(s):
    t = jnp.arange(s)
    row = (t // GRID_W).astype(F32)
    col = (t % GRID_W).astype(F32)
    inv = ROPE_BASE ** (-jnp.arange(0, ROPE_AXIS_DIM, 2, dtype=F32) / ROPE_AXIS_DIM)
    ang = jnp.concatenate([row[:, None] * inv, col[:, None] * inv], axis=-1)
    cos, sin = jnp.cos(ang), jnp.sin(ang)
    z_lo = jnp.zeros((s, QK_NOPE_DIM), F32)
    z_hi = jnp.zeros((s, HEAD_PAD - QK_NOPE_DIM - QK_ROPE_DIM), F32)
    ta = jnp.concatenate([z_lo, cos, cos, z_hi], axis=-1)
    tb = jnp.concatenate([z_lo, -sin, sin, z_hi], axis=-1)
    return ta, tb


def _identity_rope_tables(s):
    lane = np.arange(HEAD_PAD)
    ta = ((lane >= QK_NOPE_DIM) & (lane < QK_NOPE_DIM + QK_ROPE_DIM)).astype(np.float32)
    return jnp.asarray(np.tile(ta, (s, 1))), jnp.zeros((s, HEAD_PAD), F32)


def _fourier_consts(s, n_ctx):
    ar = lambda n: jnp.arange(n, dtype=jnp.int32)
    cc, sc = _dft_cos_sin(ar(F_GROUP_DIM), ar(F_GROUP_DIM), F_GROUP_DIM)
    eye = jnp.eye(F_GROUPS, dtype=F32)
    dft_c = jnp.concatenate([jnp.kron(eye, cc), -jnp.kron(eye, sc)], axis=1).astype(BF16)
    consts = {"dft_c": dft_c}

    def direct(n):
        c, sn = _dft_cos_sin(ar(n), ar(n), n)
        return jnp.concatenate([c, sn], axis=1).astype(BF16)

    def two_stage(n):
        n1, n2 = n // FFT_N2, FFT_N2
        pos = ar(n1)[None, :] * n2 + ar(n2)[:, None]
        r = (ar(n1)[None, :, None] * pos[:, None, :]) % n
        ang = r.astype(F32) * (2.0 * math.pi / n)
        c, sn = jnp.cos(ang), jnp.sin(ang)
        m1 = jnp.concatenate([jnp.concatenate([c, sn], axis=2),
                              jnp.concatenate([-sn, c], axis=2)], axis=1).astype(BF16)
        return m1, direct(n2)

    for n, tag in ((s, "x"), (n_ctx, "ctx")):
        if n <= 2 * FFT_N2:
            consts[tag] = {"w_direct": direct(n)}
        else:
            m1, w2 = two_stage(n)
            consts[tag] = {"m1": m1, "w2": w2}
    return consts


def _pad_heads(w, width):
    k = w.shape[0]
    w = w.reshape(k, MLA_HEADS, width)
    return jnp.pad(w, ((0, 0), (0, 0), (0, HEAD_PAD - width))).reshape(k, HP)


def _layer_weights(l, p, dft_c):
    d = p["w_in"].shape[1]
    w_in = p["w_in"][l]
    s2 = F_WIDTH + Q_LORA_RANK + KV_LORA_RANK
    s3 = s2 + QK_ROPE_DIM
    half = QK_ROPE_DIM // 2
    w_kr = w_in[:, s2:s3]
    w_kr_sw = jnp.concatenate([w_kr[:, half:], w_kr[:, :half]], axis=1)
    lo = jnp.zeros((d, QK_NOPE_DIM), F32)
    hi = jnp.zeros((d, HEAD_PAD - QK_NOPE_DIM - QK_ROPE_DIM), F32)
    w1 = jnp.concatenate([w_in[:, :s2], lo, w_kr, hi, lo, w_kr_sw, hi, w_in[:, s3:]], axis=1).astype(BF16)

    qd = QK_NOPE_DIM + QK_ROPE_DIM
    w_uq = p["w_uq"][l].reshape(Q_LORA_RANK, MLA_HEADS, qd)
    rope = w_uq[:, :, QK_NOPE_DIM:]
    rope_sw = jnp.concatenate([rope[:, :, half:], rope[:, :, :half]], axis=2)
    wqa = _pad_heads(w_uq.reshape(Q_LORA_RANK, MLA_HEADS * qd), qd).astype(BF16)
    wqb = jnp.pad(rope_sw, ((0, 0), (0, 0), (QK_NOPE_DIM, HEAD_PAD - qd))).reshape(Q_LORA_RANK, HP).astype(BF16)

    w_ukv = p["w_ukv"][l].reshape(KV_LORA_RANK, MLA_HEADS, QK_NOPE_DIM + V_HEAD_DIM)
    wk = _pad_heads(w_ukv[:, :, :QK_NOPE_DIM].reshape(KV_LORA_RANK, -1), QK_NOPE_DIM).astype(BF16)
    wv = jnp.pad(w_ukv[:, :, QK_NOPE_DIM:], ((0, 0), (0, 0), (0, V_ROWS - V_HEAD_DIM)))
    wv = wv.reshape(KV_LORA_RANK, MLA_HEADS * V_ROWS).T.astype(BF16)
    vone = np.zeros((MLA_HEADS * V_ROWS, 1), np.float32)
    vone[ONES_ROW::V_ROWS, 0] = 1.0

    w_o = p["w_o"][l]
    a0, a1 = F_WIDTH, F_WIDTH + MLA_HEADS * V_HEAD_DIM
    w_dw = jnp.pad(p["w_dw"][l], ((0, 32 - CONV_KERNEL), (0, 0)))
    r2 = lambda v: v.reshape(1, -1)
    return {
        "w1": w1, "qg": r2(p["q_norm_g"][l]), "wqa": wqa, "wqb": wqb,
        "kvg": r2(p["kv_norm_g"][l]), "wk": wk, "wv": wv, "dft_c": dft_c, "vone": jnp.asarray(vone),
        "w_dw": w_dw, "b_dw": r2(p["b_dw"][l]), "ln_g": r2(p["conv_ln_g"][l]), "ln_b": r2(p["conv_ln_b"][l]),
        "w_pw2": p["w_pw2"][l].astype(BF16),
        "w_f": p["w_fourier"][l].astype(BF16), "wo_f": w_o[:a0].astype(BF16), "wo_a": w_o[a0:a1].astype(BF16),
        "wo_c": w_o[a1:].astype(BF16),
        "g1": r2(p["norm1_g"][l]), "g2": r2(p["norm2_g"][l]),
    }


def _token_tile(s):
    for tm in (512, 256, 128):
        if s % tm == 0:
            return tm
    raise ValueError(f"sequence length {s} must be a multiple of 128")


def kernel(x, c, ctx, c_ctx, w_mod, b_mod, norm1_g, w_in, q_norm_g, w_uq, kv_norm_g, w_ukv, w_fourier,
           w_dw, b_dw, conv_ln_g, conv_ln_b, w_pw2, w_o, norm2_g, w_mlp1, w_mlp2, final_norm_g):
    bsz, s, d = x.shape
    n_ctx = ctx.shape[1]
    depth = w_mod.shape[0]
    assert bsz < 8 and s % GRID_W == 0 and s % LANES == 0 and n_ctx % LANES == 0
    p = dict(w_in=w_in, w_uq=w_uq, w_ukv=w_ukv, q_norm_g=q_norm_g, kv_norm_g=kv_norm_g,
             w_fourier=w_fourier, w_dw=w_dw, b_dw=b_dw, conv_ln_g=conv_ln_g, conv_ln_b=conv_ln_b,
             w_pw2=w_pw2, w_o=w_o, norm1_g=norm1_g, norm2_g=norm2_g)
    mlp_w = (w_mlp1.astype(BF16), w_mlp2.astype(BF16))

    c_rows = jnp.zeros((8, d), F32).at[:bsz].set(c).at[bsz].set(c_ctx)
    mods_all = _modulation(c_rows, w_mod, b_mod).reshape(depth, 8, N_MOD, 1, d)
    x_row = lambda b: b
    ctx_row = lambda b: bsz

    tm_x, tm_c = _token_tile(s), _token_tile(n_ctx)
    tq = next(t for t in (ATTN_QUERY_TILE, tm_x) if s % t == 0)
    tm_mlp = next(t for t in (MLP_TILE, tm_x) if s % t == 0)
    tk = next(t for t in (ATTN_KEY_TILE, tm_x) if s % t == 0 and t % tm_x == 0)
    ta_x, tb_x = _rope_tables(s)
    ta_c, tb_c = _identity_rope_tables(n_ctx)
    fc = _fourier_consts(s, n_ctx)
    gf = final_norm_g.reshape(1, d)

    for l in range(depth):
        last = l == depth - 1
        w = _layer_weights(l, p, fc["dft_c"])
        mods = mods_all[l]

        qx, kx, vx, zx, ux = _prep(x, mods, x_row, w["g1"], w, ta_x, tb_x, tm_x)
        qc, kc, vc, zc, uc = _prep(ctx, mods, ctx_row, w["g1"], w, ta_c, tb_c, tm_c)

        attn_x = _attention(qx, kc, vc, kx, vx, tq, tk)
        yf_x = _fourier_seq(zx, fc["x"])
        yc_x = _conv_module(ux, w, tm_x)
        x = _mix_mlp(x, mods, x_row, yf_x, attn_x, yc_x, w, mlp_w, l, gf, tm_mlp, final_norm=last)

        if not last:
            attn_c = _attention(qc, kc, vc, None, None, tm_c, tk)
            yf_c = _fourier_seq(zc, fc["ctx"])
            yc_c = _conv_module(uc, w, tm_c)
            ctx = _mix_mlp(ctx, mods, ctx_row, yf_c, attn_c, yc_c, w, mlp_w, l, gf, tm_c, final_norm=False)
    return x
```

```python
import functools
import math

import numpy as np
import jax
import jax.numpy as jnp
from jax import lax
from jax.experimental import pallas as pl
from jax.experimental.pallas import tpu as pltpu

F32 = jnp.float32
BF16 = jnp.bfloat16

GRID_W = 64
F_GROUPS = 4
F_GROUP_DIM = 64
F_WIDTH = F_GROUPS * F_GROUP_DIM
MLA_HEADS = 8
QK_NOPE_DIM = 64
QK_ROPE_DIM = 32
V_HEAD_DIM = 64
Q_LORA_RANK = 256
KV_LORA_RANK = 128
CONV_WIDTH = 256
CONV_KERNEL = 31
N_MOD = 6
ROPE_BASE = 10000.0
ROPE_AXIS_DIM = QK_ROPE_DIM // 2
ATTN_SCALE = (QK_NOPE_DIM + QK_ROPE_DIM) ** -0.5
EPS = 1e-6

LANES = 128
HEAD_PAD = LANES
HP = MLA_HEADS * HEAD_PAD
ONES_ROW = V_HEAD_DIM
HEADS_PER_STEP = 2
V_ROWS = 80
CONV_HALO = 16
FFT_N2 = 128
VMEM_LIMIT = 56 * 1024 * 1024
TILES_PER_PASS = 4
LAZY_HEADROOM = 60.0
ATTN_KEY_TILE = 512
ATTN_QUERY_TILE = 4096
MLP_TILE = 1024
Q_SCALE = ATTN_SCALE * math.log2(math.e)


def _cparams(n_axes, flags=None):
    return pltpu.CompilerParams(dimension_semantics=("parallel",) * n_axes,
                                vmem_limit_bytes=VMEM_LIMIT, flags=flags)


def _const_spec(shape):
    nd = len(shape)
    return pl.BlockSpec(shape, lambda *_: (0,) * nd, pipeline_mode=pl.Buffered(1))


def _dot(a, b):
    return jnp.dot(a, b, preferred_element_type=F32)


def _split_bf16(v):
    hi = v.astype(BF16)
    lo = (v - hi.astype(F32)).astype(BF16)
    return hi, lo


def _mod_kernel(c_ref, w_ref, b_ref, o_ref):
    c = c_ref[...]
    s = c * jax.nn.sigmoid(c)
    s_hi, s_lo = _split_bf16(s)
    w_hi, w_lo = _split_bf16(w_ref[...])
    o_ref[...] = _dot(s_hi, w_hi) + _dot(s_hi, w_lo) + _dot(s_lo, w_hi) + b_ref[...]


def _modulation(c_rows, w_mod, b_mod):
    n_layers, d, n6 = w_mod.shape
    tn = 1536
    return pl.pallas_call(
        _mod_kernel,
        out_shape=jax.ShapeDtypeStruct((n_layers, 8, n6), F32),
        grid=(n_layers, n6 // tn),
        in_specs=[
            pl.BlockSpec((8, d), lambda l, j: (0, 0)),
            pl.BlockSpec((None, d, tn), lambda l, j: (l, 0, j)),
            pl.BlockSpec((None, 1, tn), lambda l, j: (l, 0, j)),
        ],
        out_specs=pl.BlockSpec((None, 8, tn), lambda l, j: (l, 0, j)),
        compiler_params=_cparams(2),
        name="modulation",
    )(c_rows, w_mod, b_mod.reshape(n_layers, 1, n6))


def _rms(v, g):
    return v * lax.rsqrt(jnp.mean(v * v, axis=-1, keepdims=True) + EPS) * g


def _prep_kernel(x_ref, shift_ref, scale_ref, g1_ref, w1_ref, qg_ref, wqa_ref, wqb_ref,
                 kvg_ref, wk_ref, wv_ref, dft_ref, ta_ref, tb_ref, vone_ref,
                 q_ref, k_ref, v_ref, z_ref, u_ref):
    x = x_ref[...]
    h = (_rms(x, g1_ref[...]) * (1.0 + scale_ref[...]) + shift_ref[...]).astype(BF16)
    z = _dot(h, w1_ref[...])

    zz = _dot(z[:, 0:F_WIDTH].astype(BF16), dft_ref[...])
    z_ref[0] = zz[:, :F_WIDTH]
    z_ref[1] = zz[:, F_WIDTH:]

    ta = ta_ref[...]
    tb = tb_ref[...]
    lane = lax.broadcasted_iota(jnp.int32, (1, HEAD_PAD), 1)
    nope = (lane < QK_NOPE_DIM).astype(F32)

    c0 = F_WIDTH
    cqn = _rms(z[:, c0:c0 + Q_LORA_RANK], qg_ref[...]).astype(BF16)
    qa = _dot(cqn, wqa_ref[...])
    qb = _dot(cqn, wqb_ref[...])
    qta = (ta + nope) * Q_SCALE
    qtb = tb * Q_SCALE
    for hd in range(MLA_HEADS):
        sl = slice(hd * HEAD_PAD, (hd + 1) * HEAD_PAD)
        q_ref[:, sl] = (qa[:, sl] * qta + qb[:, sl] * qtb).astype(BF16)

    c1 = c0 + Q_LORA_RANK
    ckvn = _rms(z[:, c1:c1 + KV_LORA_RANK], kvg_ref[...]).astype(BF16)
    kn = _dot(ckvn, wk_ref[...])
    c2 = c1 + KV_LORA_RANK
    kr = z[:, c2:c2 + HEAD_PAD] * ta + z[:, c2 + HEAD_PAD:c2 + 2 * HEAD_PAD] * tb
    for hd in range(MLA_HEADS):
        sl = slice(hd * HEAD_PAD, (hd + 1) * HEAD_PAD)
        k_ref[:, sl] = (kn[:, sl] + kr).astype(BF16)
    vt = lax.dot_general(wv_ref[...], ckvn, (((1,), (1,)), ((), ())), preferred_element_type=F32)
    vt = vt + vone_ref[...]
    for hd in range(MLA_HEADS):
        v_ref[hd] = vt[hd * V_ROWS:(hd + 1) * V_ROWS, :].astype(BF16)

    c3 = c2 + 2 * HEAD_PAD
    a = z[:, c3:c3 + CONV_WIDTH]
    g = z[:, c3 + CONV_WIDTH:c3 + 2 * CONV_WIDTH]
    u_ref[...] = (a * jax.nn.sigmoid(g)).astype(BF16)


def _prep(x, mods, mod_row, g1, w, ta, tb, tm):
    bsz, s, d = x.shape
    n1 = w["w1"].shape[1]
    row = lambda j: pl.BlockSpec((None, None, 1, d), lambda b, i: (mod_row(b), j, 0, 0))
    tok = lambda width: pl.BlockSpec((None, tm, width), lambda b, i: (b, i, 0))
    return pl.pallas_call(
        _prep_kernel,
        out_shape=(
            jax.ShapeDtypeStruct((bsz, s, HP), BF16),
            jax.ShapeDtypeStruct((bsz, s, HP), BF16),
            jax.ShapeDtypeStruct((bsz, MLA_HEADS, s // tm, V_ROWS, tm), BF16),
            jax.ShapeDtypeStruct((bsz, 2, s, F_WIDTH), F32),
            jax.ShapeDtypeStruct((bsz, s, CONV_WIDTH), BF16),
        ),
        grid=(bsz, s // tm),
        in_specs=[
            tok(d), row(0), row(1), _const_spec((1, d)),
            _const_spec((d, n1)),
            _const_spec((1, Q_LORA_RANK)), _const_spec((Q_LORA_RANK, HP)), _const_spec((Q_LORA_RANK, HP)),
            _const_spec((1, KV_LORA_RANK)), _const_spec((KV_LORA_RANK, HP)),
            _const_spec((MLA_HEADS * V_ROWS, KV_LORA_RANK)),
            _const_spec((F_WIDTH, 2 * F_WIDTH)),
            pl.BlockSpec((tm, HEAD_PAD), lambda b, i: (i, 0)),
            pl.BlockSpec((tm, HEAD_PAD), lambda b, i: (i, 0)),
            _const_spec((MLA_HEADS * V_ROWS, 1)),
        ],
        out_specs=(
            tok(HP), tok(HP),
            pl.BlockSpec((None, MLA_HEADS, None, V_ROWS, tm), lambda b, i: (b, 0, i, 0, 0)),
            pl.BlockSpec((None, 2, tm, F_WIDTH), lambda b, i: (b, 0, i, 0)),
            tok(CONV_WIDTH),
        ),
        compiler_params=_cparams(2),
        name="prep",
    )(x, mods, mods, g1, w["w1"], w["qg"], w["wqa"], w["wqb"], w["kvg"], w["wk"], w["wv"],
      w["dft_c"], ta, tb, w["vone"])


def _attn_kernel(*refs, tk, n_x_blocks):
    if n_x_blocks:
        q_ref, kc_ref, vc_ref, kx_ref, vx_ref, o_ref, m_sc, acc_sc, out_sc, ex_sc = refs
    else:
        q_ref, kc_ref, vc_ref, o_ref, m_sc, acc_sc, out_sc = refs
    heads = range(HEADS_PER_STEP)
    lanes = [slice(h * HEAD_PAD, (h + 1) * HEAD_PAD) for h in heads]
    qt = [q_ref[:, lanes[h]].astype(F32).T.astype(BF16) for h in heads]

    def exact_tile(h, k, vt, first=False):
        s = _dot(k, qt[h])
        m_tile = jnp.max(s, axis=0, keepdims=True)
        m_new = m_tile if first else jnp.maximum(m_sc[h], m_tile)
        p = jnp.exp2(s - m_new).astype(BF16)
        pv = _dot(vt, p)
        acc_sc[h] = pv if first else jnp.exp2(m_sc[h] - m_new) * acc_sc[h] + pv
        m_sc[h] = m_new

    for h in heads:
        exact_tile(h, kc_ref[:, lanes[h]], vc_ref[h, 0], first=True)
    if n_x_blocks:
        n = n_x_blocks
        v_sub = tk // vx_ref.shape[3]

        def k_tile(h, t):
            return kx_ref[pl.ds(pl.multiple_of(t * tk, tk), tk), lanes[h]]

        def v_tile(h, t):
            return jnp.concatenate([vx_ref[h, v_sub * t + i] for i in range(v_sub)], axis=1)

        def lazy_tile(h, t):
            s = _dot(k_tile(h, t), qt[h])
            m_old = m_sc[h]
            p = jnp.exp2(s - m_old).astype(BF16)
            m_tile = jnp.max(s, axis=0, keepdims=True)
            m_new = jnp.maximum(m_old, m_tile)
            acc_sc[h] = (acc_sc[h] + _dot(v_tile(h, t), p)) * jnp.exp2(m_old - m_new)
            m_sc[h] = m_new
            ex_sc[h] = jnp.maximum(ex_sc[h], m_tile - m_old)

        ex_sc[...] = jnp.zeros(ex_sc.shape, F32)
        n_peel = n % TILES_PER_PASS
        for t in range(n_peel):
            for h in heads:
                lazy_tile(h, t)

        def body(j, carry):
            t0 = n_peel + TILES_PER_PASS * j
            for u in range(TILES_PER_PASS):
                for h in heads:
                    lazy_tile(h, t0 + u)
            return carry
        lax.fori_loop(0, (n - n_peel) // TILES_PER_PASS, body, 0)

        for h in heads:
            @pl.when(jnp.max(ex_sc[h]) > LAZY_HEADROOM)
            def _(h=h):
                exact_tile(h, kc_ref[:, lanes[h]], vc_ref[h, 0], first=True)

                def exact_body(t, carry):
                    exact_tile(h, k_tile(h, t), v_tile(h, t))
                    return carry
                lax.fori_loop(0, n, exact_body, 0)

    for h in heads:
        acc = acc_sc[h]
        out_sc[h * V_HEAD_DIM:(h + 1) * V_HEAD_DIM, :] = acc[:V_HEAD_DIM] / acc[ONES_ROW:ONES_ROW + 1, :]
    o_ref[...] = out_sc[...].T.astype(BF16)


def _attention(q, kc, vc, kx, vx, tq, tk):
    bsz, sq, _ = q.shape
    n_ctx = kc.shape[1]
    hw = HEADS_PER_STEP * HEAD_PAD
    qspec = pl.BlockSpec((None, tq, hw), lambda b, h, i: (b, i, h))
    kspec = lambda n: pl.BlockSpec((None, n, hw), lambda b, h, i: (b, 0, h))
    vspec = lambda v: pl.BlockSpec((None, HEADS_PER_STEP) + v.shape[2:], lambda b, h, i: (b, h, 0, 0, 0))
    assert vc.shape[2:] == (1, V_ROWS, n_ctx)
    args = [q, kc, vc]
    in_specs = [qspec, kspec(n_ctx), vspec(vc)]
    n_x_blocks = 0
    out_w = HEADS_PER_STEP * V_HEAD_DIM
    per_head = lambda rows: pltpu.VMEM((HEADS_PER_STEP, rows, tq), F32)
    scratch = [per_head(1), per_head(V_ROWS), pltpu.VMEM((out_w, tq), F32)]
    if kx is not None:
        s = kx.shape[1]
        n_x_blocks = s // tk
        assert vx.shape[3] == V_ROWS and tk % vx.shape[4] == 0 and vx.shape[2] * vx.shape[4] == s
        args += [kx, vx]
        in_specs += [kspec(s), vspec(vx)]
        scratch += [per_head(1)]
    return pl.pallas_call(
        functools.partial(_attn_kernel, tk=tk, n_x_blocks=n_x_blocks),
        out_shape=jax.ShapeDtypeStruct((bsz, sq, MLA_HEADS * V_HEAD_DIM), BF16),
        grid=(bsz, MLA_HEADS // HEADS_PER_STEP, sq // tq),
        in_specs=in_specs,
        out_specs=pl.BlockSpec((None, tq, out_w), lambda b, h, i: (b, i, h)),
        scratch_shapes=scratch,
        compiler_params=_cparams(3),
        name="attn_x" if kx is not None else "attn_ctx",
    )(*args)


def _fft1_kernel(z_ref, m_ref, o_ref, *, g, n1):
    for j in range(g):
        zz = jnp.concatenate([z_ref[0, :, j, :], z_ref[1, :, j, :]], axis=0).astype(BF16)
        r = _dot(m_ref[j], zz)
        o_ref[0, j] = r[:n1]
        o_ref[1, j] = r[n1:]


def _fft2_kernel(w_ref, x_ref, o_ref, *, g, scale):
    for j in range(g):
        xx = jnp.concatenate([x_ref[0, :, j, :], x_ref[1, :, j, :]], axis=0).astype(BF16)
        o_ref[:, j, :] = _dot(w_ref[...], xx) * scale


def _dft_direct_kernel(w_ref, x_ref, o_ref, *, scale):
    o_ref[...] = _dot(w_ref[...], x_ref[...].astype(BF16)) * scale


def _dft_cos_sin(rows, cols, period):
    r = (rows[:, None] * cols[None, :]) % period
    ang = r.astype(F32) * (2.0 * math.pi / period)
    return jnp.cos(ang), jnp.sin(ang)


def _fourier_seq(z, consts):
    bsz, _, s, c = z.shape
    scale = 1.0 / math.sqrt(s * F_GROUP_DIM)
    if s <= 2 * FFT_N2:
        w = consts["w_direct"]
        return pl.pallas_call(
            functools.partial(_dft_direct_kernel, scale=scale),
            out_shape=jax.ShapeDtypeStruct((bsz, s, c), F32),
            grid=(bsz,),
            in_specs=[_const_spec(w.shape), pl.BlockSpec((None, 2 * s, c), lambda b: (b, 0, 0))],
            out_specs=pl.BlockSpec((None, s, c), lambda b: (b, 0, 0)),
            compiler_params=_cparams(1),
            name="dft_direct",
        )(w, z.reshape(bsz, 2 * s, c))
    n1, n2 = s // FFT_N2, FFT_N2
    g = 8
    x2 = pl.pallas_call(
        functools.partial(_fft1_kernel, g=g, n1=n1),
        out_shape=jax.ShapeDtypeStruct((bsz, 2, n2, n1, c), F32),
        grid=(bsz, n2 // g),
        in_specs=[
            pl.BlockSpec((None, 2, n1, g, c), lambda b, j: (b, 0, 0, j, 0)),
            pl.BlockSpec((g, 2 * n1, 2 * n1), lambda b, j: (j, 0, 0)),
        ],
        out_specs=pl.BlockSpec((None, 2, g, n1, c), lambda b, j: (b, 0, j, 0, 0)),
        compiler_params=_cparams(2),
        name="fft_stage1",
    )(z.reshape(bsz, 2, n1, n2, c), consts["m1"])
    w2 = consts["w2"]
    y = pl.pallas_call(
        functools.partial(_fft2_kernel, g=g, scale=scale),
        out_shape=jax.ShapeDtypeStruct((bsz, n2, n1, c), F32),
        grid=(bsz, n1 // g),
        in_specs=[
            _const_spec(w2.shape),
            pl.BlockSpec((None, 2, n2, g, c), lambda b, j: (b, 0, 0, j, 0)),
        ],
        out_specs=pl.BlockSpec((None, n2, g, c), lambda b, j: (b, 0, j, 0)),
        compiler_params=_cparams(2),
        name="fft_stage2",
    )(w2, x2)
    return y.reshape(bsz, s, c)


def _conv_kernel(uc_ref, up_ref, un_ref, wdw_ref, bdw_ref, lng_ref, lnb_ref, wpw_ref, o_ref, ext_sc, sh_sc,
                 *, tm, rc):
    i = pl.program_id(1)
    last = pl.num_programs(1) - 1
    ext_sc[0:CONV_HALO] = jnp.where(i > 0, up_ref[...].astype(F32), 0.0)
    ext_sc[CONV_HALO:CONV_HALO + tm] = uc_ref[...].astype(F32)
    ext_sc[CONV_HALO + tm:2 * CONV_HALO + tm] = jnp.where(i < last, un_ref[...].astype(F32), 0.0)
    first = CONV_HALO - CONV_KERNEL // 2
    sub = sh_sc.shape[0]
    for r in range(sub):
        sh_sc[r] = ext_sc[r:r + sh_sc.shape[1], :]
    for r0 in range(0, tm, rc):
        acc = jnp.zeros((rc, CONV_WIDTH), F32)
        for k in range(CONV_KERNEL):
            r = (first + k) % sub
            lo = r0 + (first + k) - r
            acc = acc + sh_sc[r, lo:lo + rc, :] * wdw_ref[k:k + 1, :]
        acc = acc + bdw_ref[...]
        mu = jnp.mean(acc, axis=-1, keepdims=True)
        xc = acc - mu
        var = jnp.mean(xc * xc, axis=-1, keepdims=True)
        y = xc * lax.rsqrt(var + EPS) * lng_ref[...] + lnb_ref[...]
        y = y * jax.nn.sigmoid(y)
        o_ref[r0:r0 + rc, :] = _dot(y.astype(BF16), wpw_ref[...]).astype(BF16)


def _conv_module(u, w, tm):
    bsz, s, c = u.shape
    hb = tm // CONV_HALO
    n_halo = s // CONV_HALO
    return pl.pallas_call(
        functools.partial(_conv_kernel, tm=tm, rc=64),
        out_shape=jax.ShapeDtypeStruct((bsz, s, c), BF16),
        grid=(bsz, s // tm),
        in_specs=[
            pl.BlockSpec((None, tm, c), lambda b, i: (b, i, 0)),
            pl.BlockSpec((None, CONV_HALO, c), lambda b, i: (b, jnp.maximum(i * hb - 1, 0), 0)),
            pl.BlockSpec((None, CONV_HALO, c), lambda b, i: (b, jnp.minimum((i + 1) * hb, n_halo - 1), 0)),
            _const_spec((32, c)), _const_spec((1, c)), _const_spec((1, c)), _const_spec((1, c)),
            _const_spec((c, c)),
        ],
        out_specs=pl.BlockSpec((None, tm, c), lambda b, i: (b, i, 0)),
        scratch_shapes=[pltpu.VMEM((tm + 2 * CONV_HALO, c), F32),
                        pltpu.VMEM((8, tm + 2 * CONV_HALO - 8, c), F32)],
        compiler_params=_cparams(2),
        name="conv",
    )(u, u, u, w["w_dw"], w["b_dw"], w["ln_g"], w["ln_b"], w["w_pw2"])


def _mix_mlp_kernel(x_ref, gate1_ref, shift_ref, scale_ref, gate2_ref, yf_ref, at_ref, yc_ref,
                    wf_ref, wof_ref, woa_ref, woc_ref, g2_ref, w1_ref, w2_ref, gf_ref, o_ref,
                    *, ff_chunk, final_norm):
    yf = _dot(yf_ref[...].astype(BF16), wf_ref[...]).astype(BF16)
    mix = _dot(yf, wof_ref[...]) + _dot(at_ref[...], woa_ref[...]) + _dot(yc_ref[...], woc_ref[...])
    x = x_ref[...] + gate1_ref[...] * mix
    h = (_rms(x, g2_ref[...]) * (1.0 + scale_ref[...]) + shift_ref[...]).astype(BF16)
    d_ff = w1_ref.shape[1]
    acc = jnp.zeros(x.shape, F32)
    for c0 in range(0, d_ff, ff_chunk):
        a = jnp.maximum(_dot(h, w1_ref[:, c0:c0 + ff_chunk]), 0.0)
        acc = acc + _dot((a * a).astype(BF16), w2_ref[c0:c0 + ff_chunk, :])
    y = x + gate2_ref[...] * acc
    if final_norm:
        y = _rms(y, gf_ref[...])
    o_ref[...] = y


def _mix_mlp(x, mods, mod_row, yf, attn, yc, w, mlp_w, layer, gf, tm, final_norm):
    bsz, s, d = x.shape
    w_mlp1, w_mlp2 = mlp_w
    d_ff = w_mlp1.shape[2]
    layer_spec = lambda r, c: pl.BlockSpec((None, r, c), lambda b, i: (layer, 0, 0), pipeline_mode=pl.Buffered(1))
    row = lambda j: pl.BlockSpec((None, None, 1, d), lambda b, i: (mod_row(b), j, 0, 0))
    tok = lambda width: pl.BlockSpec((None, tm, width), lambda b, i: (b, i, 0))
    return pl.pallas_call(
        functools.partial(_mix_mlp_kernel, ff_chunk=1024, final_norm=final_norm),
        out_shape=jax.ShapeDtypeStruct((bsz, s, d), F32),
        grid=(bsz, s // tm),
        in_specs=[
            tok(d), row(2), row(3), row(4), row(5),
            tok(F_WIDTH), tok(MLA_HEADS * V_HEAD_DIM), tok(CONV_WIDTH),
            _const_spec((F_WIDTH, F_WIDTH)), _const_spec((F_WIDTH, d)), _const_spec((MLA_HEADS * V_HEAD_DIM, d)),
            _const_spec((CONV_WIDTH, d)),
            _const_spec((1, d)), layer_spec(d, d_ff), layer_spec(d_ff, d), _const_spec((1, d)),
        ],
        out_specs=tok(d),
        compiler_params=_cparams(2),
        name="mix_mlp",
    )(x, mods, mods, mods, mods, yf, attn, yc, w["w_f"], w["wo_f"], w["wo_a"], w["wo_c"],
      w["g2"], w_mlp1, w_mlp2, gf)


def _rope_tables(s):
    t = jnp.arange(s)
    row = (t // GRID_W).astype(F32)
    col = (t % GRID_W).astype(F32)
    inv = ROPE_BASE ** (-jnp.arange(0, ROPE_AXIS_DIM, 2, dtype=F32) / ROPE_AXIS_DIM)
    ang = jnp.concatenate([row[:, None] * inv, col[:, None] * inv], axis=-1)
    cos, sin = jnp.cos(ang), jnp.sin(ang)
    z_lo = jnp.zeros((s, QK_NOPE_DIM), F32)
    z_hi = jnp.zeros((s, HEAD_PAD - QK_NOPE_DIM - QK_ROPE_DIM), F32)
    ta = jnp.concatenate([z_lo, cos, cos, z_hi], axis=-1)
    tb = jnp.concatenate([z_lo, -sin, sin, z_hi], axis=-1)
    return ta, tb


def _identity_rope_tables(s):
    lane = np.arange(HEAD_PAD)
    ta = ((lane >= QK_NOPE_DIM) & (lane < QK_NOPE_DIM + QK_ROPE_DIM)).astype(np.float32)
    return jnp.asarray(np.tile(ta, (s, 1))), jnp.zeros((s, HEAD_PAD), F32)


def _fourier_consts(s, n_ctx):
    ar = lambda n: jnp.arange(n, dtype=jnp.int32)
    cc, sc = _dft_cos_sin(ar(F_GROUP_DIM), ar(F_GROUP_DIM), F_GROUP_DIM)
    eye = jnp.eye(F_GROUPS, dtype=F32)
    dft_c = jnp.concatenate([jnp.kron(eye, cc), -jnp.kron(eye, sc)], axis=1).astype(BF16)
    consts = {"dft_c": dft_c}

    def direct(n):
        c, sn = _dft_cos_sin(ar(n), ar(n), n)
        return jnp.concatenate([c, sn], axis=1).astype(BF16)

    def two_stage(n):
        n1, n2 = n // FFT_N2, FFT_N2
        pos = ar(n1)[None, :] * n2 + ar(n2)[:, None]
        r = (ar(n1)[None, :, None] * pos[:, None, :]) % n
        ang = r.astype(F32) * (2.0 * math.pi / n)
        c, sn = jnp.cos(ang), jnp.sin(ang)
        m1 = jnp.concatenate([jnp.concatenate([c, sn], axis=2),
                              jnp.concatenate([-sn, c], axis=2)], axis=1).astype(BF16)
        return m1, direct(n2)

    for n, tag in ((s, "x"), (n_ctx, "ctx")):
        if n <= 2 * FFT_N2:
            consts[tag] = {"w_direct": direct(n)}
        else:
            m1, w2 = two_stage(n)
            consts[tag] = {"m1": m1, "w2": w2}
    return consts


def _pad_heads(w, width):
    k = w.shape[0]
    w = w.reshape(k, MLA_HEADS, width)
    return jnp.pad(w, ((0, 0), (0, 0), (0, HEAD_PAD - width))).reshape(k, HP)


def _layer_weights(l, p, dft_c):
    d = p["w_in"].shape[1]
    w_in = p["w_in"][l]
    s2 = F_WIDTH + Q_LORA_RANK + KV_LORA_RANK
    s3 = s2 + QK_ROPE_DIM
    half = QK_ROPE_DIM // 2
    w_kr = w_in[:, s2:s3]
    w_kr_sw = jnp.concatenate([w_kr[:, half:], w_kr[:, :half]], axis=1)
    lo = jnp.zeros((d, QK_NOPE_DIM), F32)
    hi = jnp.zeros((d, HEAD_PAD - QK_NOPE_DIM - QK_ROPE_DIM), F32)
    w1 = jnp.concatenate([w_in[:, :s2], lo, w_kr, hi, lo, w_kr_sw, hi, w_in[:, s3:]], axis=1).astype(BF16)

    qd = QK_NOPE_DIM + QK_ROPE_DIM
    w_uq = p["w_uq"][l].reshape(Q_LORA_RANK, MLA_HEADS, qd)
    rope = w_uq[:, :, QK_NOPE_DIM:]
    rope_sw = jnp.concatenate([rope[:, :, half:], rope[:, :, :half]], axis=2)
    wqa = _pad_heads(w_uq.reshape(Q_LORA_RANK, MLA_HEADS * qd), qd).astype(BF16)
    wqb = jnp.pad(rope_sw, ((0, 0), (0, 0), (QK_NOPE_DIM, HEAD_PAD - qd))).reshape(Q_LORA_RANK, HP).astype(BF16)

    w_ukv = p["w_ukv"][l].reshape(KV_LORA_RANK, MLA_HEADS, QK_NOPE_DIM + V_HEAD_DIM)
    wk = _pad_heads(w_ukv[:, :, :QK_NOPE_DIM].reshape(KV_LORA_RANK, -1), QK_NOPE_DIM).astype(BF16)
    wv = jnp.pad(w_ukv[:, :, QK_NOPE_DIM:], ((0, 0), (0, 0), (0, V_ROWS - V_HEAD_DIM)))
    wv = wv.reshape(KV_LORA_RANK, MLA_HEADS * V_ROWS).T.astype(BF16)
    vone = np.zeros((MLA_HEADS * V_ROWS, 1), np.float32)
    vone[ONES_ROW::V_ROWS, 0] = 1.0

    w_o = p["w_o"][l]
    a0, a1 = F_WIDTH, F_WIDTH + MLA_HEADS * V_HEAD_DIM
    w_dw = jnp.pad(p["w_dw"][l], ((0, 32 - CONV_KERNEL), (0, 0)))
    r2 = lambda v: v.reshape(1, -1)
    return {
        "w1": w1, "qg": r2(p["q_norm_g"][l]), "wqa": wqa, "wqb": wqb,
        "kvg": r2(p["kv_norm_g"][l]), "wk": wk, "wv": wv, "dft_c": dft_c, "vone": jnp.asarray(vone),
        "w_dw": w_dw, "b_dw": r2(p["b_dw"][l]), "ln_g": r2(p["conv_ln_g"][l]), "ln_b": r2(p["conv_ln_b"][l]),
        "w_pw2": p["w_pw2"][l].astype(BF16),
        "w_f": p["w_fourier"][l].astype(BF16), "wo_f": w_o[:a0].astype(BF16), "wo_a": w_o[a0:a1].astype(BF16),
        "wo_c": w_o[a1:].astype(BF16),
        "g1": r2(p["norm1_g"][l]), "g2": r2(p["norm2_g"][l]),
    }


def _token_tile(s):
    for tm in (512, 256, 128):
        if s % tm == 0:
            return tm
    raise ValueError(f"sequence length {s} must be a multiple of 128")


def kernel(x, c, ctx, c_ctx, w_mod, b_mod, norm1_g, w_in, q_norm_g, w_uq, kv_norm_g, w_ukv, w_fourier,
           w_dw, b_dw, conv_ln_g, conv_ln_b, w_pw2, w_o, norm2_g, w_mlp1, w_mlp2, final_norm_g):
    bsz, s, d = x.shape
    n_ctx = ctx.shape[1]
    depth = w_mod.shape[0]
    assert bsz < 8 and s % GRID_W == 0 and s % LANES == 0 and n_ctx % LANES == 0
    p = dict(w_in=w_in, w_uq=w_uq, w_ukv=w_ukv, q_norm_g=q_norm_g, kv_norm_g=kv_norm_g,
             w_fourier=w_fourier, w_dw=w_dw, b_dw=b_dw, conv_ln_g=conv_ln_g, conv_ln_b=conv_ln_b,
             w_pw2=w_pw2, w_o=w_o, norm1_g=norm1_g, norm2_g=norm2_g)
    mlp_w = (w_mlp1.astype(BF16), w_mlp2.astype(BF16))

    c_rows = jnp.zeros((8, d), F32).at[:bsz].set(c).at[bsz].set(c_ctx)
    mods_all = _modulation(c_rows, w_mod, b_mod).reshape(depth, 8, N_MOD, 1, d)
    x_row = lambda b: b
    ctx_row = lambda b: bsz

    tm_x, tm_c = _token_tile(s), _token_tile(n_ctx)
    tq = next(t for t in (ATTN_QUERY_TILE, tm_x) if s % t == 0)
    tm_mlp = next(t for t in (MLP_TILE, tm_x) if s % t == 0)
    tk = next(t for t in (ATTN_KEY_TILE, tm_x) if s % t == 0 and t % tm_x == 0)
    ta_x, tb_x = _rope_tables(s)
    ta_c, tb_c = _identity_rope_tables(n_ctx)
    fc = _fourier_consts(s, n_ctx)
    gf = final_norm_g.reshape(1, d)

    for l in range(depth):
        last = l == depth - 1
        w = _layer_weights(l, p, fc["dft_c"])
        mods = mods_all[l]

        qx, kx, vx, zx, ux = _prep(x, mods, x_row, w["g1"], w, ta_x, tb_x, tm_x)
        qc, kc, vc, zc, uc = _prep(ctx, mods, ctx_row, w["g1"], w, ta_c, tb_c, tm_c)

        attn_x = _attention(qx, kc, vc, kx, vx, tq, tk)
        yf_x = _fourier_seq(zx, fc["x"])
        yc_x = _conv_module(ux, w, tm_x)
        x = _mix_mlp(x, mods, x_row, yf_x, attn_x, yc_x, w, mlp_w, l, gf, tm_mlp, final_norm=last)

        if not last:
            attn_c = _attention(qc, kc, vc, None, None, tm_c, tk)
            yf_c = _fourier_seq(zc, fc["ctx"])
            yc_c = _conv_module(uc, w, tm_c)
            ctx = _mix_mlp(ctx, mods, ctx_row, yf_c, attn_c, yc_c, w, mlp_w, l, gf, tm_c, final_norm=False)
    return x
```

```python
import functools
import math

import numpy as np
import jax
import jax.numpy as jnp
from jax import lax
from jax.experimental import pallas as pl
from jax.experimental.pallas import tpu as pltpu

F32 = jnp.float32
BF16 = jnp.bfloat16

GRID_W = 64
F_GROUPS = 4
F_GROUP_DIM = 64
F_WIDTH = F_GROUPS * F_GROUP_DIM
MLA_HEADS = 8
QK_NOPE_DIM = 64
QK_ROPE_DIM = 32
V_HEAD_DIM = 64
Q_LORA_RANK = 256
KV_LORA_RANK = 128
CONV_WIDTH = 256
CONV_KERNEL = 31
N_MOD = 6
ROPE_BASE = 10000.0
ROPE_AXIS_DIM = QK_ROPE_DIM // 2
ATTN_SCALE = (QK_NOPE_DIM + QK_ROPE_DIM) ** -0.5
EPS = 1e-6

LANES = 128
HEAD_PAD = LANES
HP = MLA_HEADS * HEAD_PAD
ONES_ROW = V_HEAD_DIM
HEADS_PER_STEP = 2
V_ROWS = 80
CONV_HALO = 16
FFT_N2 = 128
VMEM_LIMIT = 56 * 1024 * 1024
TILES_PER_PASS = 4
LAZY_HEADROOM = 60.0
ATTN_KEY_TILE = 512
ATTN_QUERY_TILE = 4096
MLP_TILE = 1024
QUERY_CHUNK = 1024
Q_SCALE = ATTN_SCALE * math.log2(math.e)


def _cparams(n_axes, flags=None):
    return pltpu.CompilerParams(dimension_semantics=("parallel",) * n_axes,
                                vmem_limit_bytes=VMEM_LIMIT, flags=flags)


def _const_spec(shape):
    nd = len(shape)
    return pl.BlockSpec(shape, lambda *_: (0,) * nd, pipeline_mode=pl.Buffered(1))


def _dot(a, b):
    return jnp.dot(a, b, preferred_element_type=F32)


def _split_bf16(v):
    hi = v.astype(BF16)
    lo = (v - hi.astype(F32)).astype(BF16)
    return hi, lo


def _mod_kernel(c_ref, w_ref, b_ref, o_ref):
    c = c_ref[...]
    s = c * jax.nn.sigmoid(c)
    s_hi, s_lo = _split_bf16(s)
    w_hi, w_lo = _split_bf16(w_ref[...])
    o_ref[...] = _dot(s_hi, w_hi) + _dot(s_hi, w_lo) + _dot(s_lo, w_hi) + b_ref[...]


def _modulation(c_rows, w_mod, b_mod):
    n_layers, d, n6 = w_mod.shape
    tn = 1536
    return pl.pallas_call(
        _mod_kernel,
        out_shape=jax.ShapeDtypeStruct((n_layers, 8, n6), F32),
        grid=(n_layers, n6 // tn),
        in_specs=[
            pl.BlockSpec((8, d), lambda l, j: (0, 0)),
            pl.BlockSpec((None, d, tn), lambda l, j: (l, 0, j)),
            pl.BlockSpec((None, 1, tn), lambda l, j: (l, 0, j)),
        ],
        out_specs=pl.BlockSpec((None, 8, tn), lambda l, j: (l, 0, j)),
        compiler_params=_cparams(2),
        name="modulation",
    )(c_rows, w_mod, b_mod.reshape(n_layers, 1, n6))


def _rms(v, g):
    return v * lax.rsqrt(jnp.mean(v * v, axis=-1, keepdims=True) + EPS) * g


def _prep_kernel(x_ref, shift_ref, scale_ref, g1_ref, w1_ref, qg_ref, wqa_ref, wqb_ref,
                 kvg_ref, wk_ref, wv_ref, dft_ref, ta_ref, tb_ref, vone_ref,
                 q_ref, k_ref, v_ref, z_ref, u_ref):
    x = x_ref[...]
    h = (_rms(x, g1_ref[...]) * (1.0 + scale_ref[...]) + shift_ref[...]).astype(BF16)
    z = _dot(h, w1_ref[...])

    zz = _dot(z[:, 0:F_WIDTH].astype(BF16), dft_ref[...])
    z_ref[0] = zz[:, :F_WIDTH]
    z_ref[1] = zz[:, F_WIDTH:]

    ta = ta_ref[...]
    tb = tb_ref[...]
    lane = lax.broadcasted_iota(jnp.int32, (1, HEAD_PAD), 1)
    nope = (lane < QK_NOPE_DIM).astype(F32)

    c0 = F_WIDTH
    cqn = _rms(z[:, c0:c0 + Q_LORA_RANK], qg_ref[...]).astype(BF16)
    qa = _dot(cqn, wqa_ref[...])
    qb = _dot(cqn, wqb_ref[...])
    qta = (ta + nope) * Q_SCALE
    qtb = tb * Q_SCALE
    for hd in range(MLA_HEADS):
        sl = slice(hd * HEAD_PAD, (hd + 1) * HEAD_PAD)
        q_ref[:, sl] = (qa[:, sl] * qta + qb[:, sl] * qtb).astype(BF16)

    c1 = c0 + Q_LORA_RANK
    ckvn = _rms(z[:, c1:c1 + KV_LORA_RANK], kvg_ref[...]).astype(BF16)
    kn = _dot(ckvn, wk_ref[...])
    c2 = c1 + KV_LORA_RANK
    kr = z[:, c2:c2 + HEAD_PAD] * ta + z[:, c2 + HEAD_PAD:c2 + 2 * HEAD_PAD] * tb
    for hd in range(MLA_HEADS):
        sl = slice(hd * HEAD_PAD, (hd + 1) * HEAD_PAD)
        k_ref[:, sl] = (kn[:, sl] + kr).astype(BF16)
    vt = lax.dot_general(wv_ref[...], ckvn, (((1,), (1,)), ((), ())), preferred_element_type=F32)
    vt = vt + vone_ref[...]
    for hd in range(MLA_HEADS):
        v_ref[hd] = vt[hd * V_ROWS:(hd + 1) * V_ROWS, :].astype(BF16)

    c3 = c2 + 2 * HEAD_PAD
    a = z[:, c3:c3 + CONV_WIDTH]
    g = z[:, c3 + CONV_WIDTH:c3 + 2 * CONV_WIDTH]
    u_ref[...] = (a * jax.nn.sigmoid(g)).astype(BF16)


def _prep(x, mods, mod_row, g1, w, ta, tb, tm):
    bsz, s, d = x.shape
    n1 = w["w1"].shape[1]
    row = lambda j: pl.BlockSpec((None, None, 1, d), lambda b, i: (mod_row(b), j, 0, 0))
    tok = lambda width: pl.BlockSpec((None, tm, width), lambda b, i: (b, i, 0))
    return pl.pallas_call(
        _prep_kernel,
        out_shape=(
            jax.ShapeDtypeStruct((bsz, s, HP), BF16),
            jax.ShapeDtypeStruct((bsz, s, HP), BF16),
            jax.ShapeDtypeStruct((bsz, MLA_HEADS, s // tm, V_ROWS, tm), BF16),
            jax.ShapeDtypeStruct((bsz, 2, s, F_WIDTH), F32),
            jax.ShapeDtypeStruct((bsz, s, CONV_WIDTH), BF16),
        ),
        grid=(bsz, s // tm),
        in_specs=[
            tok(d), row(0), row(1), _const_spec((1, d)),
            _const_spec((d, n1)),
            _const_spec((1, Q_LORA_RANK)), _const_spec((Q_LORA_RANK, HP)), _const_spec((Q_LORA_RANK, HP)),
            _const_spec((1, KV_LORA_RANK)), _const_spec((KV_LORA_RANK, HP)),
            _const_spec((MLA_HEADS * V_ROWS, KV_LORA_RANK)),
            _const_spec((F_WIDTH, 2 * F_WIDTH)),
            pl.BlockSpec((tm, HEAD_PAD), lambda b, i: (i, 0)),
            pl.BlockSpec((tm, HEAD_PAD), lambda b, i: (i, 0)),
            _const_spec((MLA_HEADS * V_ROWS, 1)),
        ],
        out_specs=(
            tok(HP), tok(HP),
            pl.BlockSpec((None, MLA_HEADS, None, V_ROWS, tm), lambda b, i: (b, 0, i, 0, 0)),
            pl.BlockSpec((None, 2, tm, F_WIDTH), lambda b, i: (b, 0, i, 0)),
            tok(CONV_WIDTH),
        ),
        compiler_params=_cparams(2),
        name="prep",
    )(x, mods, mods, g1, w["w1"], w["qg"], w["wqa"], w["wqb"], w["kvg"], w["wk"], w["wv"],
      w["dft_c"], ta, tb, w["vone"])


def _attn_kernel(*refs, tk, n_x_blocks):
    if n_x_blocks:
        q_ref, kc_ref, vc_ref, kx_ref, vx_ref, o_ref, m_sc, acc_sc, out_sc, ex_sc = refs
    else:
        q_ref, kc_ref, vc_ref, o_ref, m_sc, acc_sc, out_sc = refs
    heads = range(HEADS_PER_STEP)
    lanes = [slice(h * HEAD_PAD, (h + 1) * HEAD_PAD) for h in heads]
    qt = [q_ref[:, lanes[h]].astype(F32).T.astype(BF16) for h in heads]

    def exact_tile(h, k, vt, first=False):
        s = _dot(k, qt[h])
        m_tile = jnp.max(s, axis=0, keepdims=True)
        m_new = m_tile if first else jnp.maximum(m_sc[h], m_tile)
        p = jnp.exp2(s - m_new).astype(BF16)
        pv = _dot(vt, p)
        acc_sc[h] = pv if first else jnp.exp2(m_sc[h] - m_new) * acc_sc[h] + pv
        m_sc[h] = m_new

    for h in heads:
        exact_tile(h, kc_ref[:, lanes[h]], vc_ref[h, 0], first=True)
    if n_x_blocks:
        n = n_x_blocks
        v_sub = tk // vx_ref.shape[3]

        def k_tile(h, t):
            return kx_ref[pl.ds(pl.multiple_of(t * tk, tk), tk), lanes[h]]

        def v_tile(h, t):
            return jnp.concatenate([vx_ref[h, v_sub * t + i] for i in range(v_sub)], axis=1)

        def lazy_tile(h, t):
            k, vt = k_tile(h, t), v_tile(h, t)
            tq = m_sc.shape[2]
            chunk = min(QUERY_CHUNK, tq)
            for q0 in range(0, tq, chunk):
                qs = slice(q0, q0 + chunk)
                s = _dot(k, qt[h][:, qs])
                m_old = m_sc[h, :, qs]
                p = jnp.exp2(s - m_old).astype(BF16)
                m_tile = jnp.max(s, axis=0, keepdims=True)
                m_new = jnp.maximum(m_old, m_tile)
                acc_sc[h, :, qs] = (acc_sc[h, :, qs] + _dot(vt, p)) * jnp.exp2(m_old - m_new)
                m_sc[h, :, qs] = m_new
                ex_sc[h, :, qs] = jnp.maximum(ex_sc[h, :, qs], m_tile - m_old)

        ex_sc[...] = jnp.zeros(ex_sc.shape, F32)
        n_peel = n % TILES_PER_PASS
        for t in range(n_peel):
            for h in heads:
                lazy_tile(h, t)

        def body(j, carry):
            t0 = n_peel + TILES_PER_PASS * j
            for u in range(TILES_PER_PASS):
                for h in heads:
                    lazy_tile(h, t0 + u)
            return carry
        lax.fori_loop(0, (n - n_peel) // TILES_PER_PASS, body, 0)

        for h in heads:
            @pl.when(jnp.max(ex_sc[h]) > LAZY_HEADROOM)
            def _(h=h):
                exact_tile(h, kc_ref[:, lanes[h]], vc_ref[h, 0], first=True)

                def exact_body(t, carry):
                    exact_tile(h, k_tile(h, t), v_tile(h, t))
                    return carry
                lax.fori_loop(0, n, exact_body, 0)

    for h in heads:
        acc = acc_sc[h]
        out_sc[h * V_HEAD_DIM:(h + 1) * V_HEAD_DIM, :] = acc[:V_HEAD_DIM] / acc[ONES_ROW:ONES_ROW + 1, :]
    o_ref[...] = out_sc[...].T.astype(BF16)


def _attention(q, kc, vc, kx, vx, tq, tk):
    bsz, sq, _ = q.shape
    n_ctx = kc.shape[1]
    hw = HEADS_PER_STEP * HEAD_PAD
    qspec = pl.BlockSpec((None, tq, hw), lambda b, h, i: (b, i, h))
    kspec = lambda n: pl.BlockSpec((None, n, hw), lambda b, h, i: (b, 0, h))
    vspec = lambda v: pl.BlockSpec((None, HEADS_PER_STEP) + v.shape[2:], lambda b, h, i: (b, h, 0, 0, 0))
    assert vc.shape[2:] == (1, V_ROWS, n_ctx)
    args = [q, kc, vc]
    in_specs = [qspec, kspec(n_ctx), vspec(vc)]
    n_x_blocks = 0
    out_w = HEADS_PER_STEP * V_HEAD_DIM
    per_head = lambda rows: pltpu.VMEM((HEADS_PER_STEP, rows, tq), F32)
    scratch = [per_head(1), per_head(V_ROWS), pltpu.VMEM((out_w, tq), F32)]
    if kx is not None:
        s = kx.shape[1]
        n_x_blocks = s // tk
        assert vx.shape[3] == V_ROWS and tk % vx.shape[4] == 0 and vx.shape[2] * vx.shape[4] == s
        args += [kx, vx]
        in_specs += [kspec(s), vspec(vx)]
        scratch += [per_head(1)]
    return pl.pallas_call(
        functools.partial(_attn_kernel, tk=tk, n_x_blocks=n_x_blocks),
        out_shape=jax.ShapeDtypeStruct((bsz, sq, MLA_HEADS * V_HEAD_DIM), BF16),
        grid=(bsz, MLA_HEADS // HEADS_PER_STEP, sq // tq),
        in_specs=in_specs,
        out_specs=pl.BlockSpec((None, tq, out_w), lambda b, h, i: (b, i, h)),
        scratch_shapes=scratch,
        compiler_params=_cparams(3),
        name="attn_x" if kx is not None else "attn_ctx",
    )(*args)


def _fft1_kernel(z_ref, m_ref, o_ref, *, g, n1):
    for j in range(g):
        zz = jnp.concatenate([z_ref[0, :, j, :], z_ref[1, :, j, :]], axis=0).astype(BF16)
        r = _dot(m_ref[j], zz)
        o_ref[0, j] = r[:n1]
        o_ref[1, j] = r[n1:]


def _fft2_kernel(w_ref, x_ref, o_ref, *, g, scale):
    for j in range(g):
        xx = jnp.concatenate([x_ref[0, :, j, :], x_ref[1, :, j, :]], axis=0).astype(BF16)
        o_ref[:, j, :] = _dot(w_ref[...], xx) * scale


def _dft_direct_kernel(w_ref, x_ref, o_ref, *, scale):
    o_ref[...] = _dot(w_ref[...], x_ref[...].astype(BF16)) * scale


def _dft_cos_sin(rows, cols, period):
    r = (rows[:, None] * cols[None, :]) % period
    ang = r.astype(F32) * (2.0 * math.pi / period)
    return jnp.cos(ang), jnp.sin(ang)


def _fourier_seq(z, consts):
    bsz, _, s, c = z.shape
    scale = 1.0 / math.sqrt(s * F_GROUP_DIM)
    if s <= 2 * FFT_N2:
        w = consts["w_direct"]
        return pl.pallas_call(
            functools.partial(_dft_direct_kernel, scale=scale),
            out_shape=jax.ShapeDtypeStruct((bsz, s, c), F32),
            grid=(bsz,),
            in_specs=[_const_spec(w.shape), pl.BlockSpec((None, 2 * s, c), lambda b: (b, 0, 0))],
            out_specs=pl.BlockSpec((None, s, c), lambda b: (b, 0, 0)),
            compiler_params=_cparams(1),
            name="dft_direct",
        )(w, z.reshape(bsz, 2 * s, c))
    n1, n2 = s // FFT_N2, FFT_N2
    g = 8
    x2 = pl.pallas_call(
        functools.partial(_fft1_kernel, g=g, n1=n1),
        out_shape=jax.ShapeDtypeStruct((bsz, 2, n2, n1, c), F32),
        grid=(bsz, n2 // g),
        in_specs=[
            pl.BlockSpec((None, 2, n1, g, c), lambda b, j: (b, 0, 0, j, 0)),
            pl.BlockSpec((g, 2 * n1, 2 * n1), lambda b, j: (j, 0, 0)),
        ],
        out_specs=pl.BlockSpec((None, 2, g, n1, c), lambda b, j: (b, 0, j, 0, 0)),
        compiler_params=_cparams(2),
        name="fft_stage1",
    )(z.reshape(bsz, 2, n1, n2, c), consts["m1"])
    w2 = consts["w2"]
    y = pl.pallas_call(
        functools.partial(_fft2_kernel, g=g, scale=scale),
        out_shape=jax.ShapeDtypeStruct((bsz, n2, n1, c), F32),
        grid=(bsz, n1 // g),
        in_specs=[
            _const_spec(w2.shape),
            pl.BlockSpec((None, 2, n2, g, c), lambda b, j: (b, 0, 0, j, 0)),
        ],
        out_specs=pl.BlockSpec((None, n2, g, c), lambda b, j: (b, 0, j, 0)),
        compiler_params=_cparams(2),
        name="fft_stage2",
    )(w2, x2)
    return y.reshape(bsz, s, c)


def _conv_kernel(uc_ref, up_ref, un_ref, wdw_ref, bdw_ref, lng_ref, lnb_ref, wpw_ref, o_ref, ext_sc, sh_sc,
                 *, tm, rc):
    i = pl.program_id(1)
    last = pl.num_programs(1) - 1
    ext_sc[0:CONV_HALO] = jnp.where(i > 0, up_ref[...].astype(F32), 0.0)
    ext_sc[CONV_HALO:CONV_HALO + tm] = uc_ref[...].astype(F32)
    ext_sc[CONV_HALO + tm:2 * CONV_HALO + tm] = jnp.where(i < last, un_ref[...].astype(F32), 0.0)
    first = CONV_HALO - CONV_KERNEL // 2
    sub = sh_sc.shape[0]
    for r in range(sub):
        sh_sc[r] = ext_sc[r:r + sh_sc.shape[1], :]
    for r0 in range(0, tm, rc):
        acc = jnp.zeros((rc, CONV_WIDTH), F32)
        for k in range(CONV_KERNEL):
            r = (first + k) % sub
            lo = r0 + (first + k) - r
            acc = acc + sh_sc[r, lo:lo + rc, :] * wdw_ref[k:k + 1, :]
        acc = acc + bdw_ref[...]
        mu = jnp.mean(acc, axis=-1, keepdims=True)
        xc = acc - mu
        var = jnp.mean(xc * xc, axis=-1, keepdims=True)
        y = xc * lax.rsqrt(var + EPS) * lng_ref[...] + lnb_ref[...]
        y = y * jax.nn.sigmoid(y)
        o_ref[r0:r0 + rc, :] = _dot(y.astype(BF16), wpw_ref[...]).astype(BF16)


def _conv_module(u, w, tm):
    bsz, s, c = u.shape
    hb = tm // CONV_HALO
    n_halo = s // CONV_HALO
    return pl.pallas_call(
        functools.partial(_conv_kernel, tm=tm, rc=64),
        out_shape=jax.ShapeDtypeStruct((bsz, s, c), BF16),
        grid=(bsz, s // tm),
        in_specs=[
            pl.BlockSpec((None, tm, c), lambda b, i: (b, i, 0)),
            pl.BlockSpec((None, CONV_HALO, c), lambda b, i: (b, jnp.maximum(i * hb - 1, 0), 0)),
            pl.BlockSpec((None, CONV_HALO, c), lambda b, i: (b, jnp.minimum((i + 1) * hb, n_halo - 1), 0)),
            _const_spec((32, c)), _const_spec((1, c)), _const_spec((1, c)), _const_spec((1, c)),
            _const_spec((c, c)),
        ],
        out_specs=pl.BlockSpec((None, tm, c), lambda b, i: (b, i, 0)),
        scratch_shapes=[pltpu.VMEM((tm + 2 * CONV_HALO, c), F32),
                        pltpu.VMEM((8, tm + 2 * CONV_HALO - 8, c), F32)],
        compiler_params=_cparams(2),
        name="conv",
    )(u, u, u, w["w_dw"], w["b_dw"], w["ln_g"], w["ln_b"], w["w_pw2"])


def _mix_mlp_kernel(x_ref, gate1_ref, shift_ref, scale_ref, gate2_ref, yf_ref, at_ref, yc_ref,
                    wf_ref, wof_ref, woa_ref, woc_ref, g2_ref, w1_ref, w2_ref, gf_ref, o_ref,
                    *, ff_chunk, final_norm):
    yf = _dot(yf_ref[...].astype(BF16), wf_ref[...]).astype(BF16)
    mix = _dot(yf, wof_ref[...]) + _dot(at_ref[...], woa_ref[...]) + _dot(yc_ref[...], woc_ref[...])
    x = x_ref[...] + gate1_ref[...] * mix
    h = (_rms(x, g2_ref[...]) * (1.0 + scale_ref[...]) + shift_ref[...]).astype(BF16)
    d_ff = w1_ref.shape[1]
    acc = jnp.zeros(x.shape, F32)
    for c0 in range(0, d_ff, ff_chunk):
        a = jnp.maximum(_dot(h, w1_ref[:, c0:c0 + ff_chunk]), 0.0)
        acc = acc + _dot((a * a).astype(BF16), w2_ref[c0:c0 + ff_chunk, :])
    y = x + gate2_ref[...] * acc
    if final_norm:
        y = _rms(y, gf_ref[...])
    o_ref[...] = y


def _mix_mlp(x, mods, mod_row, yf, attn, yc, w, mlp_w, layer, gf, tm, final_norm):
    bsz, s, d = x.shape
    w_mlp1, w_mlp2 = mlp_w
    d_ff = w_mlp1.shape[2]
    layer_spec = lambda r, c: pl.BlockSpec((None, r, c), lambda b, i: (layer, 0, 0), pipeline_mode=pl.Buffered(1))
    row = lambda j: pl.BlockSpec((None, None, 1, d), lambda b, i: (mod_row(b), j, 0, 0))
    tok = lambda width: pl.BlockSpec((None, tm, width), lambda b, i: (b, i, 0))
    return pl.pallas_call(
        functools.partial(_mix_mlp_kernel, ff_chunk=1024, final_norm=final_norm),
        out_shape=jax.ShapeDtypeStruct((bsz, s, d), F32),
        grid=(bsz, s // tm),
        in_specs=[
            tok(d), row(2), row(3), row(4), row(5),
            tok(F_WIDTH), tok(MLA_HEADS * V_HEAD_DIM), tok(CONV_WIDTH),
            _const_spec((F_WIDTH, F_WIDTH)), _const_spec((F_WIDTH, d)), _const_spec((MLA_HEADS * V_HEAD_DIM, d)),
            _const_spec((CONV_WIDTH, d)),
            _const_spec((1, d)), layer_spec(d, d_ff), layer_spec(d_ff, d), _const_spec((1, d)),
        ],
        out_specs=tok(d),
        compiler_params=_cparams(2),
        name="mix_mlp",
    )(x, mods, mods, mods, mods, yf, attn, yc, w["w_f"], w["wo_f"], w["wo_a"], w["wo_c"],
      w["g2"], w_mlp1, w_mlp2, gf)


def _rope_tables(s):
    t = jnp.arange(s)
    row = (t // GRID_W).astype(F32)
    col = (t % GRID_W).astype(F32)
    inv = ROPE_BASE ** (-jnp.arange(0, ROPE_AXIS_DIM, 2, dtype=F32) / ROPE_AXIS_DIM)
    ang = jnp.concatenate([row[:, None] * inv, col[:, None] * inv], axis=-1)
    cos, sin = jnp.cos(ang), jnp.sin(ang)
    z_lo = jnp.zeros((s, QK_NOPE_DIM), F32)
    z_hi = jnp.zeros((s, HEAD_PAD - QK_NOPE_DIM - QK_ROPE_DIM), F32)
    ta = jnp.concatenate([z_lo, cos, cos, z_hi], axis=-1)
    tb = jnp.concatenate([z_lo, -sin, sin, z_hi], axis=-1)
    return ta, tb


def _identity_rope_tables(s):
    lane = np.arange(HEAD_PAD)
    ta = ((lane >= QK_NOPE_DIM) & (lane < QK_NOPE_DIM + QK_ROPE_DIM)).astype(np.float32)
    return jnp.asarray(np.tile(ta, (s, 1))), jnp.zeros((s, HEAD_PAD), F32)


def _fourier_consts(s, n_ctx):
    ar = lambda n: jnp.arange(n, dtype=jnp.int32)
    cc, sc = _dft_cos_sin(ar(F_GROUP_DIM), ar(F_GROUP_DIM), F_GROUP_DIM)
    eye = jnp.eye(F_GROUPS, dtype=F32)
    dft_c = jnp.concatenate([jnp.kron(eye, cc), -jnp.kron(eye, sc)], axis=1).astype(BF16)
    consts = {"dft_c": dft_c}

    def direct(n):
        c, sn = _dft_cos_sin(ar(n), ar(n), n)
        return jnp.concatenate([c, sn], axis=1).astype(BF16)

    def two_stage(n):
        n1, n2 = n // FFT_N2, FFT_N2
        pos = ar(n1)[None, :] * n2 + ar(n2)[:, None]
        r = (ar(n1)[None, :, None] * pos[:, None, :]) % n
        ang = r.astype(F32) * (2.0 * math.pi / n)
        c, sn = jnp.cos(ang), jnp.sin(ang)
        m1 = jnp.concatenate([jnp.concatenate([c, sn], axis=2),
                              jnp.concatenate([-sn, c], axis=2)], axis=1).astype(BF16)
        return m1, direct(n2)

    for n, tag in ((s, "x"), (n_ctx, "ctx")):
        if n <= 2 * FFT_N2:
            consts[tag] = {"w_direct": direct(n)}
        else:
            m1, w2 = two_stage(n)
            consts[tag] = {"m1": m1, "w2": w2}
    return consts


def _pad_heads(w, width):
    k = w.shape[0]
    w = w.reshape(k, MLA_HEADS, width)
    return jnp.pad(w, ((0, 0), (0, 0), (0, HEAD_PAD - width))).reshape(k, HP)


def _layer_weights(l, p, dft_c):
    d = p["w_in"].shape[1]
    w_in = p["w_in"][l]
    s2 = F_WIDTH + Q_LORA_RANK + KV_LORA_RANK
    s3 = s2 + QK_ROPE_DIM
    half = QK_ROPE_DIM // 2
    w_kr = w_in[:, s2:s3]
    w_kr_sw = jnp.concatenate([w_kr[:, half:], w_kr[:, :half]], axis=1)
    lo = jnp.zeros((d, QK_NOPE_DIM), F32)
    hi = jnp.zeros((d, HEAD_PAD - QK_NOPE_DIM - QK_ROPE_DIM), F32)
    w1 = jnp.concatenate([w_in[:, :s2], lo, w_kr, hi, lo, w_kr_sw, hi, w_in[:, s3:]], axis=1).astype(BF16)

    qd = QK_NOPE_DIM + QK_ROPE_DIM
    w_uq = p["w_uq"][l].reshape(Q_LORA_RANK, MLA_HEADS, qd)
    rope = w_uq[:, :, QK_NOPE_DIM:]
    rope_sw = jnp.concatenate([rope[:, :, half:], rope[:, :, :half]], axis=2)
    wqa = _pad_heads(w_uq.reshape(Q_LORA_RANK, MLA_HEADS * qd), qd).astype(BF16)
    wqb = jnp.pad(rope_sw, ((0, 0), (0, 0), (QK_NOPE_DIM, HEAD_PAD - qd))).reshape(Q_LORA_RANK, HP).astype(BF16)

    w_ukv = p["w_ukv"][l].reshape(KV_LORA_RANK, MLA_HEADS, QK_NOPE_DIM + V_HEAD_DIM)
    wk = _pad_heads(w_ukv[:, :, :QK_NOPE_DIM].reshape(KV_LORA_RANK, -1), QK_NOPE_DIM).astype(BF16)
    wv = jnp.pad(w_ukv[:, :, QK_NOPE_DIM:], ((0, 0), (0, 0), (0, V_ROWS - V_HEAD_DIM)))
    wv = wv.reshape(KV_LORA_RANK, MLA_HEADS * V_ROWS).T.astype(BF16)
    vone = np.zeros((MLA_HEADS * V_ROWS, 1), np.float32)
    vone[ONES_ROW::V_ROWS, 0] = 1.0

    w_o = p["w_o"][l]
    a0, a1 = F_WIDTH, F_WIDTH + MLA_HEADS * V_HEAD_DIM
    w_dw = jnp.pad(p["w_dw"][l], ((0, 32 - CONV_KERNEL), (0, 0)))
    r2 = lambda v: v.reshape(1, -1)
    return {
        "w1": w1, "qg": r2(p["q_norm_g"][l]), "wqa": wqa, "wqb": wqb,
        "kvg": r2(p["kv_norm_g"][l]), "wk": wk, "wv": wv, "dft_c": dft_c, "vone": jnp.asarray(vone),
        "w_dw": w_dw, "b_dw": r2(p["b_dw"][l]), "ln_g": r2(p["conv_ln_g"][l]), "ln_b": r2(p["conv_ln_b"][l]),
        "w_pw2": p["w_pw2"][l].astype(BF16),
        "w_f": p["w_fourier"][l].astype(BF16), "wo_f": w_o[:a0].astype(BF16), "wo_a": w_o[a0:a1].astype(BF16),
        "wo_c": w_o[a1:].astype(BF16),
        "g1": r2(p["norm1_g"][l]), "g2": r2(p["norm2_g"][l]),
    }


def _token_tile(s):
    for tm in (512, 256, 128):
        if s % tm == 0:
            return tm
    raise ValueError(f"sequence length {s} must be a multiple of 128")


def kernel(x, c, ctx, c_ctx, w_mod, b_mod, norm1_g, w_in, q_norm_g, w_uq, kv_norm_g, w_ukv, w_fourier,
           w_dw, b_dw, conv_ln_g, conv_ln_b, w_pw2, w_o, norm2_g, w_mlp1, w_mlp2, final_norm_g):
    bsz, s, d = x.shape
    n_ctx = ctx.shape[1]
    depth = w_mod.shape[0]
    assert bsz < 8 and s % GRID_W == 0 and s % LANES == 0 and n_ctx % LANES == 0
    p = dict(w_in=w_in, w_uq=w_uq, w_ukv=w_ukv, q_norm_g=q_norm_g, kv_norm_g=kv_norm_g,
             w_fourier=w_fourier, w_dw=w_dw, b_dw=b_dw, conv_ln_g=conv_ln_g, conv_ln_b=conv_ln_b,
             w_pw2=w_pw2, w_o=w_o, norm1_g=norm1_g, norm2_g=norm2_g)
    mlp_w = (w_mlp1.astype(BF16), w_mlp2.astype(BF16))

    c_rows = jnp.zeros((8, d), F32).at[:bsz].set(c).at[bsz].set(c_ctx)
    mods_all = _modulation(c_rows, w_mod, b_mod).reshape(depth, 8, N_MOD, 1, d)
    x_row = lambda b: b
    ctx_row = lambda b: bsz

    tm_x, tm_c = _token_tile(s), _token_tile(n_ctx)
    tq = next(t for t in (ATTN_QUERY_TILE, tm_x) if s % t == 0)
    tm_mlp = next(t for t in (MLP_TILE, tm_x) if s % t == 0)
    tk = next(t for t in (ATTN_KEY_TILE, tm_x) if s % t == 0 and t % tm_x == 0)
    ta_x, tb_x = _rope_tables(s)
    ta_c, tb_c = _identity_rope_tables(n_ctx)
    fc = _fourier_consts(s, n_ctx)
    gf = final_norm_g.reshape(1, d)

    for l in range(depth):
        last = l == depth - 1
        w = _layer_weights(l, p, fc["dft_c"])
        mods = mods_all[l]

        qx, kx, vx, zx, ux = _prep(x, mods, x_row, w["g1"], w, ta_x, tb_x, tm_x)
        qc, kc, vc, zc, uc = _prep(ctx, mods, ctx_row, w["g1"], w, ta_c, tb_c, tm_c)

        attn_x = _attention(qx, kc, vc, kx, vx, tq, tk)
        yf_x = _fourier_seq(zx, fc["x"])
        yc_x = _conv_module(ux, w, tm_x)
        x = _mix_mlp(x, mods, x_row, yf_x, attn_x, yc_x, w, mlp_w, l, gf, tm_mlp, final_norm=last)

        if not last:
            attn_c = _attention(qc, kc, vc, None, None, tm_c, tk)
            yf_c = _fourier_seq(zc, fc["ctx"])
            yc_c = _conv_module(uc, w, tm_c)
            ctx = _mix_mlp(ctx, mods, ctx_row, yf_c, attn_c, yc_c, w, mlp_w, l, gf, tm_c, final_norm=False)
    return x
```

```python
import functools
import math

import numpy as np
import jax
import jax.numpy as jnp
from jax import lax
from jax.experimental import pallas as pl
from jax.experimental.pallas import tpu as pltpu

F32 = jnp.float32
BF16 = jnp.bfloat16

GRID_W = 64
F_GROUPS = 4
F_GROUP_DIM = 64
F_WIDTH = F_GROUPS * F_GROUP_DIM
MLA_HEADS = 8
QK_NOPE_DIM = 64
QK_ROPE_DIM = 32
V_HEAD_DIM = 64
Q_LORA_RANK = 256
KV_LORA_RANK = 128
CONV_WIDTH = 256
CONV_KERNEL = 31
N_MOD = 6
ROPE_BASE = 10000.0
ROPE_AXIS_DIM = QK_ROPE_DIM // 2
ATTN_SCALE = (QK_NOPE_DIM + QK_ROPE_DIM) ** -0.5
EPS = 1e-6

LANES = 128
HEAD_PAD = LANES
HP = MLA_HEADS * HEAD_PAD
ONES_ROW = V_HEAD_DIM
HEADS_PER_STEP = 2
V_ROWS = 80
CONV_HALO = 16
FFT_N2 = 128
VMEM_LIMIT = 56 * 1024 * 1024
TILES_PER_PASS = 4
LAZY_HEADROOM = 60.0
ATTN_KEY_TILE = 512
ATTN_QUERY_TILE = 4096
MLP_TILE = 1024
Q_SCALE = ATTN_SCALE * math.log2(math.e)


def _cparams(n_axes, flags=None):
    return pltpu.CompilerParams(dimension_semantics=("parallel",) * n_axes,
                                vmem_limit_bytes=VMEM_LIMIT, flags=flags)


def _const_spec(shape):
    nd = len(shape)
    return pl.BlockSpec(shape, lambda *_: (0,) * nd, pipeline_mode=pl.Buffered(1))


def _dot(a, b):
    return jnp.dot(a, b, preferred_element_type=F32)


def _split_bf16(v):
    hi = v.astype(BF16)
    lo = (v - hi.astype(F32)).astype(BF16)
    return hi, lo


def _mod_kernel(c_ref, w_ref, b_ref, o_ref):
    c = c_ref[...]
    s = c * jax.nn.sigmoid(c)
    s_hi, s_lo = _split_bf16(s)
    w_hi, w_lo = _split_bf16(w_ref[...])
    o_ref[...] = _dot(s_hi, w_hi) + _dot(s_hi, w_lo) + _dot(s_lo, w_hi) + b_ref[...]


def _modulation(c_rows, w_mod, b_mod):
    n_layers, d, n6 = w_mod.shape
    tn = 1536
    return pl.pallas_call(
        _mod_kernel,
        out_shape=jax.ShapeDtypeStruct((n_layers, 8, n6), F32),
        grid=(n_layers, n6 // tn),
        in_specs=[
            pl.BlockSpec((8, d), lambda l, j: (0, 0)),
            pl.BlockSpec((None, d, tn), lambda l, j: (l, 0, j)),
            pl.BlockSpec((None, 1, tn), lambda l, j: (l, 0, j)),
        ],
        out_specs=pl.BlockSpec((None, 8, tn), lambda l, j: (l, 0, j)),
        compiler_params=_cparams(2),
        name="modulation",
    )(c_rows, w_mod, b_mod.reshape(n_layers, 1, n6))


def _rms(v, g):
    return v * lax.rsqrt(jnp.mean(v * v, axis=-1, keepdims=True) + EPS) * g


def _prep_kernel(x_ref, shift_ref, scale_ref, g1_ref, w1_ref, qg_ref, wqa_ref, wqb_ref,
                 kvg_ref, wk_ref, wv_ref, dft_ref, ta_ref, tb_ref, vone_ref,
                 q_ref, k_ref, v_ref, z_ref, u_ref):
    x = x_ref[...]
    h = (_rms(x, g1_ref[...]) * (1.0 + scale_ref[...]) + shift_ref[...]).astype(BF16)
    z = _dot(h, w1_ref[...])

    zz = _dot(z[:, 0:F_WIDTH].astype(BF16), dft_ref[...])
    z_ref[0] = zz[:, :F_WIDTH]
    z_ref[1] = zz[:, F_WIDTH:]

    ta = ta_ref[...]
    tb = tb_ref[...]
    lane = lax.broadcasted_iota(jnp.int32, (1, HEAD_PAD), 1)
    nope = (lane < QK_NOPE_DIM).astype(F32)

    c0 = F_WIDTH
    cqn = _rms(z[:, c0:c0 + Q_LORA_RANK], qg_ref[...]).astype(BF16)
    qa = _dot(cqn, wqa_ref[...])
    qb = _dot(cqn, wqb_ref[...])
    qta = (ta + nope) * Q_SCALE
    qtb = tb * Q_SCALE
    for hd in range(MLA_HEADS):
        sl = slice(hd * HEAD_PAD, (hd + 1) * HEAD_PAD)
        q_ref[:, sl] = (qa[:, sl] * qta + qb[:, sl] * qtb).astype(BF16)

    c1 = c0 + Q_LORA_RANK
    ckvn = _rms(z[:, c1:c1 + KV_LORA_RANK], kvg_ref[...]).astype(BF16)
    kn = _dot(ckvn, wk_ref[...])
    c2 = c1 + KV_LORA_RANK
    kr = z[:, c2:c2 + HEAD_PAD] * ta + z[:, c2 + HEAD_PAD:c2 + 2 * HEAD_PAD] * tb
    for hd in range(MLA_HEADS):
        sl = slice(hd * HEAD_PAD, (hd + 1) * HEAD_PAD)
        k_ref[:, sl] = (kn[:, sl] + kr).astype(BF16)
    vt = lax.dot_general(wv_ref[...], ckvn, (((1,), (1,)), ((), ())), preferred_element_type=F32)
    vt = vt + vone_ref[...]
    for hd in range(MLA_HEADS):
        v_ref[hd] = vt[hd * V_ROWS:(hd + 1) * V_ROWS, :].astype(BF16)

    c3 = c2 + 2 * HEAD_PAD
    a = z[:, c3:c3 + CONV_WIDTH]
    g = z[:, c3 + CONV_WIDTH:c3 + 2 * CONV_WIDTH]
    u_ref[...] = (a * jax.nn.sigmoid(g)).astype(BF16)


def _prep(x, mods, mod_row, g1, w, ta, tb, tm):
    bsz, s, d = x.shape
    n1 = w["w1"].shape[1]
    row = lambda j: pl.BlockSpec((None, None, 1, d), lambda b, i: (mod_row(b), j, 0, 0))
    tok = lambda width: pl.BlockSpec((None, tm, width), lambda b, i: (b, i, 0))
    return pl.pallas_call(
        _prep_kernel,
        out_shape=(
            jax.ShapeDtypeStruct((bsz, s, HP), BF16),
            jax.ShapeDtypeStruct((bsz, s, HP), BF16),
            jax.ShapeDtypeStruct((bsz, MLA_HEADS, s // tm, V_ROWS, tm), BF16),
            jax.ShapeDtypeStruct((bsz, 2, s, F_WIDTH), F32),
            jax.ShapeDtypeStruct((bsz, s, CONV_WIDTH), BF16),
        ),
        grid=(bsz, s // tm),
        in_specs=[
            tok(d), row(0), row(1), _const_spec((1, d)),
            _const_spec((d, n1)),
            _const_spec((1, Q_LORA_RANK)), _const_spec((Q_LORA_RANK, HP)), _const_spec((Q_LORA_RANK, HP)),
            _const_spec((1, KV_LORA_RANK)), _const_spec((KV_LORA_RANK, HP)),
            _const_spec((MLA_HEADS * V_ROWS, KV_LORA_RANK)),
            _const_spec((F_WIDTH, 2 * F_WIDTH)),
            pl.BlockSpec((tm, HEAD_PAD), lambda b, i: (i, 0)),
            pl.BlockSpec((tm, HEAD_PAD), lambda b, i: (i, 0)),
            _const_spec((MLA_HEADS * V_ROWS, 1)),
        ],
        out_specs=(
            tok(HP), tok(HP),
            pl.BlockSpec((None, MLA_HEADS, None, V_ROWS, tm), lambda b, i: (b, 0, i, 0, 0)),
            pl.BlockSpec((None, 2, tm, F_WIDTH), lambda b, i: (b, 0, i, 0)),
            tok(CONV_WIDTH),
        ),
        compiler_params=_cparams(2),
        name="prep",
    )(x, mods, mods, g1, w["w1"], w["qg"], w["wqa"], w["wqb"], w["kvg"], w["wk"], w["wv"],
      w["dft_c"], ta, tb, w["vone"])


def _attn_kernel(*refs, tk, n_x_blocks):
    if n_x_blocks:
        q_ref, kc_ref, vc_ref, kx_ref, vx_ref, o_ref, m_sc, acc_sc, out_sc, ex_sc = refs
    else:
        q_ref, kc_ref, vc_ref, o_ref, m_sc, acc_sc, out_sc = refs
    heads = range(HEADS_PER_STEP)
    lanes = [slice(h * HEAD_PAD, (h + 1) * HEAD_PAD) for h in heads]
    qt = [q_ref[:, lanes[h]].astype(F32).T.astype(BF16) for h in heads]

    def exact_tile(h, k, vt, first=False):
        s = _dot(k, qt[h])
        m_tile = jnp.max(s, axis=0, keepdims=True)
        m_new = m_tile if first else jnp.maximum(m_sc[h], m_tile)
        p = jnp.exp2(s - m_new).astype(BF16)
        pv = _dot(vt, p)
        acc_sc[h] = pv if first else jnp.exp2(m_sc[h] - m_new) * acc_sc[h] + pv
        m_sc[h] = m_new

    for h in heads:
        exact_tile(h, kc_ref[:, lanes[h]], vc_ref[h, 0], first=True)
    if n_x_blocks:
        n = n_x_blocks
        v_sub = tk // vx_ref.shape[3]

        def k_tile(h, t):
            return kx_ref[pl.ds(pl.multiple_of(t * tk, tk), tk), lanes[h]]

        def v_tile(h, t):
            return jnp.concatenate([vx_ref[h, v_sub * t + i] for i in range(v_sub)], axis=1)

        def lazy_tile(h, t):
            s = _dot(k_tile(h, t), qt[h])
            m_old = m_sc[h]
            p = jnp.exp2(s - m_old).astype(BF16)
            m_tile = jnp.max(s, axis=0, keepdims=True)
            m_new = jnp.maximum(m_old, m_tile)
            acc_sc[h] = (acc_sc[h] + _dot(v_tile(h, t), p)) * jnp.exp2(m_old - m_new)
            m_sc[h] = m_new
            ex_sc[h] = jnp.maximum(ex_sc[h], m_tile - m_old)

        ex_sc[...] = jnp.zeros(ex_sc.shape, F32)
        n_peel = n % TILES_PER_PASS
        for t in range(n_peel):
            for h in heads:
                lazy_tile(h, t)

        def body(j, carry):
            t0 = n_peel + TILES_PER_PASS * j
            for u in range(TILES_PER_PASS):
                for h in heads:
                    lazy_tile(h, t0 + u)
            return carry
        lax.fori_loop(0, (n - n_peel) // TILES_PER_PASS, body, 0)

        for h in heads:
            @pl.when(jnp.max(ex_sc[h]) > LAZY_HEADROOM)
            def _(h=h):
                exact_tile(h, kc_ref[:, lanes[h]], vc_ref[h, 0], first=True)

                def exact_body(t, carry):
                    exact_tile(h, k_tile(h, t), v_tile(h, t))
                    return carry
                lax.fori_loop(0, n, exact_body, 0)

    for h in heads:
        acc = acc_sc[h]
        out_sc[h * V_HEAD_DIM:(h + 1) * V_HEAD_DIM, :] = acc[:V_HEAD_DIM] / acc[ONES_ROW:ONES_ROW + 1, :]
    o_ref[...] = out_sc[...].T.astype(BF16)


def _attention(q, kc, vc, kx, vx, tq, tk):
    bsz, sq, _ = q.shape
    n_ctx = kc.shape[1]
    hw = HEADS_PER_STEP * HEAD_PAD
    qspec = pl.BlockSpec((None, tq, hw), lambda b, h, i: (b, i, h))
    kspec = lambda n: pl.BlockSpec((None, n, hw), lambda b, h, i: (b, 0, h))
    vspec = lambda v: pl.BlockSpec((None, HEADS_PER_STEP) + v.shape[2:], lambda b, h, i: (b, h, 0, 0, 0))
    assert vc.shape[2:] == (1, V_ROWS, n_ctx)
    args = [q, kc, vc]
    in_specs = [qspec, kspec(n_ctx), vspec(vc)]
    n_x_blocks = 0
    out_w = HEADS_PER_STEP * V_HEAD_DIM
    per_head = lambda rows: pltpu.VMEM((HEADS_PER_STEP, rows, tq), F32)
    scratch = [per_head(1), per_head(V_ROWS), pltpu.VMEM((out_w, tq), F32)]
    if kx is not None:
        s = kx.shape[1]
        n_x_blocks = s // tk
        assert vx.shape[3] == V_ROWS and tk % vx.shape[4] == 0 and vx.shape[2] * vx.shape[4] == s
        args += [kx, vx]
        in_specs += [kspec(s), vspec(vx)]
        scratch += [per_head(1)]
    return pl.pallas_call(
        functools.partial(_attn_kernel, tk=tk, n_x_blocks=n_x_blocks),
        out_shape=jax.ShapeDtypeStruct((bsz, sq, MLA_HEADS * V_HEAD_DIM), BF16),
        grid=(bsz, MLA_HEADS // HEADS_PER_STEP, sq // tq),
        in_specs=in_specs,
        out_specs=pl.BlockSpec((None, tq, out_w), lambda b, h, i: (b, i, h)),
        scratch_shapes=scratch,
        compiler_params=_cparams(3),
        name="attn_x" if kx is not None else "attn_ctx",
    )(*args)


def _fft1_kernel(z_ref, m_ref, o_ref, *, g, n1):
    for j in range(g):
        zz = jnp.concatenate([z_ref[0, :, j, :], z_ref[1, :, j, :]], axis=0).astype(BF16)
        r = _dot(m_ref[j], zz)
        o_ref[0, j] = r[:n1]
        o_ref[1, j] = r[n1:]


def _fft2_kernel(w_ref, x_ref, o_ref, *, g, scale):
    for j in range(g):
        xx = jnp.concatenate([x_ref[0, :, j, :], x_ref[1, :, j, :]], axis=0).astype(BF16)
        o_ref[:, j, :] = _dot(w_ref[...], xx) * scale


def _dft_direct_kernel(w_ref, x_ref, o_ref, *, scale):
    o_ref[...] = _dot(w_ref[...], x_ref[...].astype(BF16)) * scale


def _dft_cos_sin(rows, cols, period):
    r = (rows[:, None] * cols[None, :]) % period
    ang = r.astype(F32) * (2.0 * math.pi / period)
    return jnp.cos(ang), jnp.sin(ang)


def _fourier_seq(z, consts):
    bsz, _, s, c = z.shape
    scale = 1.0 / math.sqrt(s * F_GROUP_DIM)
    if s <= 2 * FFT_N2:
        w = consts["w_direct"]
        return pl.pallas_call(
            functools.partial(_dft_direct_kernel, scale=scale),
            out_shape=jax.ShapeDtypeStruct((bsz, s, c), F32),
            grid=(bsz,),
            in_specs=[_const_spec(w.shape), pl.BlockSpec((None, 2 * s, c), lambda b: (b, 0, 0))],
            out_specs=pl.BlockSpec((None, s, c), lambda b: (b, 0, 0)),
            compiler_params=_cparams(1),
            name="dft_direct",
        )(w, z.reshape(bsz, 2 * s, c))
    n1, n2 = s // FFT_N2, FFT_N2
    g = 16
    x2 = pl.pallas_call(
        functools.partial(_fft1_kernel, g=g, n1=n1),
        out_shape=jax.ShapeDtypeStruct((bsz, 2, n2, n1, c), F32),
        grid=(bsz, n2 // g),
        in_specs=[
            pl.BlockSpec((None, 2, n1, g, c), lambda b, j: (b, 0, 0, j, 0)),
            pl.BlockSpec((g, 2 * n1, 2 * n1), lambda b, j: (j, 0, 0)),
        ],
        out_specs=pl.BlockSpec((None, 2, g, n1, c), lambda b, j: (b, 0, j, 0, 0)),
        compiler_params=_cparams(2),
        name="fft_stage1",
    )(z.reshape(bsz, 2, n1, n2, c), consts["m1"])
    w2 = consts["w2"]
    y = pl.pallas_call(
        functools.partial(_fft2_kernel, g=g, scale=scale),
        out_shape=jax.ShapeDtypeStruct((bsz, n2, n1, c), F32),
        grid=(bsz, n1 // g),
        in_specs=[
            _const_spec(w2.shape),
            pl.BlockSpec((None, 2, n2, g, c), lambda b, j: (b, 0, 0, j, 0)),
        ],
        out_specs=pl.BlockSpec((None, n2, g, c), lambda b, j: (b, 0, j, 0)),
        compiler_params=_cparams(2),
        name="fft_stage2",
    )(w2, x2)
    return y.reshape(bsz, s, c)


def _conv_kernel(uc_ref, up_ref, un_ref, wdw_ref, bdw_ref, lng_ref, lnb_ref, wpw_ref, o_ref, ext_sc, sh_sc,
                 *, tm, rc):
    i = pl.program_id(1)
    last = pl.num_programs(1) - 1
    ext_sc[0:CONV_HALO] = jnp.where(i > 0, up_ref[...].astype(F32), 0.0)
    ext_sc[CONV_HALO:CONV_HALO + tm] = uc_ref[...].astype(F32)
    ext_sc[CONV_HALO + tm:2 * CONV_HALO + tm] = jnp.where(i < last, un_ref[...].astype(F32), 0.0)
    first = CONV_HALO - CONV_KERNEL // 2
    sub = sh_sc.shape[0]
    for r in range(sub):
        sh_sc[r] = ext_sc[r:r + sh_sc.shape[1], :]
    for r0 in range(0, tm, rc):
        acc = jnp.zeros((rc, CONV_WIDTH), F32)
        for k in range(CONV_KERNEL):
            r = (first + k) % sub
            lo = r0 + (first + k) - r
            acc = acc + sh_sc[r, lo:lo + rc, :] * wdw_ref[k:k + 1, :]
        acc = acc + bdw_ref[...]
        mu = jnp.mean(acc, axis=-1, keepdims=True)
        xc = acc - mu
        var = jnp.mean(xc * xc, axis=-1, keepdims=True)
        y = xc * lax.rsqrt(var + EPS) * lng_ref[...] + lnb_ref[...]
        y = y * jax.nn.sigmoid(y)
        o_ref[r0:r0 + rc, :] = _dot(y.astype(BF16), wpw_ref[...]).astype(BF16)


def _conv_module(u, w, tm):
    bsz, s, c = u.shape
    hb = tm // CONV_HALO
    n_halo = s // CONV_HALO
    return pl.pallas_call(
        functools.partial(_conv_kernel, tm=tm, rc=64),
        out_shape=jax.ShapeDtypeStruct((bsz, s, c), BF16),
        grid=(bsz, s // tm),
        in_specs=[
            pl.BlockSpec((None, tm, c), lambda b, i: (b, i, 0)),
            pl.BlockSpec((None, CONV_HALO, c), lambda b, i: (b, jnp.maximum(i * hb - 1, 0), 0)),
            pl.BlockSpec((None, CONV_HALO, c), lambda b, i: (b, jnp.minimum((i + 1) * hb, n_halo - 1), 0)),
            _const_spec((32, c)), _const_spec((1, c)), _const_spec((1, c)), _const_spec((1, c)),
            _const_spec((c, c)),
        ],
        out_specs=pl.BlockSpec((None, tm, c), lambda b, i: (b, i, 0)),
        scratch_shapes=[pltpu.VMEM((tm + 2 * CONV_HALO, c), F32),
                        pltpu.VMEM((8, tm + 2 * CONV_HALO - 8, c), F32)],
        compiler_params=_cparams(2),
        name="conv",
    )(u, u, u, w["w_dw"], w["b_dw"], w["ln_g"], w["ln_b"], w["w_pw2"])


def _mix_mlp_kernel(x_ref, gate1_ref, shift_ref, scale_ref, gate2_ref, yf_ref, at_ref, yc_ref,
                    wf_ref, wof_ref, woa_ref, woc_ref, g2_ref, w1_ref, w2_ref, gf_ref, o_ref,
                    *, ff_chunk, final_norm):
    yf = _dot(yf_ref[...].astype(BF16), wf_ref[...]).astype(BF16)
    mix = _dot(yf, wof_ref[...]) + _dot(at_ref[...], woa_ref[...]) + _dot(yc_ref[...], woc_ref[...])
    x = x_ref[...] + gate1_ref[...] * mix
    h = (_rms(x, g2_ref[...]) * (1.0 + scale_ref[...]) + shift_ref[...]).astype(BF16)
    d_ff = w1_ref.shape[1]
    acc = jnp.zeros(x.shape, F32)
    for c0 in range(0, d_ff, ff_chunk):
        a = jnp.maximum(_dot(h, w1_ref[:, c0:c0 + ff_chunk]), 0.0)
        acc = acc + _dot((a * a).astype(BF16), w2_ref[c0:c0 + ff_chunk, :])
    y = x + gate2_ref[...] * acc
    if final_norm:
        y = _rms(y, gf_ref[...])
    o_ref[...] = y


def _mix_mlp(x, mods, mod_row, yf, attn, yc, w, mlp_w, layer, gf, tm, final_norm):
    bsz, s, d = x.shape
    w_mlp1, w_mlp2 = mlp_w
    d_ff = w_mlp1.shape[2]
    layer_spec = lambda r, c: pl.BlockSpec((None, r, c), lambda b, i: (layer, 0, 0), pipeline_mode=pl.Buffered(1))
    row = lambda j: pl.BlockSpec((None, None, 1, d), lambda b, i: (mod_row(b), j, 0, 0))
    tok = lambda width: pl.BlockSpec((None, tm, width), lambda b, i: (b, i, 0))
    return pl.pallas_call(
        functools.partial(_mix_mlp_kernel, ff_chunk=1024, final_norm=final_norm),
        out_shape=jax.ShapeDtypeStruct((bsz, s, d), F32),
        grid=(bsz, s // tm),
        in_specs=[
            tok(d), row(2), row(3), row(4), row(5),
            tok(F_WIDTH), tok(MLA_HEADS * V_HEAD_DIM), tok(CONV_WIDTH),
            _const_spec((F_WIDTH, F_WIDTH)), _const_spec((F_WIDTH, d)), _const_spec((MLA_HEADS * V_HEAD_DIM, d)),
            _const_spec((CONV_WIDTH, d)),
            _const_spec((1, d)), layer_spec(d, d_ff), layer_spec(d_ff, d), _const_spec((1, d)),
        ],
        out_specs=tok(d),
        compiler_params=_cparams(2),
        name="mix_mlp",
    )(x, mods, mods, mods, mods, yf, attn, yc, w["w_f"], w["wo_f"], w["wo_a"], w["wo_c"],
      w["g2"], w_mlp1, w_mlp2, gf)


def _rope_tables(s):
    t = jnp.arange(s)
    row = (t // GRID_W).astype(F32)
    col = (t % GRID_W).astype(F32)
    inv = ROPE_BASE ** (-jnp.arange(0, ROPE_AXIS_DIM, 2, dtype=F32) / ROPE_AXIS_DIM)
    ang = jnp.concatenate([row[:, None] * inv, col[:, None] * inv], axis=-1)
    cos, sin = jnp.cos(ang), jnp.sin(ang)
    z_lo = jnp.zeros((s, QK_NOPE_DIM), F32)
    z_hi = jnp.zeros((s, HEAD_PAD - QK_NOPE_DIM - QK_ROPE_DIM), F32)
    ta = jnp.concatenate([z_lo, cos, cos, z_hi], axis=-1)
    tb = jnp.concatenate([z_lo, -sin, sin, z_hi], axis=-1)
    return ta, tb


def _identity_rope_tables(s):
    lane = np.arange(HEAD_PAD)
    ta = ((lane >= QK_NOPE_DIM) & (lane < QK_NOPE_DIM + QK_ROPE_DIM)).astype(np.float32)
    return jnp.asarray(np.tile(ta, (s, 1))), jnp.zeros((s, HEAD_PAD), F32)


def _fourier_consts(s, n_ctx):
    ar = lambda n: jnp.arange(n, dtype=jnp.int32)
    cc, sc = _dft_cos_sin(ar(F_GROUP_DIM), ar(F_GROUP_DIM), F_GROUP_DIM)
    eye = jnp.eye(F_GROUPS, dtype=F32)
    dft_c = jnp.concatenate([jnp.kron(eye, cc), -jnp.kron(eye, sc)], axis=1).astype(BF16)
    consts = {"dft_c": dft_c}

    def direct(n):
        c, sn = _dft_cos_sin(ar(n), ar(n), n)
        return jnp.concatenate([c, sn], axis=1).astype(BF16)

    def two_stage(n):
        n1, n2 = n // FFT_N2, FFT_N2
        pos = ar(n1)[None, :] * n2 + ar(n2)[:, None]
        r = (ar(n1)[None, :, None] * pos[:, None, :]) % n
        ang = r.astype(F32) * (2.0 * math.pi / n)
        c, sn = jnp.cos(ang), jnp.sin(ang)
        m1 = jnp.concatenate([jnp.concatenate([c, sn], axis=2),
                              jnp.concatenate([-sn, c], axis=2)], axis=1).astype(BF16)
        return m1, direct(n2)

    for n, tag in ((s, "x"), (n_ctx, "ctx")):
        if n <= 2 * FFT_N2:
            consts[tag] = {"w_direct": direct(n)}
        else:
            m1, w2 = two_stage(n)
            consts[tag] = {"m1": m1, "w2": w2}
    return consts


def _pad_heads(w, width):
    k = w.shape[0]
    w = w.reshape(k, MLA_HEADS, width)
    return jnp.pad(w, ((0, 0), (0, 0), (0, HEAD_PAD - width))).reshape(k, HP)


def _layer_weights(l, p, dft_c):
    d = p["w_in"].shape[1]
    w_in = p["w_in"][l]
    s2 = F_WIDTH + Q_LORA_RANK + KV_LORA_RANK
    s3 = s2 + QK_ROPE_DIM
    half = QK_ROPE_DIM // 2
    w_kr = w_in[:, s2:s3]
    w_kr_sw = jnp.concatenate([w_kr[:, half:], w_kr[:, :half]], axis=1)
    lo = jnp.zeros((d, QK_NOPE_DIM), F32)
    hi = jnp.zeros((d, HEAD_PAD - QK_NOPE_DIM - QK_ROPE_DIM), F32)
    w1 = jnp.concatenate([w_in[:, :s2], lo, w_kr, hi, lo, w_kr_sw, hi, w_in[:, s3:]], axis=1).astype(BF16)

    qd = QK_NOPE_DIM + QK_ROPE_DIM
    w_uq = p["w_uq"][l].reshape(Q_LORA_RANK, MLA_HEADS, qd)
    rope = w_uq[:, :, QK_NOPE_DIM:]
    rope_sw = jnp.concatenate([rope[:, :, half:], rope[:, :, :half]], axis=2)
    wqa = _pad_heads(w_uq.reshape(Q_LORA_RANK, MLA_HEADS * qd), qd).astype(BF16)
    wqb = jnp.pad(rope_sw, ((0, 0), (0, 0), (QK_NOPE_DIM, HEAD_PAD - qd))).reshape(Q_LORA_RANK, HP).astype(BF16)

    w_ukv = p["w_ukv"][l].reshape(KV_LORA_RANK, MLA_HEADS, QK_NOPE_DIM + V_HEAD_DIM)
    wk = _pad_heads(w_ukv[:, :, :QK_NOPE_DIM].reshape(KV_LORA_RANK, -1), QK_NOPE_DIM).astype(BF16)
    wv = jnp.pad(w_ukv[:, :, QK_NOPE_DIM:], ((0, 0), (0, 0), (0, V_ROWS - V_HEAD_DIM)))
    wv = wv.reshape(KV_LORA_RANK, MLA_HEADS * V_ROWS).T.astype(BF16)
    vone = np.zeros((MLA_HEADS * V_ROWS, 1), np.float32)
    vone[ONES_ROW::V_ROWS, 0] = 1.0

    w_o = p["w_o"][l]
    a0, a1 = F_WIDTH, F_WIDTH + MLA_HEADS * V_HEAD_DIM
    w_dw = jnp.pad(p["w_dw"][l], ((0, 32 - CONV_KERNEL), (0, 0)))
    r2 = lambda v: v.reshape(1, -1)
    return {
        "w1": w1, "qg": r2(p["q_norm_g"][l]), "wqa": wqa, "wqb": wqb,
        "kvg": r2(p["kv_norm_g"][l]), "wk": wk, "wv": wv, "dft_c": dft_c, "vone": jnp.asarray(vone),
        "w_dw": w_dw, "b_dw": r2(p["b_dw"][l]), "ln_g": r2(p["conv_ln_g"][l]), "ln_b": r2(p["conv_ln_b"][l]),
        "w_pw2": p["w_pw2"][l].astype(BF16),
        "w_f": p["w_fourier"][l].astype(BF16), "wo_f": w_o[:a0].astype(BF16), "wo_a": w_o[a0:a1].astype(BF16),
        "wo_c": w_o[a1:].astype(BF16),
        "g1": r2(p["norm1_g"][l]), "g2": r2(p["norm2_g"][l]),
    }


def _token_tile(s):
    for tm in (512, 256, 128):
        if s % tm == 0:
            return tm
    raise ValueError(f"sequence length {s} must be a multiple of 128")


def kernel(x, c, ctx, c_ctx, w_mod, b_mod, norm1_g, w_in, q_norm_g, w_uq, kv_norm_g, w_ukv, w_fourier,
           w_dw, b_dw, conv_ln_g, conv_ln_b, w_pw2, w_o, norm2_g, w_mlp1, w_mlp2, final_norm_g):
    bsz, s, d = x.shape
    n_ctx = ctx.shape[1]
    depth = w_mod.shape[0]
    assert bsz < 8 and s % GRID_W == 0 and s % LANES == 0 and n_ctx % LANES == 0
    p = dict(w_in=w_in, w_uq=w_uq, w_ukv=w_ukv, q_norm_g=q_norm_g, kv_norm_g=kv_norm_g,
             w_fourier=w_fourier, w_dw=w_dw, b_dw=b_dw, conv_ln_g=conv_ln_g, conv_ln_b=conv_ln_b,
             w_pw2=w_pw2, w_o=w_o, norm1_g=norm1_g, norm2_g=norm2_g)
    mlp_w = (w_mlp1.astype(BF16), w_mlp2.astype(BF16))

    c_rows = jnp.zeros((8, d), F32).at[:bsz].set(c).at[bsz].set(c_ctx)
    mods_all = _modulation(c_rows, w_mod, b_mod).reshape(depth, 8, N_MOD, 1, d)
    x_row = lambda b: b
    ctx_row = lambda b: bsz

    tm_x, tm_c = _token_tile(s), _token_tile(n_ctx)
    tq = next(t for t in (ATTN_QUERY_TILE, tm_x) if s % t == 0)
    tm_mlp = next(t for t in (MLP_TILE, tm_x) if s % t == 0)
    tk = next(t for t in (ATTN_KEY_TILE, tm_x) if s % t == 0 and t % tm_x == 0)
    ta_x, tb_x = _rope_tables(s)
    ta_c, tb_c = _identity_rope_tables(n_ctx)
    fc = _fourier_consts(s, n_ctx)
    gf = final_norm_g.reshape(1, d)

    for l in range(depth):
        last = l == depth - 1
        w = _layer_weights(l, p, fc["dft_c"])
        mods = mods_all[l]

        qx, kx, vx, zx, ux = _prep(x, mods, x_row, w["g1"], w, ta_x, tb_x, tm_x)
        qc, kc, vc, zc, uc = _prep(ctx, mods, ctx_row, w["g1"], w, ta_c, tb_c, tm_c)

        attn_x = _attention(qx, kc, vc, kx, vx, tq, tk)
        yf_x = _fourier_seq(zx, fc["x"])
        yc_x = _conv_module(ux, w, tm_x)
        x = _mix_mlp(x, mods, x_row, yf_x, attn_x, yc_x, w, mlp_w, l, gf, tm_mlp, final_norm=last)

        if not last:
            attn_c = _attention(qc, kc, vc, None, None, tm_c, tk)
            yf_c = _fourier_seq(zc, fc["ctx"])
            yc_c = _conv_module(uc, w, tm_c)
            ctx = _mix_mlp(ctx, mods, ctx_row, yf_c, attn_c, yc_c, w, mlp_w, l, gf, tm_c, final_norm=False)
    return x
```
